```python
import math
import jax, jax.numpy as jnp
from jax import lax
import numpy as np

D_MODEL = 1024
BATCH = 2
SEQ = 16384
DEPTH = 1
DEC_BATCH = 32
DEC_SEQ = 32
PAST_LEN = 4096

CHUNK = 64
SSD_EXPAND = 2
SSD_INNER = SSD_EXPAND * D_MODEL
SSD_HEAD_DIM = 64
SSD_HEADS = SSD_INNER // SSD_HEAD_DIM
SSD_GROUPS = 4
D_STATE = 128
CONV_W = 4
CONV_DIM = SSD_INNER + 2 * SSD_GROUPS * D_STATE
SSD_BLOCK = 64
N_HEADS = 16
N_KV_HEADS = 4
HEAD_DIM = 64
IDX_HEADS = 8
IDX_DIM = 64
TOPK_MAX = 256
QUERY_BLOCK = 128
REL_BUCKETS = 32
REL_MAX_DIST = 128
D_FF = 4 * D_MODEL
ALPHA = (2.0 * DEPTH) ** 0.25
BETA = (8.0 * DEPTH) ** -0.25
LN_EPS = 1e-5
RMS_EPS = 1e-5
IN_SIZES = (SSD_INNER, CONV_DIM, SSD_HEADS,
            N_HEADS * HEAD_DIM, N_KV_HEADS * HEAD_DIM, N_KV_HEADS * HEAD_DIM,
            IDX_HEADS * IDX_DIM, IDX_DIM, IDX_HEADS,
            D_MODEL, D_MODEL)
IN_WIDTH = sum(IN_SIZES)
IN_SPLITS = tuple(int(v) for v in np.cumsum(IN_SIZES)[:-1])

kernel_name = 'hybrid_ssd_dsa_stream_step'


def layer_norm(x, g, b):
    xf = x.astype(jnp.float32)
    mu = jnp.mean(xf, axis=-1, keepdims=True)
    xc = xf - mu
    var = jnp.mean(xc * xc, axis=-1, keepdims=True)
    return (xc * lax.rsqrt(var + LN_EPS) * g.astype(jnp.float32) + b.astype(jnp.float32)).astype(x.dtype)


def gated_rmsnorm(y, z, w):
    h = (y * jax.nn.silu(z)).astype(jnp.float32)
    hg = h.reshape(h.shape[:-1] + (SSD_GROUPS, SSD_INNER // SSD_GROUPS))
    hg = hg * lax.rsqrt(jnp.mean(hg * hg, axis=-1, keepdims=True) + RMS_EPS)
    return (hg.reshape(h.shape) * w.astype(jnp.float32)).astype(y.dtype)


def t5_bucket(rel):
    half = REL_BUCKETS // 2
    max_exact = half // 2
    n = jnp.abs(rel)
    large = max_exact + (jnp.log(jnp.maximum(n, max_exact).astype(jnp.float32) / max_exact)
                         / math.log(REL_MAX_DIST / max_exact) * (half - max_exact)).astype(jnp.int32)
    large = jnp.minimum(large, half - 1)
    return jnp.where(rel > 0, half, 0) + jnp.where(n < max_exact, n, large)


def ssd_scan(x, dt, A, Bm, Cm, h0):
    b, l, h, p = x.shape
    g, n = Bm.shape[2], Bm.shape[3]
    r = h // g
    q = SSD_BLOCK
    pad = (-l) % q
    def padt(t):
        return jnp.pad(t, [(0, 0), (0, pad)] + [(0, 0)] * (t.ndim - 2))
    x, dt, Bm, Cm = padt(x), padt(dt), padt(Bm), padt(Cm)
    nc = (l + pad) // q
    xc = x.reshape(b, nc, q, g, r, p)
    dtc = dt.reshape(b, nc, q, g, r).astype(jnp.float32)
    Bc = Bm.reshape(b, nc, q, g, n)
    Cc = Cm.reshape(b, nc, q, g, n)
    a = dtc * A.reshape(g, r).astype(jnp.float32)
    acum = jnp.cumsum(a, axis=2)
    at = jnp.moveaxis(acum, 2, -1)
    seg = at[..., :, None] - at[..., None, :]
    causal = jnp.tril(jnp.ones((q, q), dtype=bool))
    decay = jnp.exp(jnp.where(causal, seg, -jnp.inf))
    cb = jnp.einsum('bcqgn,bcsgn->bcgqs', Cc, Bc).astype(jnp.float32)
    m = cb[:, :, :, None] * decay * jnp.moveaxis(dtc, 2, -1)[..., None, :]
    y_diag = jnp.einsum('bcgrqs,bcsgrp->bcqgrp', m, xc)
    w_end = jnp.exp(acum[:, :, -1:] - acum) * dtc
    s_c = jnp.einsum('bcsgn,bcsgrp->bcgrpn', Bc, xc * w_end[..., None])
    chunk_decay = jnp.exp(acum[:, :, -1])

    def step(hc, inp):
        dec, sc = inp
        return hc * dec[..., None, None] + sc, hc

    h_init = h0.reshape(b, g, r, p, n).astype(jnp.float32)
    h_last, h_prev = lax.scan(step, h_init, (jnp.moveaxis(chunk_decay, 1, 0), jnp.moveaxis(s_c, 1, 0)))
    h_prev = jnp.moveaxis(h_prev, 0, 1)
    y_off = jnp.einsum('bcqgn,bcgrpn->bcqgrp', Cc, h_prev) * jnp.exp(acum)[..., None]
    y = (y_diag + y_off).reshape(b, nc * q, h, p)[:, :l]
    return y, h_last.reshape(b, h, p, n)


def dsa_attention(q, qi, wi, k_all, v_all, ki_all, rel_bias, q_start):
    b, t = q.shape[0], q.shape[1]
    s = k_all.shape[1]
    n_sel = min(TOPK_MAX, s // 4)
    qb = QUERY_BLOCK if t % QUERY_BLOCK == 0 else t
    nb = t // qb
    grp = N_HEADS // N_KV_HEADS
    key_pos = jnp.arange(s, dtype=jnp.int32)
    bidx = jnp.arange(b)[:, None, None]

    def blocks(arr):
        return jnp.moveaxis(arr.reshape((b, nb, qb) + arr.shape[2:]), 1, 0)

    q_pos = (q_start + jnp.arange(t, dtype=jnp.int32)).reshape(nb, qb)

    def one_block(args):
        q_b, qi_b, wi_b, pos_b = args
        visible_end = (pos_b // CHUNK + 1) * CHUNK
        dots = jnp.einsum('bthd,bsd->bths', qi_b, ki_all).astype(jnp.float32) * IDX_DIM ** -0.5
        score = jnp.einsum('bth,bths->bts', wi_b.astype(jnp.float32) * IDX_HEADS ** -0.5, jax.nn.relu(dots))
        visible = key_pos[None, :] < visible_end[:, None]
        score = jnp.where(visible[None], score, -jnp.inf)
        _, sel = lax.top_k(score, n_sel)
        valid = sel < visible_end[None, :, None]
        k_sel = k_all[bidx, sel]
        v_sel = v_all[bidx, sel]
        qg = q_b.reshape(b, qb, N_KV_HEADS, grp, HEAD_DIM)
        logits = jnp.einsum('btkgd,btskd->btkgs', qg, k_sel).astype(jnp.float32) * HEAD_DIM ** -0.5
        bias = rel_bias[t5_bucket(sel - pos_b[None, :, None])]
        bias = jnp.moveaxis(bias, 2, 3).reshape(b, qb, N_KV_HEADS, grp, n_sel)
        logits = jnp.where(valid[:, :, None, None, :], logits + bias.astype(jnp.float32), -jnp.inf)
        probs = jax.nn.softmax(logits, axis=-1).astype(v_sel.dtype)
        out = jnp.einsum('btkgs,btskd->btkgd', probs, v_sel)
        return out.reshape(b, qb, N_HEADS * HEAD_DIM)

    out = lax.map(one_block, (blocks(q), blocks(qi), blocks(wi), q_pos))
    return jnp.moveaxis(out, 0, 1).reshape(b, t, N_HEADS * HEAD_DIM)


def hybrid_layer(x, cache_k, cache_v, cache_kidx, state_ssm, state_conv, rel_bias,
                 w_in, conv_w, conv_b, dt_bias, a_log, d_skip, ssd_norm_w, w_ssd_o, w_attn_o, w_out,
                 ln1_g, ln1_b, w_up, w_down, ln2_g, ln2_b):
    b, t, _ = x.shape
    past = cache_k.shape[1]
    proj = x @ w_in
    z, xbc, dt, q, k, v, qi, ki, wi, g_ssd, g_attn = jnp.split(proj, IN_SPLITS, axis=-1)

    xbc_pad = jnp.concatenate([state_conv.astype(xbc.dtype), xbc], axis=1)
    new_conv = xbc_pad[:, -(CONV_W - 1):]
    conv = conv_b
    for i in range(CONV_W):
        conv = conv + xbc_pad[:, i:i + t] * conv_w[i]
    xbc_act = jax.nn.silu(conv)
    xs, Bm, Cm = jnp.split(xbc_act, (SSD_INNER, SSD_INNER + SSD_GROUPS * D_STATE), axis=-1)
    xs = xs.reshape(b, t, SSD_HEADS, SSD_HEAD_DIM)
    Bm = Bm.reshape(b, t, SSD_GROUPS, D_STATE)
    Cm = Cm.reshape(b, t, SSD_GROUPS, D_STATE)
    dt_pos = jax.nn.softplus(dt.astype(jnp.float32) + dt_bias.astype(jnp.float32))
    A = -jnp.exp(a_log.astype(jnp.float32))
    y_ssd, new_ssm = ssd_scan(xs, dt_pos, A, Bm, Cm, state_ssm)
    y_ssd = (y_ssd + d_skip[:, None] * xs).astype(x.dtype).reshape(b, t, SSD_INNER)
    y_ssd = gated_rmsnorm(y_ssd, z, ssd_norm_w)
    y1 = y_ssd @ w_ssd_o

    k_new = k.reshape(b, t, N_KV_HEADS, HEAD_DIM)
    v_new = v.reshape(b, t, N_KV_HEADS, HEAD_DIM)
    k_all = jnp.concatenate([cache_k.astype(k_new.dtype), k_new], axis=1)
    v_all = jnp.concatenate([cache_v.astype(v_new.dtype), v_new], axis=1)
    ki_all = jnp.concatenate([cache_kidx.astype(ki.dtype), ki], axis=1)
    o = dsa_attention(q.reshape(b, t, N_HEADS, HEAD_DIM), qi.reshape(b, t, IDX_HEADS, IDX_DIM), wi,
                      k_all, v_all, ki_all, rel_bias, past)
    y2 = o @ w_attn_o

    mixed = (jax.nn.sigmoid(g_ssd) * y1 + jax.nn.sigmoid(g_attn) * y2) @ w_out
    h = layer_norm(ALPHA * x + mixed, ln1_g, ln1_b)
    f = jnp.square(jax.nn.relu(h @ w_up)) @ w_down
    out = layer_norm(ALPHA * h + f, ln2_g, ln2_b)
    return out, (k_new, v_new, ki, new_ssm.astype(x.dtype), new_conv)


def setup_inputs(seed: int = 0) -> dict:
    key = jax.random.key(seed)
    ks = jax.random.split(key, 32)
    f32 = jnp.float32

    def nrm(k, shape, scale):
        return jax.random.normal(k, shape, f32) * scale

    L = DEPTH
    dt0 = jnp.exp(jax.random.uniform(ks[11], (L, SSD_HEADS), f32, math.log(1e-3), math.log(1e-1)))
    return {
        'x_prompt': nrm(ks[0], (BATCH, SEQ, D_MODEL), 1.0),
        'x_sample': nrm(ks[1], (DEC_BATCH, DEC_SEQ, D_MODEL), 1.0),
        'cache_k': nrm(ks[2], (L, DEC_BATCH, PAST_LEN, N_KV_HEADS, HEAD_DIM), 1.0),
        'cache_v': nrm(ks[3], (L, DEC_BATCH, PAST_LEN, N_KV_HEADS, HEAD_DIM), 1.0),
        'cache_kidx': nrm(ks[4], (L, DEC_BATCH, PAST_LEN, IDX_DIM), 1.0),
        'state_ssm': nrm(ks[5], (L, DEC_BATCH, SSD_HEADS, SSD_HEAD_DIM, D_STATE), 0.1),
        'state_conv': nrm(ks[6], (L, DEC_BATCH, CONV_W - 1, CONV_DIM), 1.0),
        'rel_bias': nrm(ks[7], (REL_BUCKETS, N_HEADS), 0.5),
        'w_in': nrm(ks[8], (L, D_MODEL, IN_WIDTH), D_MODEL ** -0.5),
        'conv_w': nrm(ks[9], (L, CONV_W, CONV_DIM), CONV_W ** -0.5),
        'conv_b': nrm(ks[10], (L, CONV_DIM), 0.01),
        'dt_bias': dt0 + jnp.log(-jnp.expm1(-dt0)),
        'a_log': jnp.log(jax.random.uniform(ks[12], (L, SSD_HEADS), f32, 1.0, 16.0)),
        'd_skip': 1.0 + nrm(ks[13], (L, SSD_HEADS), 0.01),
        'ssd_norm_w': 1.0 + nrm(ks[14], (L, SSD_INNER), 0.01),
        'w_ssd_o': nrm(ks[15], (L, SSD_INNER, D_MODEL), SSD_INNER ** -0.5 * BETA),
        'w_attn_o': nrm(ks[16], (L, N_HEADS * HEAD_DIM, D_MODEL), (N_HEADS * HEAD_DIM) ** -0.5 * BETA),
        'w_out': nrm(ks[17], (L, D_MODEL, D_MODEL), D_MODEL ** -0.5 * BETA),
        'ln1_g': 1.0 + nrm(ks[18], (L, D_MODEL), 0.01),
        'ln1_b': nrm(ks[19], (L, D_MODEL), 0.01),
        'w_up': nrm(ks[20], (L, D_MODEL, D_FF), D_MODEL ** -0.5),
        'w_down': nrm(ks[21], (L, D_FF, D_MODEL), D_FF ** -0.5 * BETA),
        'ln2_g': 1.0 + nrm(ks[22], (L, D_MODEL), 0.01),
        'ln2_b': nrm(ks[23], (L, D_MODEL), 0.01),
    }


def reference(x_prompt, x_sample, cache_k, cache_v, cache_kidx, state_ssm, state_conv, rel_bias,
              w_in, conv_w, conv_b, dt_bias, a_log, d_skip, ssd_norm_w, w_ssd_o, w_attn_o, w_out,
              ln1_g, ln1_b, w_up, w_down, ln2_g, ln2_b):
    bp = x_prompt.shape[0]
    dtype = x_prompt.dtype
    empty_k = jnp.zeros((bp, 0, N_KV_HEADS, HEAD_DIM), dtype)
    empty_ki = jnp.zeros((bp, 0, IDX_DIM), dtype)
    zero_ssm = jnp.zeros((bp, SSD_HEADS, SSD_HEAD_DIM, D_STATE), dtype)
    zero_conv = jnp.zeros((bp, CONV_W - 1, CONV_DIM), dtype)
    yp, ys = x_prompt, x_sample
    st_p, st_s = [], []
    for l in range(DEPTH):
        yp, sp = hybrid_layer(yp, empty_k, empty_k, empty_ki, zero_ssm, zero_conv, rel_bias,
                              w_in[l], conv_w[l], conv_b[l], dt_bias[l], a_log[l], d_skip[l], ssd_norm_w[l],
                              w_ssd_o[l], w_attn_o[l], w_out[l], ln1_g[l], ln1_b[l], w_up[l], w_down[l],
                              ln2_g[l], ln2_b[l])
        ys, ss = hybrid_layer(ys, cache_k[l], cache_v[l], cache_kidx[l], state_ssm[l], state_conv[l], rel_bias,
                              w_in[l], conv_w[l], conv_b[l], dt_bias[l], a_log[l], d_skip[l], ssd_norm_w[l],
                              w_ssd_o[l], w_attn_o[l], w_out[l], ln1_g[l], ln1_b[l], w_up[l], w_down[l],
                              ln2_g[l], ln2_b[l])
        st_p.append(sp)
        st_s.append(ss)
    k_p = jnp.stack([s[0] for s in st_p])
    v_p = jnp.stack([s[1] for s in st_p])
    ki_p = jnp.stack([s[2] for s in st_p])
    ssm_p = jnp.stack([s[3] for s in st_p])
    conv_p = jnp.stack([s[4] for s in st_p])
    k_s = jnp.stack([s[0] for s in st_s])
    v_s = jnp.stack([s[1] for s in st_s])
    ki_s = jnp.stack([s[2] for s in st_s])
    ssm_s = jnp.stack([s[3] for s in st_s])
    conv_s = jnp.stack([s[4] for s in st_s])
    return (yp, ys, k_p, v_p, ki_p, ssm_p, conv_p, k_s, v_s, ki_s, ssm_s, conv_s)
```

```python
import functools
import math

import jax
import jax.numpy as jnp
from jax import lax
from jax.experimental import pallas as pl
from jax.experimental.pallas import tpu as pltpu

F32 = jnp.float32
I32 = jnp.int32
MXU_DTYPE = jnp.bfloat16

D_MODEL = 1024
CHUNK = 64
SSD_INNER = 2048
SSD_HEAD_DIM = 64
SSD_HEADS = 32
SSD_GROUPS = 4
HEADS_PER_GROUP = SSD_HEADS // SSD_GROUPS
D_STATE = 128
CONV_W = 4
CONV_DIM = SSD_INNER + 2 * SSD_GROUPS * D_STATE
N_HEADS = 16
N_KV_HEADS = 4
Q_PER_KV = N_HEADS // N_KV_HEADS
HEAD_DIM = 64
KV_DIM = N_KV_HEADS * HEAD_DIM
IDX_HEADS = 8
IDX_DIM = 64
TOPK_MAX = 256
REL_BUCKETS = 32
REL_MAX_DIST = 128
D_FF = 4 * D_MODEL
DEPTH = 1
ALPHA = (2.0 * DEPTH) ** 0.25
LN_EPS = 1e-5
RMS_EPS = 1e-5
IN_SIZES = (SSD_INNER, CONV_DIM, SSD_HEADS, N_HEADS * HEAD_DIM, KV_DIM, KV_DIM,
            IDX_HEADS * IDX_DIM, IDX_DIM, IDX_HEADS, D_MODEL, D_MODEL)

LANES = 128
VMEM_LIMIT_BYTES = 56 * 1024 * 1024

SM_KI = 0
SM_DT = IDX_DIM
SM_WI = IDX_DIM + SSD_HEADS
SM_PAD = LANES - (IDX_DIM + SSD_HEADS + IDX_HEADS)

NEG_BIG = -1e30
INT_MIN = -2 ** 31
KEY_NEG_INF = -2139095041


def _cparams(n_axes):
    return pltpu.CompilerParams(dimension_semantics=("arbitrary",) * n_axes,
                                vmem_limit_bytes=VMEM_LIMIT_BYTES)


def _resident(shape, index_map):
    return pl.BlockSpec(shape, index_map, pipeline_mode=pl.Buffered(1))


def _dot(a, b):
    return jnp.dot(a.astype(MXU_DTYPE), b.astype(MXU_DTYPE), preferred_element_type=F32)


def _dot_exact(a, b):
    return jnp.dot(a, b, preferred_element_type=F32, precision=lax.Precision.HIGHEST)


def _sigmoid(x):
    return 1.0 / (1.0 + jnp.exp(-x))


def _softplus(x):
    return jnp.maximum(x, 0.0) + jnp.log1p(jnp.exp(-jnp.abs(x)))


_PROJ_WIDTHS = (SSD_INNER, CONV_DIM, N_HEADS * HEAD_DIM, KV_DIM, KV_DIM, IDX_HEADS * IDX_DIM,
                D_MODEL, D_MODEL, LANES)
_PROJ_WIDTH = sum(_PROJ_WIDTHS)
_PROJ_CHUNK = 512


def _prep_w_in(w_in):
    offs = [0]
    for s in IN_SIZES:
        offs.append(offs[-1] + s)
    z, xbc, dt, q, k, v, qi, ki, wi, gs, ga = [w_in[:, offs[i]:offs[i + 1]] for i in range(len(IN_SIZES))]
    small = jnp.concatenate([ki, dt, wi, jnp.zeros((D_MODEL, SM_PAD), w_in.dtype)], axis=1)
    w = jnp.concatenate([z, xbc, q * HEAD_DIM ** -0.5, k, v, qi * IDX_DIM ** -0.5, gs, ga, small], axis=1)
    return w.astype(MXU_DTYPE)


def _inproj_kernel(x_ref, w_ref, z_ref, xbc_ref, q_ref, k_ref, v_ref, qi_ref, gs_ref, ga_ref, sm_ref):
    xb = x_ref[...].astype(MXU_DTYPE)
    outs = (z_ref, xbc_ref, q_ref, k_ref, v_ref, qi_ref, gs_ref, ga_ref, sm_ref)
    c0 = 0
    for ref, width in zip(outs, _PROJ_WIDTHS):
        for o in range(0, width, _PROJ_CHUNK):
            n = min(_PROJ_CHUNK, width - o)
            r = jnp.dot(xb, w_ref[:, c0 + o:c0 + o + n], preferred_element_type=F32)
            ref[:, o:o + n] = r.astype(ref.dtype)
        c0 += width


def _inproj(x2d, w_prep, tm):
    m = x2d.shape[0]
    dtypes = (F32, F32, MXU_DTYPE, F32, F32, MXU_DTYPE, F32, F32, F32)
    out_shape = tuple(jax.ShapeDtypeStruct((m, w), d) for w, d in zip(_PROJ_WIDTHS, dtypes))
    out_specs = tuple(pl.BlockSpec((tm, w), lambda i: (i, 0)) for w in _PROJ_WIDTHS)
    return pl.pallas_call(
        _inproj_kernel,
        grid=(m // tm,),
        in_specs=[pl.BlockSpec((tm, D_MODEL), lambda i: (i, 0)),
                  _resident((D_MODEL, _PROJ_WIDTH), lambda i: (0, 0))],
        out_specs=out_specs,
        out_shape=out_shape,
        compiler_params=_cparams(1),
        name="inproj",
    )(x2d, w_prep)


_HALO = CONV_W - 1
_XP_BASE = 8


def _ssd_kernel(xbc_ref, z_ref, sm_ref, dtT_ref, conv0_ref, ssm0_ref, convw_ref, convb_ref,
                hp_ref, hpT_ref, dfull_ref, nw_ref, expand_ref,
                yn_ref, ssm_out_ref, xp_ref, st_ref, y_ref, *, lc, t_valid):
    c = pl.program_id(1)
    nc = pl.num_programs(1)

    @pl.when(c == 0)
    def _():
        xp_ref[_XP_BASE - _HALO:_XP_BASE, :] = conv0_ref[0]
        st_ref[...] = ssm0_ref[0]

    xp_ref[_XP_BASE:_XP_BASE + lc, :] = xbc_ref[0]
    conv = convb_ref[...]
    for i in range(CONV_W):
        conv = conv + xp_ref[_XP_BASE - _HALO + i:_XP_BASE - _HALO + i + lc, :] * convw_ref[i:i + 1, :]
    xp_ref[_XP_BASE - _HALO:_XP_BASE, :] = xp_ref[_XP_BASE + lc - _HALO:_XP_BASE + lc, :]
    act = conv * _sigmoid(conv)
    xs = act[:, :SSD_INNER]
    bm = act[:, SSD_INNER:SSD_INNER + SSD_GROUPS * D_STATE]
    cm = act[:, SSD_INNER + SSD_GROUPS * D_STATE:]

    dt = _softplus(sm_ref[0][:, SM_DT:SM_DT + SSD_HEADS] + hp_ref[0:1, :])
    dt_t = _softplus(dtT_ref[0] + hpT_ref[:, 0:1])
    if t_valid < lc:
        dt = jnp.where(lax.broadcasted_iota(I32, (lc, SSD_HEADS), 0) < t_valid, dt, 0.0)
        dt_t = jnp.where(lax.broadcasted_iota(I32, (SSD_HEADS, lc), 1) < t_valid, dt_t, 0.0)
    a = dt * (-jnp.exp(hp_ref[1:2, :]))
    a_t = dt_t * (-jnp.exp(hpT_ref[:, 1:2]))
    ri = lax.broadcasted_iota(I32, (lc, lc), 0)
    ci = lax.broadcasted_iota(I32, (lc, lc), 1)
    causal = ci <= ri
    acum = _dot_exact(causal.astype(F32), a)
    acum_t = _dot_exact(a_t, (ri <= ci).astype(F32))
    e_a = jnp.exp(acum)
    w_end = jnp.exp(acum[lc - 1:lc, :] - acum) * dt
    e_a_full = _dot_exact(e_a, expand_ref[...])
    w_end_full = _dot_exact(w_end, expand_ref[...])
    xw = xs * w_end_full
    lane = lax.broadcasted_iota(I32, (lc, 2 * SSD_HEAD_DIM), 1)
    gw = HEADS_PER_GROUP * SSD_HEAD_DIM

    for g in range(SSD_GROUPS):
        b_g = bm[:, g * D_STATE:(g + 1) * D_STATE]
        c_g = cm[:, g * D_STATE:(g + 1) * D_STATE].astype(MXU_DTYPE)
        b_gt = b_g.T.astype(MXU_DTYPE)
        cb = jnp.dot(c_g, b_gt, preferred_element_type=F32)
        state = st_ref[g]
        y_off = _dot(c_g, state) * e_a_full[:, g * gw:(g + 1) * gw]
        st_ref[g] = state * e_a_full[lc - 1:lc, g * gw:(g + 1) * gw] + _dot(b_gt, xw[:, g * gw:(g + 1) * gw])
        for jp in range(HEADS_PER_GROUP // 2):
            h0 = g * HEADS_PER_GROUP + 2 * jp
            ms = []
            for h in (h0, h0 + 1):
                seg = acum[:, h:h + 1] - acum_t[h:h + 1, :]
                decay = jnp.exp(jnp.where(causal, seg, -jnp.inf))
                ms.append((cb * decay * dt_t[h:h + 1, :]).astype(MXU_DTYPE))
            c0 = h0 * SSD_HEAD_DIM
            x_pair = xs[:, c0:c0 + 2 * SSD_HEAD_DIM]
            r = jnp.dot(jnp.concatenate(ms, axis=0), x_pair.astype(MXU_DTYPE), preferred_element_type=F32)
            y_diag = jnp.where(lane < SSD_HEAD_DIM, r[:lc], r[lc:])
            o0 = 2 * jp * SSD_HEAD_DIM
            y_ref[:, c0:c0 + 2 * SSD_HEAD_DIM] = (y_diag + y_off[:, o0:o0 + 2 * SSD_HEAD_DIM]
                                                  + dfull_ref[:, c0:c0 + 2 * SSD_HEAD_DIM] * x_pair)

    zv = z_ref[0]
    hg = y_ref[...] * (zv * _sigmoid(zv))
    for g in range(SSD_GROUPS):
        sl = hg[:, g * gw:(g + 1) * gw]
        ms = jnp.mean(sl * sl, axis=-1, keepdims=True)
        yn_ref[0, :, g * gw:(g + 1) * gw] = (sl * lax.rsqrt(ms + RMS_EPS) * nw_ref[:, g * gw:(g + 1) * gw]).astype(yn_ref.dtype)

    @pl.when(c == nc - 1)
    def _():
        ssm_out_ref[0] = st_ref[...]


def _ssd(xbc, z, sm, dt_t, conv0, ssm0_t, conv_w, conv_b, dt_bias, a_log, d_skip, norm_w, lc, t_valid):
    b, t, _ = xbc.shape
    nc = t // lc
    hp = jnp.stack([dt_bias, a_log]).astype(F32)
    d_full = jnp.repeat(d_skip.astype(F32), SSD_HEAD_DIM)[None, :]
    expand = jnp.repeat(jnp.eye(SSD_HEADS, dtype=F32), SSD_HEAD_DIM, axis=1)
    gw = HEADS_PER_GROUP * SSD_HEAD_DIM
    const2 = lambda bi, ci: (0, 0)
    kern = functools.partial(_ssd_kernel, lc=lc, t_valid=t_valid)
    return pl.pallas_call(
        kern,
        grid=(b, nc),
        in_specs=[pl.BlockSpec((1, lc, CONV_DIM), lambda bi, ci: (bi, ci, 0)),
                  pl.BlockSpec((1, lc, SSD_INNER), lambda bi, ci: (bi, ci, 0)),
                  pl.BlockSpec((1, lc, LANES), lambda bi, ci: (bi, ci, 0)),
                  pl.BlockSpec((1, SSD_HEADS, lc), lambda bi, ci: (bi, 0, ci)),
                  pl.BlockSpec((1, _HALO, CONV_DIM), lambda bi, ci: (bi, 0, 0)),
                  pl.BlockSpec((1, SSD_GROUPS, D_STATE, gw), lambda bi, ci: (bi, 0, 0, 0)),
                  pl.BlockSpec((CONV_W, CONV_DIM), const2),
                  pl.BlockSpec((1, CONV_DIM), const2),
                  pl.BlockSpec((2, SSD_HEADS), const2),
                  pl.BlockSpec((SSD_HEADS, 2), const2),
                  pl.BlockSpec((1, SSD_INNER), const2),
                  pl.BlockSpec((1, SSD_INNER), const2),
                  pl.BlockSpec((SSD_HEADS, SSD_INNER), const2)],
        out_specs=(pl.BlockSpec((1, lc, SSD_INNER), lambda bi, ci: (bi, ci, 0)),
                   pl.BlockSpec((1, SSD_GROUPS, D_STATE, gw), lambda bi, ci: (bi, 0, 0, 0))),
        out_shape=(jax.ShapeDtypeStruct((b, t, SSD_INNER), MXU_DTYPE),
                   jax.ShapeDtypeStruct((b, SSD_GROUPS, D_STATE, gw), F32)),
        scratch_shapes=[pltpu.VMEM((_XP_BASE + lc, CONV_DIM), F32),
                        pltpu.VMEM((SSD_GROUPS, D_STATE, gw), F32),
                        pltpu.VMEM((lc, SSD_INNER), F32)],
        compiler_params=_cparams(2),
        name="ssd",
    )(xbc, z, sm, dt_t, conv0, ssm0_t, conv_w.astype(F32), conv_b.astype(F32)[None, :], hp, hp.T,
      d_full, norm_w.astype(F32)[None, :], expand)


def _state_to_kernel_layout(s):
    b = s.shape[0]
    s = s.reshape(b, SSD_GROUPS, HEADS_PER_GROUP, SSD_HEAD_DIM, D_STATE)
    return s.transpose(0, 1, 4, 2, 3).reshape(b, SSD_GROUPS, D_STATE, HEADS_PER_GROUP * SSD_HEAD_DIM)


def _state_from_kernel_layout(s):
    b = s.shape[0]
    s = s.reshape(b, SSD_GROUPS, D_STATE, HEADS_PER_GROUP, SSD_HEAD_DIM)
    return s.transpose(0, 1, 3, 4, 2).reshape(b, SSD_HEADS, SSD_HEAD_DIM, D_STATE)


_T5_LOG_STARTS = (12, 16, 23, 32, 46, 64, 91)
_T5_FAR_BUCKET = REL_BUCKETS // 2 - 1


def _bias_kernel(rb_ref, out_ref, *, tk):
    half = REL_BUCKETS // 2
    max_exact = half // 2
    ri = lax.broadcasted_iota(I32, (tk, tk), 0)
    ci = lax.broadcasted_iota(I32, (tk, tk), 1)
    for t in range(2):
        rel = ci - ri - (1 - t) * tk
        n = jnp.abs(rel)
        large = jnp.full((tk, tk), max_exact, I32)
        for start in _T5_LOG_STARTS:
            large = large + (n >= start).astype(I32)
        bucket = jnp.where(rel > 0, half, 0) + jnp.where(n < max_exact, n, large)
        for h in range(N_HEADS):
            acc = jnp.zeros((tk, tk), F32)
            for bkt in range(REL_BUCKETS):
                acc = jnp.where(bucket == bkt, rb_ref[bkt, h], acc)
            out_ref[t, h] = acc - rb_ref[_T5_FAR_BUCKET, h]


def _bias_tiles(rel_bias, tk):
    return pl.pallas_call(
        functools.partial(_bias_kernel, tk=tk),
        in_specs=[pl.BlockSpec(memory_space=pltpu.SMEM)],
        out_specs=pl.BlockSpec(memory_space=pltpu.VMEM),
        out_shape=jax.ShapeDtypeStruct((2, N_HEADS, tk, tk), F32),
        name="bias_tiles",
    )(rel_bias.astype(F32))


def _select_kernel(qi_ref, sm_ref, kit_ref, mask_ref, key_ref, *, tq, tk, s_len, q_start, n_sel):
    qb = pl.program_id(1)
    n_blocks = mask_ref.shape[2] // tk
    p0 = q_start + qb * tq
    pos = p0 + lax.broadcasted_iota(I32, (tq, 1), 0)
    vis_end = (pos // CHUNK + 1) * CHUNK
    vis_max = jnp.minimum(((p0 + tq - 1) // CHUNK + 1) * CHUNK, s_len)
    nkb = (vis_max + tk - 1) // tk

    wi = sm_ref[0][:, SM_WI:SM_WI + IDX_HEADS] * IDX_HEADS ** -0.5
    qi = qi_ref[0]
    lane_k = lax.broadcasted_iota(I32, (1, tk), 1)

    def score_body(j, carry):
        k0 = pl.multiple_of(j * tk, tk)
        kt = kit_ref[0, :, pl.ds(k0, tk)]
        sc = jnp.zeros((tq, tk), F32)
        for hh in range(IDX_HEADS):
            d = jnp.dot(qi[:, hh * IDX_DIM:(hh + 1) * IDX_DIM], kt, preferred_element_type=F32)
            sc = sc + wi[:, hh:hh + 1] * jnp.maximum(d, 0.0)
        sc = jnp.where(sc == 0.0, 0.0, sc)
        key_pos = k0 + lane_k
        sc = jnp.where((key_pos < vis_end) & (key_pos < s_len), sc, -jnp.inf)
        bits = pltpu.bitcast(sc, I32)
        key_ref[:, pl.ds(k0, tk)] = jnp.where(bits < 0, bits ^ 0x7FFFFFFF, bits)
        return carry

    lax.fori_loop(0, nkb, score_body, 0)

    def count(pred, thr):
        thr_b = jnp.broadcast_to(thr, (tq, LANES))

        def body(j, acc):
            k0 = pl.multiple_of(j * tk, tk)
            for u in range(tk // LANES):
                blk = key_ref[:, pl.ds(k0 + u * LANES, LANES)]
                acc = acc + pred(blk, thr_b).astype(I32)
            return acc

        acc = lax.fori_loop(0, nkb, body, jnp.zeros((tq, LANES), I32))
        return jnp.sum(acc, axis=1, keepdims=True)

    ge = lambda x, y: x >= y
    base = jnp.where(count(ge, jnp.zeros((tq, 1), I32)) >= n_sel, 0, INT_MIN).astype(I32)

    def bit_body(i, base):
        cand = base | lax.shift_left(jnp.int32(1), 30 - i)
        return jnp.where(count(ge, cand) >= n_sel, cand, base)

    tau = lax.fori_loop(0, 31, bit_body, base)
    n_gt = count(lambda x, y: x > y, tau)
    n_eq = count(lambda x, y: x == y, tau)
    need = n_sel - n_gt
    tie_rows = (n_eq > need) & (tau > KEY_NEG_INF)
    has_ties = jnp.max(tie_rows.astype(I32)) > 0
    tau_vis = jnp.maximum(tau, KEY_NEG_INF + 1)

    @pl.when(jnp.logical_not(has_ties))
    def _():
        def body(j, carry):
            k0 = pl.multiple_of(j * tk, tk)
            mask_ref[0, :, pl.ds(k0, tk)] = (key_ref[:, pl.ds(k0, tk)] >= tau_vis).astype(I32).astype(mask_ref.dtype)
            return carry
        lax.fori_loop(0, nkb, body, 0)

    @pl.when(has_ties)
    def _():
        upper = (lax.broadcasted_iota(I32, (tk, tk), 0) <= lax.broadcasted_iota(I32, (tk, tk), 1)).astype(MXU_DTYPE)
        need_f = need.astype(F32)

        def body(j, seen):
            k0 = pl.multiple_of(j * tk, tk)
            blk = key_ref[:, pl.ds(k0, tk)]
            eq = blk == tau
            rank = seen + jnp.dot(eq.astype(F32).astype(MXU_DTYPE), upper, preferred_element_type=F32)
            sel = ((blk > tau) | (eq & (rank <= need_f))) & (blk > KEY_NEG_INF)
            mask_ref[0, :, pl.ds(k0, tk)] = sel.astype(I32).astype(mask_ref.dtype)
            return seen + jnp.sum(eq.astype(F32), axis=1, keepdims=True)
        lax.fori_loop(0, nkb, body, jnp.zeros((tq, 1), F32))

    def zero_body(j, carry):
        k0 = pl.multiple_of(j * tk, tk)
        mask_ref[0, :, pl.ds(k0, tk)] = jnp.zeros((tq, tk), mask_ref.dtype)
        return carry

    lax.fori_loop(nkb, n_blocks, zero_body, 0)


def _select(qi, sm, kit, tq, tk, s_len, q_start, n_sel):
    b, t, _ = qi.shape
    s_pad = kit.shape[2]
    kern = functools.partial(_select_kernel, tq=tq, tk=tk, s_len=s_len, q_start=q_start, n_sel=n_sel)
    return pl.pallas_call(
        kern,
        grid=(b, t // tq),
        in_specs=[pl.BlockSpec((1, tq, IDX_HEADS * IDX_DIM), lambda bi, qi_: (bi, qi_, 0)),
                  pl.BlockSpec((1, tq, LANES), lambda bi, qi_: (bi, qi_, 0)),
                  _resident((1, IDX_DIM, s_pad), lambda bi, qi_: (bi, 0, 0))],
        out_specs=pl.BlockSpec((1, tq, s_pad), lambda bi, qi_: (bi, qi_, 0)),
        out_shape=jax.ShapeDtypeStruct((b, t, s_pad), jnp.int8),
        scratch_shapes=[pltpu.VMEM((tq, s_pad), I32)],
        compiler_params=_cparams(2),
        name="select",
    )(qi, sm, kit)


def _attn_kernel(q_ref, kt_ref, v_ref, mask_ref, bias_ref, o_ref, m_ref, l_ref, acc_ref, qg_ref,
                 *, tq, tk, q_start):
    qb = pl.program_id(1)
    jd = (q_start + qb * tq) // tk
    rows = Q_PER_KV * tq

    for g in range(N_KV_HEADS):
        for r in range(Q_PER_KV):
            c0 = (g * Q_PER_KV + r) * HEAD_DIM
            qg_ref[g, r * tq:(r + 1) * tq, :] = q_ref[0, :, c0:c0 + HEAD_DIM]
    m_ref[...] = jnp.full(m_ref.shape, NEG_BIG, F32)
    l_ref[...] = jnp.zeros(l_ref.shape, F32)
    acc_ref[...] = jnp.zeros(acc_ref.shape, F32)

    def process(j, near):
        k0 = pl.multiple_of(j * tk, tk)
        mb = jnp.where(mask_ref[0, :, pl.ds(k0, tk)].astype(I32) != 0, 0.0, NEG_BIG)
        mb = jnp.concatenate([mb] * Q_PER_KV, axis=0)
        for g in range(N_KV_HEADS):
            kt = kt_ref[0, g * HEAD_DIM:(g + 1) * HEAD_DIM, pl.ds(k0, tk)]
            s = jnp.dot(qg_ref[g], kt, preferred_element_type=F32) + mb
            if near is not None:
                s = s + bias_ref[near, g]
            m_old = m_ref[g]
            m_new = jnp.maximum(m_old, jnp.max(s, axis=1, keepdims=True))
            alpha = jnp.exp(m_old - m_new)
            p = jnp.exp(s - m_new)
            l_ref[g] = alpha * l_ref[g] + jnp.sum(p, axis=1, keepdims=True)
            vt = v_ref[0, pl.ds(k0, tk), g * HEAD_DIM:(g + 1) * HEAD_DIM]
            acc_ref[g] = alpha * acc_ref[g] + jnp.dot(p.astype(MXU_DTYPE), vt, preferred_element_type=F32)
            m_ref[g] = m_new

    def far_body(j, carry):
        process(j, None)
        return carry

    lax.fori_loop(0, jnp.maximum(jd - 1, 0), far_body, 0)

    @pl.when(jd >= 1)
    def _():
        process(jd - 1, 0)

    process(jd, 1)

    for g in range(N_KV_HEADS):
        o = acc_ref[g] / l_ref[g]
        for r in range(Q_PER_KV):
            c0 = (g * Q_PER_KV + r) * HEAD_DIM
            o_ref[0, :, c0:c0 + HEAD_DIM] = o[r * tq:(r + 1) * tq].astype(o_ref.dtype)


def _attention(q, kt, v, mask, bias, tq, tk, q_start):
    b, t, _ = q.shape
    s_pad = kt.shape[2]
    rows = Q_PER_KV * tq
    kern = functools.partial(_attn_kernel, tq=tq, tk=tk, q_start=q_start)
    return pl.pallas_call(
        kern,
        grid=(b, t // tq),
        in_specs=[pl.BlockSpec((1, tq, N_HEADS * HEAD_DIM), lambda bi, qi_: (bi, qi_, 0)),
                  _resident((1, KV_DIM, s_pad), lambda bi, qi_: (bi, 0, 0)),
                  _resident((1, s_pad, KV_DIM), lambda bi, qi_: (bi, 0, 0)),
                  pl.BlockSpec((1, tq, s_pad), lambda bi, qi_: (bi, qi_, 0)),
                  _resident((2, N_KV_HEADS, rows, tk), lambda bi, qi_: (0, 0, 0, 0))],
        out_specs=pl.BlockSpec((1, tq, N_HEADS * HEAD_DIM), lambda bi, qi_: (bi, qi_, 0)),
        out_shape=jax.ShapeDtypeStruct((b, t, N_HEADS * HEAD_DIM), MXU_DTYPE),
        scratch_shapes=[pltpu.VMEM((N_KV_HEADS, rows, 1), F32),
                        pltpu.VMEM((N_KV_HEADS, rows, 1), F32),
                        pltpu.VMEM((N_KV_HEADS, rows, HEAD_DIM), F32),
                        pltpu.VMEM((N_KV_HEADS, rows, HEAD_DIM), MXU_DTYPE)],
        compiler_params=_cparams(2),
        name="attention",
    )(q, kt, v, mask, bias)


def _layer_norm(x, g, b):
    mu = jnp.mean(x, axis=-1, keepdims=True)
    xc = x - mu
    var = jnp.mean(xc * xc, axis=-1, keepdims=True)
    return xc * lax.rsqrt(var + LN_EPS) * g + b


def _post_kernel(o_ref, yn_ref, gs_ref, ga_ref, x_ref, wso_ref, wao_ref, wout_ref, wup_ref, wdown_ref,
                 ln_ref, out_ref):
    y1 = jnp.dot(yn_ref[...], wso_ref[...], preferred_element_type=F32)
    y2 = jnp.dot(o_ref[...], wao_ref[...], preferred_element_type=F32)
    mix = _sigmoid(gs_ref[...]) * y1 + _sigmoid(ga_ref[...]) * y2
    mixed = _dot(mix, wout_ref[...])
    h = _layer_norm(ALPHA * x_ref[...] + mixed, ln_ref[0:1, :], ln_ref[1:2, :])
    u = jnp.maximum(_dot(h, wup_ref[...]), 0.0)
    f = _dot(u * u, wdown_ref[...])
    out_ref[...] = _layer_norm(ALPHA * h + f, ln_ref[2:3, :], ln_ref[3:4, :])


def _post(o, yn, gs, ga, x2d, w_ssd_o, w_attn_o, w_out, w_up, w_down, ln, tm):
    m = x2d.shape[0]
    row = lambda w: pl.BlockSpec((tm, w), lambda i: (i, 0))
    const = lambda i: (0, 0)
    return pl.pallas_call(
        _post_kernel,
        grid=(m // tm,),
        in_specs=[row(N_HEADS * HEAD_DIM), row(SSD_INNER), row(D_MODEL), row(D_MODEL), row(D_MODEL),
                  _resident((SSD_INNER, D_MODEL), const),
                  _resident((N_HEADS * HEAD_DIM, D_MODEL), const),
                  _resident((D_MODEL, D_MODEL), const),
                  _resident((D_MODEL, D_FF), const),
                  _resident((D_FF, D_MODEL), const),
                  _resident((4, D_MODEL), const)],
        out_specs=row(D_MODEL),
        out_shape=jax.ShapeDtypeStruct((m, D_MODEL), F32),
        compiler_params=_cparams(1),
        name="post",
    )(o, yn, gs, ga, x2d, w_ssd_o, w_attn_o, w_out, w_up, w_down, ln)


def _round_up(n, m):
    return (n + m - 1) // m * m


def _layer(x, cache_k, cache_v, cache_kidx, state_ssm, state_conv, bias_tiles, w_prep, conv_w, conv_b,
           dt_bias, a_log, d_skip, ssd_norm_w, w_ssd_o, w_attn_o, w_out, w_up, w_down, ln, *, tq, tk, lc, tm):
    b, t, _ = x.shape
    past = cache_k.shape[1]
    s_len = past + t
    m = b * t
    assert m % tm == 0 and t % tq == 0 and past % tk == 0 and tq <= tk and tk % LANES == 0
    x2d = x.reshape(m, D_MODEL)

    z, xbc, q, k, v, qi, gs, ga, sm = _inproj(x2d, w_prep, tm)

    t_pad = _round_up(t, lc)
    pad_t = lambda a: jnp.pad(a.reshape(b, t, a.shape[-1]), ((0, 0), (0, t_pad - t), (0, 0)))
    sm3 = pad_t(sm)
    dt_t = sm3[:, :, SM_DT:SM_DT + SSD_HEADS].transpose(0, 2, 1)
    yn, ssm_new = _ssd(pad_t(xbc), pad_t(z), sm3, dt_t, state_conv.astype(F32),
                       _state_to_kernel_layout(state_ssm.astype(F32)), conv_w, conv_b, dt_bias, a_log, d_skip,
                       ssd_norm_w, lc, min(t, lc))
    yn = yn[:, :t].reshape(m, SSD_INNER)
    new_ssm = _state_from_kernel_layout(ssm_new)
    xbc3 = xbc.reshape(b, t, CONV_DIM)
    new_conv = jnp.concatenate([state_conv.astype(F32), xbc3], axis=1)[:, -_HALO:]

    k3 = k.reshape(b, t, KV_DIM)
    v3 = v.reshape(b, t, KV_DIM)
    ki3 = sm.reshape(b, t, LANES)[:, :, SM_KI:SM_KI + IDX_DIM]
    s_pad = _round_up(s_len, tk)
    pad_s = lambda a: jnp.pad(a, ((0, 0), (0, s_pad - s_len), (0, 0)))
    k_all = pad_s(jnp.concatenate([cache_k.reshape(b, past, KV_DIM).astype(MXU_DTYPE), k3.astype(MXU_DTYPE)], axis=1))
    v_all = pad_s(jnp.concatenate([cache_v.reshape(b, past, KV_DIM).astype(MXU_DTYPE), v3.astype(MXU_DTYPE)], axis=1))
    ki_all = pad_s(jnp.concatenate([cache_kidx.astype(MXU_DTYPE), ki3.astype(MXU_DTYPE)], axis=1))
    n_sel = min(TOPK_MAX, s_len // 4)
    mask = _select(qi.reshape(b, t, -1), sm.reshape(b, t, LANES), ki_all.transpose(0, 2, 1),
                   tq, tk, s_len, past, n_sel)
    bias = bias_tiles[:, :, :tq, :].reshape(2, N_KV_HEADS, Q_PER_KV * tq, tk)
    o = _attention(q.reshape(b, t, -1), k_all.transpose(0, 2, 1), v_all, mask, bias, tq, tk, past)

    out = _post(o.reshape(m, -1), yn, gs, ga, x2d, w_ssd_o, w_attn_o, w_out, w_up, w_down, ln, tm)
    return out.reshape(b, t, D_MODEL), (k3.reshape(b, t, N_KV_HEADS, HEAD_DIM), v3.reshape(b, t, N_KV_HEADS, HEAD_DIM),
                                        ki3, new_ssm, new_conv)


ATTN_TILE = 128
SSD_CHUNK = 128
ROW_TILE = 256


def kernel(x_prompt, x_sample, cache_k, cache_v, cache_kidx, state_ssm, state_conv, rel_bias, w_in, conv_w, conv_b,
           dt_bias, a_log, d_skip, ssd_norm_w, w_ssd_o, w_attn_o, w_out, ln1_g, ln1_b, w_up, w_down, ln2_g, ln2_b):
    bp = x_prompt.shape[0]
    dtype = x_prompt.dtype
    bias_tiles = _bias_tiles(rel_bias, ATTN_TILE)
    yp, ys = x_prompt, x_sample
    st_p, st_s = [], []
    for l in range(DEPTH):
        w_prep = _prep_w_in(w_in[l])
        ln = jnp.stack([ln1_g[l], ln1_b[l], ln2_g[l], ln2_b[l]]).astype(F32)
        weights = (bias_tiles, w_prep, conv_w[l], conv_b[l], dt_bias[l], a_log[l], d_skip[l], ssd_norm_w[l],
                   w_ssd_o[l].astype(MXU_DTYPE), w_attn_o[l].astype(MXU_DTYPE), w_out[l].astype(MXU_DTYPE),
                   w_up[l].astype(MXU_DTYPE), w_down[l].astype(MXU_DTYPE), ln)
        yp, sp = _layer(yp, jnp.zeros((bp, 0, N_KV_HEADS, HEAD_DIM), dtype), jnp.zeros((bp, 0, N_KV_HEADS, HEAD_DIM), dtype),
                        jnp.zeros((bp, 0, IDX_DIM), dtype), jnp.zeros((bp, SSD_HEADS, SSD_HEAD_DIM, D_STATE), dtype),
                        jnp.zeros((bp, _HALO, CONV_DIM), dtype), *weights,
                        tq=min(ATTN_TILE, yp.shape[1]), tk=ATTN_TILE, lc=SSD_CHUNK, tm=ROW_TILE)
        ys, ss = _layer(ys, cache_k[l], cache_v[l], cache_kidx[l], state_ssm[l], state_conv[l], *weights,
                        tq=min(ATTN_TILE, ys.shape[1]), tk=ATTN_TILE, lc=SSD_CHUNK, tm=ROW_TILE)
        st_p.append(sp)
        st_s.append(ss)
    stack = lambda sts, i: jnp.stack([s[i] for s in sts])
    return (yp, ys) + tuple(stack(st_p, i) for i in range(5)) + tuple(stack(st_s, i) for i in range(5))
```

```python
import functools

import jax
import jax.numpy as jnp
from jax import lax
from jax.experimental import pallas as pl
from jax.experimental.pallas import tpu as pltpu

F32 = jnp.float32
I32 = jnp.int32
MXU_DTYPE = jnp.bfloat16

D_MODEL = 1024
CHUNK = 64
SSD_INNER = 2048
SSD_HEAD_DIM = 64
SSD_HEADS = 32
SSD_GROUPS = 4
HEADS_PER_GROUP = SSD_HEADS // SSD_GROUPS
D_STATE = 128
CONV_W = 4
CONV_DIM = SSD_INNER + 2 * SSD_GROUPS * D_STATE
N_HEADS = 16
N_KV_HEADS = 4
Q_PER_KV = N_HEADS // N_KV_HEADS
HEAD_DIM = 64
KV_DIM = N_KV_HEADS * HEAD_DIM
IDX_HEADS = 8
IDX_DIM = 64
TOPK_MAX = 256
REL_BUCKETS = 32
REL_MAX_DIST = 128
D_FF = 4 * D_MODEL
DEPTH = 1
ALPHA = (2.0 * DEPTH) ** 0.25
LN_EPS = 1e-5
RMS_EPS = 1e-5
IN_SIZES = (SSD_INNER, CONV_DIM, SSD_HEADS, N_HEADS * HEAD_DIM, KV_DIM, KV_DIM,
            IDX_HEADS * IDX_DIM, IDX_DIM, IDX_HEADS, D_MODEL, D_MODEL)

LANES = 128
VMEM_LIMIT_BYTES = 56 * 1024 * 1024

SM_KI = 0
SM_DT = IDX_DIM
SM_WI = IDX_DIM + SSD_HEADS
SM_PAD = LANES - (IDX_DIM + SSD_HEADS + IDX_HEADS)

NEG_BIG = -1e30
INT_MIN = -2 ** 31
INT_MAX = 2 ** 31 - 1
KEY_NEG_INF = -2139095041


def _cparams(n_axes):
    return pltpu.CompilerParams(dimension_semantics=("arbitrary",) * n_axes,
                                vmem_limit_bytes=VMEM_LIMIT_BYTES)


def _resident(shape, index_map):
    return pl.BlockSpec(shape, index_map, pipeline_mode=pl.Buffered(1))


def _dot(a, b):
    return jnp.dot(a.astype(MXU_DTYPE), b.astype(MXU_DTYPE), preferred_element_type=F32)


def _dot_exact(a, b):
    return jnp.dot(a, b, preferred_element_type=F32, precision=lax.Precision.HIGHEST)


def _sigmoid(x):
    return 1.0 / (1.0 + jnp.exp(-x))


def _softplus(x):
    return jnp.maximum(x, 0.0) + jnp.log1p(jnp.exp(-jnp.abs(x)))


_PROJ_WIDTHS = (SSD_INNER, CONV_DIM, N_HEADS * HEAD_DIM, KV_DIM, KV_DIM, IDX_HEADS * IDX_DIM,
                D_MODEL, D_MODEL, LANES)
_PROJ_WIDTH = sum(_PROJ_WIDTHS)
_PROJ_CHUNK = 512


def _prep_w_in(w_in):
    offs = [0]
    for s in IN_SIZES:
        offs.append(offs[-1] + s)
    z, xbc, dt, q, k, v, qi, ki, wi, gs, ga = [w_in[:, offs[i]:offs[i + 1]] for i in range(len(IN_SIZES))]
    small = jnp.concatenate([ki, dt, wi, jnp.zeros((D_MODEL, SM_PAD), w_in.dtype)], axis=1)
    w = jnp.concatenate([z, xbc, q * HEAD_DIM ** -0.5, k, v, qi * IDX_DIM ** -0.5, gs, ga, small], axis=1)
    return w.astype(MXU_DTYPE)


def _inproj_kernel(x_ref, w_ref, z_ref, xbc_ref, q_ref, k_ref, v_ref, qi_ref, gs_ref, ga_ref, sm_ref):
    xb = x_ref[...].astype(MXU_DTYPE)
    outs = (z_ref, xbc_ref, q_ref, k_ref, v_ref, qi_ref, gs_ref, ga_ref, sm_ref)
    c0 = 0
    for ref, width in zip(outs, _PROJ_WIDTHS):
        for o in range(0, width, _PROJ_CHUNK):
            n = min(_PROJ_CHUNK, width - o)
            r = jnp.dot(xb, w_ref[:, c0 + o:c0 + o + n], preferred_element_type=F32)
            ref[:, o:o + n] = r.astype(ref.dtype)
        c0 += width


def _inproj(x2d, w_prep, tm):
    m = x2d.shape[0]
    dtypes = (F32, F32, MXU_DTYPE, F32, F32, MXU_DTYPE, F32, F32, F32)
    out_shape = tuple(jax.ShapeDtypeStruct((m, w), d) for w, d in zip(_PROJ_WIDTHS, dtypes))
    out_specs = tuple(pl.BlockSpec((tm, w), lambda i: (i, 0)) for w in _PROJ_WIDTHS)
    return pl.pallas_call(
        _inproj_kernel,
        grid=(m // tm,),
        in_specs=[pl.BlockSpec((tm, D_MODEL), lambda i: (i, 0)),
                  _resident((D_MODEL, _PROJ_WIDTH), lambda i: (0, 0))],
        out_specs=out_specs,
        out_shape=out_shape,
        compiler_params=_cparams(1),
        name="inproj",
    )(x2d, w_prep)


_HALO = CONV_W - 1
_XP_BASE = 8


def _ssd_kernel(xbc_ref, z_ref, sm_ref, dtT_ref, conv0_ref, ssm0_ref, convw_ref, convb_ref,
                hp_ref, hpT_ref, dfull_ref, nw_ref, expand_ref,
                yn_ref, ssm_out_ref, xp_ref, st_ref, y_ref, *, lc, t_valid):
    c = pl.program_id(1)
    nc = pl.num_programs(1)

    @pl.when(c == 0)
    def _():
        xp_ref[_XP_BASE - _HALO:_XP_BASE, :] = conv0_ref[0]
        st_ref[...] = ssm0_ref[0]

    xp_ref[_XP_BASE:_XP_BASE + lc, :] = xbc_ref[0]
    conv = convb_ref[...]
    for i in range(CONV_W):
        conv = conv + xp_ref[_XP_BASE - _HALO + i:_XP_BASE - _HALO + i + lc, :] * convw_ref[i:i + 1, :]
    xp_ref[_XP_BASE - _HALO:_XP_BASE, :] = xp_ref[_XP_BASE + lc - _HALO:_XP_BASE + lc, :]
    act = conv * _sigmoid(conv)
    xs = act[:, :SSD_INNER]
    bm = act[:, SSD_INNER:SSD_INNER + SSD_GROUPS * D_STATE]
    cm = act[:, SSD_INNER + SSD_GROUPS * D_STATE:]

    dt = _softplus(sm_ref[0][:, SM_DT:SM_DT + SSD_HEADS] + hp_ref[0:1, :])
    dt_t = _softplus(dtT_ref[0] + hpT_ref[:, 0:1])
    if t_valid < lc:
        dt = jnp.where(lax.broadcasted_iota(I32, (lc, SSD_HEADS), 0) < t_valid, dt, 0.0)
        dt_t = jnp.where(lax.broadcasted_iota(I32, (SSD_HEADS, lc), 1) < t_valid, dt_t, 0.0)
    a = dt * (-jnp.exp(hp_ref[1:2, :]))
    a_t = dt_t * (-jnp.exp(hpT_ref[:, 1:2]))
    ri = lax.broadcasted_iota(I32, (lc, lc), 0)
    ci = lax.broadcasted_iota(I32, (lc, lc), 1)
    causal = ci <= ri
    acum = _dot_exact(causal.astype(F32), a)
    acum_t = _dot_exact(a_t, (ri <= ci).astype(F32))
    e_a = jnp.exp(acum)
    w_end = jnp.exp(acum[lc - 1:lc, :] - acum) * dt
    e_a_full = _dot_exact(e_a, expand_ref[...])
    w_end_full = _dot_exact(w_end, expand_ref[...])
    xw = xs * w_end_full
    lane = lax.broadcasted_iota(I32, (lc, 2 * SSD_HEAD_DIM), 1)
    gw = HEADS_PER_GROUP * SSD_HEAD_DIM

    for g in range(SSD_GROUPS):
        b_g = bm[:, g * D_STATE:(g + 1) * D_STATE]
        c_g = cm[:, g * D_STATE:(g + 1) * D_STATE].astype(MXU_DTYPE)
        b_gt = b_g.T.astype(MXU_DTYPE)
        cb = jnp.dot(c_g, b_gt, preferred_element_type=F32)
        state = st_ref[g]
        y_off = _dot(c_g, state) * e_a_full[:, g * gw:(g + 1) * gw]
        st_ref[g] = state * e_a_full[lc - 1:lc, g * gw:(g + 1) * gw] + _dot(b_gt, xw[:, g * gw:(g + 1) * gw])
        for jp in range(HEADS_PER_GROUP // 2):
            h0 = g * HEADS_PER_GROUP + 2 * jp
            ms = []
            for h in (h0, h0 + 1):
                seg = acum[:, h:h + 1] - acum_t[h:h + 1, :]
                decay = jnp.exp(jnp.where(causal, seg, -jnp.inf))
                ms.append((cb * decay * dt_t[h:h + 1, :]).astype(MXU_DTYPE))
            c0 = h0 * SSD_HEAD_DIM
            x_pair = xs[:, c0:c0 + 2 * SSD_HEAD_DIM]
            r = jnp.dot(jnp.concatenate(ms, axis=0), x_pair.astype(MXU_DTYPE), preferred_element_type=F32)
            y_diag = jnp.where(lane < SSD_HEAD_DIM, r[:lc], r[lc:])
            o0 = 2 * jp * SSD_HEAD_DIM
            y_ref[:, c0:c0 + 2 * SSD_HEAD_DIM] = (y_diag + y_off[:, o0:o0 + 2 * SSD_HEAD_DIM]
                                                  + dfull_ref[:, c0:c0 + 2 * SSD_HEAD_DIM] * x_pair)

    zv = z_ref[0]
    hg = y_ref[...] * (zv * _sigmoid(zv))
    for g in range(SSD_GROUPS):
        sl = hg[:, g * gw:(g + 1) * gw]
        ms = jnp.mean(sl * sl, axis=-1, keepdims=True)
        yn_ref[0, :, g * gw:(g + 1) * gw] = (sl * lax.rsqrt(ms + RMS_EPS) * nw_ref[:, g * gw:(g + 1) * gw]).astype(yn_ref.dtype)

    @pl.when(c == nc - 1)
    def _():
        ssm_out_ref[0] = st_ref[...]


def _ssd(xbc, z, sm, dt_t, conv0, ssm0_t, conv_w, conv_b, dt_bias, a_log, d_skip, norm_w, lc, t_valid):
    b, t, _ = xbc.shape
    nc = t // lc
    hp = jnp.stack([dt_bias, a_log]).astype(F32)
    d_full = jnp.repeat(d_skip.astype(F32), SSD_HEAD_DIM)[None, :]
    expand = jnp.repeat(jnp.eye(SSD_HEADS, dtype=F32), SSD_HEAD_DIM, axis=1)
    gw = HEADS_PER_GROUP * SSD_HEAD_DIM
    const2 = lambda bi, ci: (0, 0)
    kern = functools.partial(_ssd_kernel, lc=lc, t_valid=t_valid)
    return pl.pallas_call(
        kern,
        grid=(b, nc),
        in_specs=[pl.BlockSpec((1, lc, CONV_DIM), lambda bi, ci: (bi, ci, 0)),
                  pl.BlockSpec((1, lc, SSD_INNER), lambda bi, ci: (bi, ci, 0)),
                  pl.BlockSpec((1, lc, LANES), lambda bi, ci: (bi, ci, 0)),
                  pl.BlockSpec((1, SSD_HEADS, lc), lambda bi, ci: (bi, 0, ci)),
                  pl.BlockSpec((1, _HALO, CONV_DIM), lambda bi, ci: (bi, 0, 0)),
                  pl.BlockSpec((1, SSD_GROUPS, D_STATE, gw), lambda bi, ci: (bi, 0, 0, 0)),
                  pl.BlockSpec((CONV_W, CONV_DIM), const2),
                  pl.BlockSpec((1, CONV_DIM), const2),
                  pl.BlockSpec((2, SSD_HEADS), const2),
                  pl.BlockSpec((SSD_HEADS, 2), const2),
                  pl.BlockSpec((1, SSD_INNER), const2),
                  pl.BlockSpec((1, SSD_INNER), const2),
                  pl.BlockSpec((SSD_HEADS, SSD_INNER), const2)],
        out_specs=(pl.BlockSpec((1, lc, SSD_INNER), lambda bi, ci: (bi, ci, 0)),
                   pl.BlockSpec((1, SSD_GROUPS, D_STATE, gw), lambda bi, ci: (bi, 0, 0, 0))),
        out_shape=(jax.ShapeDtypeStruct((b, t, SSD_INNER), MXU_DTYPE),
                   jax.ShapeDtypeStruct((b, SSD_GROUPS, D_STATE, gw), F32)),
        scratch_shapes=[pltpu.VMEM((_XP_BASE + lc, CONV_DIM), F32),
                        pltpu.VMEM((SSD_GROUPS, D_STATE, gw), F32),
                        pltpu.VMEM((lc, SSD_INNER), F32)],
        compiler_params=_cparams(2),
        name="ssd",
    )(xbc, z, sm, dt_t, conv0, ssm0_t, conv_w.astype(F32), conv_b.astype(F32)[None, :], hp, hp.T,
      d_full, norm_w.astype(F32)[None, :], expand)


def _state_to_kernel_layout(s):
    b = s.shape[0]
    s = s.reshape(b, SSD_GROUPS, HEADS_PER_GROUP, SSD_HEAD_DIM, D_STATE)
    return s.transpose(0, 1, 4, 2, 3).reshape(b, SSD_GROUPS, D_STATE, HEADS_PER_GROUP * SSD_HEAD_DIM)


def _state_from_kernel_layout(s):
    b = s.shape[0]
    s = s.reshape(b, SSD_GROUPS, D_STATE, HEADS_PER_GROUP, SSD_HEAD_DIM)
    return s.transpose(0, 1, 3, 4, 2).reshape(b, SSD_HEADS, SSD_HEAD_DIM, D_STATE)


_T5_LOG_STARTS = (12, 16, 23, 32, 46, 64, 91)
_T5_FAR_BUCKET = REL_BUCKETS // 2 - 1


def _bias_kernel(rb_ref, out_ref, *, tk):
    half = REL_BUCKETS // 2
    max_exact = half // 2
    ri = lax.broadcasted_iota(I32, (tk, tk), 0)
    ci = lax.broadcasted_iota(I32, (tk, tk), 1)
    for t in range(2):
        rel = ci - ri - (1 - t) * tk
        n = jnp.abs(rel)
        large = jnp.full((tk, tk), max_exact, I32)
        for start in _T5_LOG_STARTS:
            large = large + (n >= start).astype(I32)
        bucket = jnp.where(rel > 0, half, 0) + jnp.where(n < max_exact, n, large)
        for h in range(N_HEADS):
            acc = jnp.zeros((tk, tk), F32)
            for bkt in range(REL_BUCKETS):
                acc = jnp.where(bucket == bkt, rb_ref[bkt, h], acc)
            out_ref[t, h] = acc - rb_ref[_T5_FAR_BUCKET, h]


def _bias_tiles(rel_bias, tk):
    return pl.pallas_call(
        functools.partial(_bias_kernel, tk=tk),
        in_specs=[pl.BlockSpec(memory_space=pltpu.SMEM)],
        out_specs=pl.BlockSpec(memory_space=pltpu.VMEM),
        out_shape=jax.ShapeDtypeStruct((2, N_HEADS, tk, tk), F32),
        name="bias_tiles",
    )(rel_bias.astype(F32))


_MAX_BISECTIONS = 40


def _select_kernel(qi_ref, sm_ref, kit_ref, mask_ref, key_ref, w_ref, *, tq, wk, s_len, q_start, n_sel):
    qb = pl.program_id(1)
    n_blocks = mask_ref.shape[2] // wk
    nt = wk // LANES
    p0 = q_start + qb * tq
    pos = p0 + lax.broadcasted_iota(I32, (tq, 1), 0)
    vis_end = (pos // CHUNK + 1) * CHUNK
    vis_max = jnp.minimum(((p0 + tq - 1) // CHUNK + 1) * CHUNK, s_len)
    nkb = (vis_max + wk - 1) // wk

    wi = sm_ref[0][:, SM_WI:SM_WI + IDX_HEADS] * IDX_HEADS ** -0.5
    for hh in range(IDX_HEADS):
        w_ref[hh] = jnp.broadcast_to(wi[:, hh:hh + 1], (tq, LANES))
    qi = qi_ref[0]
    lane_k = lax.broadcasted_iota(I32, (1, LANES), 1)

    def score_body(j, carry):
        k0 = pl.multiple_of(j * wk, wk)
        kt = kit_ref[0, :, pl.ds(k0, wk)]
        ds = [jnp.dot(qi[:, hh * IDX_DIM:(hh + 1) * IDX_DIM], kt, preferred_element_type=F32)
              for hh in range(IDX_HEADS)]
        for u in range(nt):
            sc = jnp.zeros((tq, LANES), F32)
            for hh in range(IDX_HEADS):
                sc = sc + w_ref[hh] * jnp.maximum(ds[hh][:, u * LANES:(u + 1) * LANES], 0.0)
            sc = jnp.where(sc == 0.0, 0.0, sc)
            key_pos = k0 + u * LANES + lane_k
            sc = jnp.where((key_pos < vis_end) & (key_pos < s_len), sc, -jnp.inf)
            bits = lax.bitcast_convert_type(sc, I32)
            key_ref[:, pl.ds(k0 + u * LANES, LANES)] = jnp.where(bits < 0, bits ^ 0x7FFFFFFF, bits)
        return carry

    lax.fori_loop(0, nkb, score_body, 0)

    def count_ge(thr):
        thr_b = jnp.broadcast_to(thr, (tq, LANES))

        def body(j, acc):
            k0 = pl.multiple_of(j * wk, wk)
            for u in range(nt):
                blk = key_ref[:, pl.ds(k0 + u * LANES, LANES)]
                acc = acc + (blk >= thr_b).astype(I32)
            return acc

        acc = lax.fori_loop(0, nkb, body, jnp.zeros((tq, LANES), I32))
        return jnp.sum(acc, axis=1, keepdims=True)

    cnt_fin = count_ge(jnp.full((tq, 1), KEY_NEG_INF + 1, I32))
    cnt_pos = count_ge(jnp.zeros((tq, 1), I32))
    pos_enough = cnt_pos >= n_sel
    lo0 = jnp.where(pos_enough, 0, KEY_NEG_INF + 1).astype(I32)
    cnt_lo0 = jnp.where(pos_enough, cnt_pos, cnt_fin)
    hi0 = jnp.where(pos_enough, INT_MAX, 0).astype(I32)
    cnt_hi0 = jnp.where(pos_enough, 0, cnt_pos)

    def unfinished(lo, hi, cnt_lo):
        return (cnt_lo > n_sel) & (hi - 1 > lo)

    def search_cond(st):
        return (st[0] < _MAX_BISECTIONS) & (st[1] > 0)

    def search_body(st):
        it, _, lo, hi, cnt_lo, cnt_hi = st
        live = unfinished(lo, hi, cnt_lo)
        mid = (lo >> 1) + (hi >> 1) + (lo & hi & 1)
        cnt = count_ge(jnp.where(live, mid, lo))
        up = live & (cnt >= n_sel)
        dn = live & (cnt < n_sel)
        lo = jnp.where(up, mid, lo)
        cnt_lo = jnp.where(up, cnt, cnt_lo)
        hi = jnp.where(dn, mid, hi)
        cnt_hi = jnp.where(dn, cnt, cnt_hi)
        return (it + 1, jnp.max(unfinished(lo, hi, cnt_lo).astype(I32)), lo, hi, cnt_lo, cnt_hi)

    st0 = (jnp.int32(0), jnp.max(unfinished(lo0, hi0, cnt_lo0).astype(I32)), lo0, hi0, cnt_lo0, cnt_hi0)
    _, _, tau, _, cnt_tau, cnt_above = lax.while_loop(search_cond, search_body, st0)

    tie_rows = cnt_tau > n_sel
    has_ties = jnp.max(tie_rows.astype(I32)) > 0

    @pl.when(jnp.logical_not(has_ties))
    def _():
        tau_b = jnp.broadcast_to(tau, (tq, LANES))

        def body(j, carry):
            k0 = pl.multiple_of(j * wk, wk)
            for u in range(nt):
                sl = pl.ds(k0 + u * LANES, LANES)
                mask_ref[0, :, sl] = (key_ref[:, sl] >= tau_b).astype(I32).astype(mask_ref.dtype)
            return carry
        lax.fori_loop(0, nkb, body, 0)

    @pl.when(has_ties)
    def _():
        upper = (lax.broadcasted_iota(I32, (wk, wk), 0) <= lax.broadcasted_iota(I32, (wk, wk), 1)).astype(MXU_DTYPE)
        need = jnp.where(tie_rows, n_sel - cnt_above, wk * n_blocks).astype(F32)

        def body(j, seen):
            k0 = pl.multiple_of(j * wk, wk)
            blk = key_ref[:, pl.ds(k0, wk)]
            eq = blk == tau
            rank = seen + jnp.dot(eq.astype(F32).astype(MXU_DTYPE), upper, preferred_element_type=F32)
            sel = (blk > tau) | (eq & (rank <= need))
            mask_ref[0, :, pl.ds(k0, wk)] = sel.astype(I32).astype(mask_ref.dtype)
            return seen + jnp.sum(eq.astype(F32), axis=1, keepdims=True)
        lax.fori_loop(0, nkb, body, jnp.zeros((tq, 1), F32))

    def zero_body(j, carry):
        k0 = pl.multiple_of(j * wk, wk)
        mask_ref[0, :, pl.ds(k0, wk)] = jnp.zeros((tq, wk), mask_ref.dtype)
        return carry

    lax.fori_loop(nkb, n_blocks, zero_body, 0)


def _select(qi, sm, kit, tq, wk, s_len, q_start, n_sel):
    b, t, _ = qi.shape
    s_pad = kit.shape[2]
    kern = functools.partial(_select_kernel, tq=tq, wk=wk, s_len=s_len, q_start=q_start, n_sel=n_sel)
    return pl.pallas_call(
        kern,
        grid=(b, t // tq),
        in_specs=[pl.BlockSpec((1, tq, IDX_HEADS * IDX_DIM), lambda bi, qi_: (bi, qi_, 0)),
                  pl.BlockSpec((1, tq, LANES), lambda bi, qi_: (bi, qi_, 0)),
                  _resident((1, IDX_DIM, s_pad), lambda bi, qi_: (bi, 0, 0))],
        out_specs=pl.BlockSpec((1, tq, s_pad), lambda bi, qi_: (bi, qi_, 0)),
        out_shape=jax.ShapeDtypeStruct((b, t, s_pad), jnp.int8),
        scratch_shapes=[pltpu.VMEM((tq, s_pad), I32),
                        pltpu.VMEM((IDX_HEADS, tq, LANES), F32)],
        compiler_params=_cparams(2),
        name="select",
    )(qi, sm, kit)


_V_AUG = 2 * HEAD_DIM


def _attn_kernel(q_ref, kt_ref, v_ref, mask_ref, bias_ref, o_ref, m_ref, acc_ref, qg_ref,
                 *, tq, tk, wide, q_start):
    qb = pl.program_id(1)
    jd = (q_start + qb * tq) // tk

    for g in range(N_KV_HEADS):
        for r in range(Q_PER_KV):
            c0 = (g * Q_PER_KV + r) * HEAD_DIM
            qg_ref[g, r * tq:(r + 1) * tq, :] = q_ref[0, :, c0:c0 + HEAD_DIM]
    m_ref[...] = jnp.full(m_ref.shape, NEG_BIG, F32)
    acc_ref[...] = jnp.zeros(acc_ref.shape, F32)

    def process(k0, width, near):
        nt = width // LANES
        mb = jnp.where(mask_ref[0, :, pl.ds(k0, width)].astype(I32) != 0, 0.0, NEG_BIG)
        mb = jnp.concatenate([mb] * Q_PER_KV, axis=0)
        for g in range(N_KV_HEADS):
            kt = kt_ref[0, g * HEAD_DIM:(g + 1) * HEAD_DIM, pl.ds(k0, width)]
            s = jnp.dot(qg_ref[g], kt, preferred_element_type=F32) + mb
            if near is not None:
                s = s + bias_ref[near, g]
            tiles = [s[:, u * LANES:(u + 1) * LANES] for u in range(nt)]
            mx = tiles[0]
            for u in range(1, nt):
                mx = jnp.maximum(mx, tiles[u])
            m_old = m_ref[g]
            m_new = jnp.maximum(m_old, jnp.max(mx, axis=1, keepdims=True))
            alpha = jnp.exp(m_old - m_new)
            p = jnp.concatenate([jnp.exp(tl - m_new).astype(MXU_DTYPE) for tl in tiles], axis=1)
            vt = v_ref[0, pl.ds(k0, width), g * _V_AUG:(g + 1) * _V_AUG]
            acc_ref[g] = alpha * acc_ref[g] + jnp.dot(p, vt, preferred_element_type=F32)
            m_ref[g] = m_new

    n_far = jnp.maximum(jd - 1, 0)
    per_wide = wide // tk
    n_wide = n_far // per_wide

    def wide_body(j, carry):
        process(pl.multiple_of(j * wide, wide), wide, None)
        return carry

    def far_body(j, carry):
        process(pl.multiple_of(j * tk, tk), tk, None)
        return carry

    lax.fori_loop(0, n_wide, wide_body, 0)
    lax.fori_loop(n_wide * per_wide, n_far, far_body, 0)

    @pl.when(jd >= 1)
    def _():
        process(pl.multiple_of((jd - 1) * tk, tk), tk, 0)

    process(pl.multiple_of(jd * tk, tk), tk, 1)

    for g in range(N_KV_HEADS):
        acc = acc_ref[g]
        o = acc[:, :HEAD_DIM] / acc[:, HEAD_DIM:HEAD_DIM + 1]
        for r in range(Q_PER_KV):
            c0 = (g * Q_PER_KV + r) * HEAD_DIM
            o_ref[0, :, c0:c0 + HEAD_DIM] = o[r * tq:(r + 1) * tq].astype(o_ref.dtype)


def _attention(q, kt, v_aug, mask, bias, tq, tk, wide, q_start):
    b, t, _ = q.shape
    s_pad = kt.shape[2]
    rows = Q_PER_KV * tq
    kern = functools.partial(_attn_kernel, tq=tq, tk=tk, wide=wide, q_start=q_start)
    return pl.pallas_call(
        kern,
        grid=(b, t // tq),
        in_specs=[pl.BlockSpec((1, tq, N_HEADS * HEAD_DIM), lambda bi, qi_: (bi, qi_, 0)),
                  _resident((1, KV_DIM, s_pad), lambda bi, qi_: (bi, 0, 0)),
                  _resident((1, s_pad, N_KV_HEADS * _V_AUG), lambda bi, qi_: (bi, 0, 0)),
                  pl.BlockSpec((1, tq, s_pad), lambda bi, qi_: (bi, qi_, 0)),
                  _resident((2, N_KV_HEADS, rows, tk), lambda bi, qi_: (0, 0, 0, 0))],
        out_specs=pl.BlockSpec((1, tq, N_HEADS * HEAD_DIM), lambda bi, qi_: (bi, qi_, 0)),
        out_shape=jax.ShapeDtypeStruct((b, t, N_HEADS * HEAD_DIM), MXU_DTYPE),
        scratch_shapes=[pltpu.VMEM((N_KV_HEADS, rows, LANES), F32),
                        pltpu.VMEM((N_KV_HEADS, rows, _V_AUG), F32),
                        pltpu.VMEM((N_KV_HEADS, rows, HEAD_DIM), MXU_DTYPE)],
        compiler_params=_cparams(2),
        name="attention",
    )(q, kt, v_aug, mask, bias)


def _layer_norm(x, g, b):
    mu = jnp.mean(x, axis=-1, keepdims=True)
    xc = x - mu
    var = jnp.mean(xc * xc, axis=-1, keepdims=True)
    return xc * lax.rsqrt(var + LN_EPS) * g + b


def _post_kernel(o_ref, yn_ref, gs_ref, ga_ref, x_ref, wso_ref, wao_ref, wout_ref, wup_ref, wdown_ref,
                 ln_ref, out_ref):
    y1 = jnp.dot(yn_ref[...], wso_ref[...], preferred_element_type=F32)
    y2 = jnp.dot(o_ref[...], wao_ref[...], preferred_element_type=F32)
    mix = _sigmoid(gs_ref[...]) * y1 + _sigmoid(ga_ref[...]) * y2
    mixed = _dot(mix, wout_ref[...])
    h = _layer_norm(ALPHA * x_ref[...] + mixed, ln_ref[0:1, :], ln_ref[1:2, :])
    u = jnp.maximum(_dot(h, wup_ref[...]), 0.0)
    f = _dot(u * u, wdown_ref[...])
    out_ref[...] = _layer_norm(ALPHA * h + f, ln_ref[2:3, :], ln_ref[3:4, :])


def _post(o, yn, gs, ga, x2d, w_ssd_o, w_attn_o, w_out, w_up, w_down, ln, tm):
    m = x2d.shape[0]
    row = lambda w: pl.BlockSpec((tm, w), lambda i: (i, 0))
    const = lambda i: (0, 0)
    return pl.pallas_call(
        _post_kernel,
        grid=(m // tm,),
        in_specs=[row(N_HEADS * HEAD_DIM), row(SSD_INNER), row(D_MODEL), row(D_MODEL), row(D_MODEL),
                  _resident((SSD_INNER, D_MODEL), const),
                  _resident((N_HEADS * HEAD_DIM, D_MODEL), const),
                  _resident((D_MODEL, D_MODEL), const),
                  _resident((D_MODEL, D_FF), const),
                  _resident((D_FF, D_MODEL), const),
                  _resident((4, D_MODEL), const)],
        out_specs=row(D_MODEL),
        out_shape=jax.ShapeDtypeStruct((m, D_MODEL), F32),
        compiler_params=_cparams(1),
        name="post",
    )(o, yn, gs, ga, x2d, w_ssd_o, w_attn_o, w_out, w_up, w_down, ln)


def _round_up(n, m):
    return (n + m - 1) // m * m


def _layer(x, cache_k, cache_v, cache_kidx, state_ssm, state_conv, bias_tiles, w_prep, conv_w, conv_b,
           dt_bias, a_log, d_skip, ssd_norm_w, w_ssd_o, w_attn_o, w_out, w_up, w_down, ln, *, tq, tk, wide, lc, tm):
    b, t, _ = x.shape
    past = cache_k.shape[1]
    s_len = past + t
    m = b * t
    assert m % tm == 0 and t % tq == 0 and past % tk == 0 and tk % LANES == 0 and wide % tk == 0
    assert tq == tk or t == tq, "every query tile must start on a key-tile boundary"
    assert t % lc == 0 or t < lc
    x2d = x.reshape(m, D_MODEL)

    z, xbc, q, k, v, qi, gs, ga, sm = _inproj(x2d, w_prep, tm)

    t_pad = _round_up(t, lc)
    pad_t = lambda a: jnp.pad(a.reshape(b, t, a.shape[-1]), ((0, 0), (0, t_pad - t), (0, 0)))
    sm3 = pad_t(sm)
    dt_t = sm3[:, :, SM_DT:SM_DT + SSD_HEADS].transpose(0, 2, 1)
    yn, ssm_new = _ssd(pad_t(xbc), pad_t(z), sm3, dt_t, state_conv.astype(F32),
                       _state_to_kernel_layout(state_ssm.astype(F32)), conv_w, conv_b, dt_bias, a_log, d_skip,
                       ssd_norm_w, lc, min(t, lc))
    yn = yn[:, :t].reshape(m, SSD_INNER)
    new_ssm = _state_from_kernel_layout(ssm_new)
    xbc3 = xbc.reshape(b, t, CONV_DIM)
    new_conv = jnp.concatenate([state_conv.astype(F32), xbc3], axis=1)[:, -_HALO:]

    k3 = k.reshape(b, t, KV_DIM)
    v3 = v.reshape(b, t, KV_DIM)
    ki3 = sm.reshape(b, t, LANES)[:, :, SM_KI:SM_KI + IDX_DIM]
    s_pad = _round_up(s_len, wide)
    pad_s = lambda a: jnp.pad(a, ((0, 0), (0, s_pad - s_len)) + ((0, 0),) * (a.ndim - 2))
    k_all = pad_s(jnp.concatenate([cache_k.reshape(b, past, KV_DIM).astype(MXU_DTYPE), k3.astype(MXU_DTYPE)], axis=1))
    v_all = jnp.concatenate([cache_v.astype(MXU_DTYPE), v3.reshape(b, t, N_KV_HEADS, HEAD_DIM).astype(MXU_DTYPE)], axis=1)
    v_aug = pad_s(jnp.concatenate([v_all, jnp.ones((b, s_len, N_KV_HEADS, 1), MXU_DTYPE),
                                   jnp.zeros((b, s_len, N_KV_HEADS, _V_AUG - HEAD_DIM - 1), MXU_DTYPE)], axis=3))
    ki_all = pad_s(jnp.concatenate([cache_kidx.astype(MXU_DTYPE), ki3.astype(MXU_DTYPE)], axis=1))
    n_sel = min(TOPK_MAX, s_len // 4)
    mask = _select(qi.reshape(b, t, -1), sm.reshape(b, t, LANES), ki_all.transpose(0, 2, 1),
                   tq, wide, s_len, past, n_sel)
    bias = bias_tiles[:, :, :tq, :].reshape(2, N_KV_HEADS, Q_PER_KV * tq, tk)
    o = _attention(q.reshape(b, t, -1), k_all.transpose(0, 2, 1), v_aug.reshape(b, s_pad, N_KV_HEADS * _V_AUG),
                   mask, bias, tq, tk, wide, past)

    out = _post(o.reshape(m, -1), yn, gs, ga, x2d, w_ssd_o, w_attn_o, w_out, w_up, w_down, ln, tm)
    return out.reshape(b, t, D_MODEL), (k3.reshape(b, t, N_KV_HEADS, HEAD_DIM), v3.reshape(b, t, N_KV_HEADS, HEAD_DIM),
                                        ki3, new_ssm, new_conv)


ATTN_TILE = 128
WIDE_TILE = 512
SSD_CHUNK = 128
ROW_TILE = 256


def kernel(x_prompt, x_sample, cache_k, cache_v, cache_kidx, state_ssm, state_conv, rel_bias, w_in, conv_w, conv_b,
           dt_bias, a_log, d_skip, ssd_norm_w, w_ssd_o, w_attn_o, w_out, ln1_g, ln1_b, w_up, w_down, ln2_g, ln2_b):
    bp = x_prompt.shape[0]
    dtype = x_prompt.dtype
    bias_tiles = _bias_tiles(rel_bias, ATTN_TILE)
    yp, ys = x_prompt, x_sample
    st_p, st_s = [], []
    tiles = dict(tk=ATTN_TILE, wide=WIDE_TILE, lc=SSD_CHUNK, tm=ROW_TILE)
    for l in range(DEPTH):
        w_prep = _prep_w_in(w_in[l])
        ln = jnp.stack([ln1_g[l], ln1_b[l], ln2_g[l], ln2_b[l]]).astype(F32)
        weights = (bias_tiles, w_prep, conv_w[l], conv_b[l], dt_bias[l], a_log[l], d_skip[l], ssd_norm_w[l],
                   w_ssd_o[l].astype(MXU_DTYPE), w_attn_o[l].astype(MXU_DTYPE), w_out[l].astype(MXU_DTYPE),
                   w_up[l].astype(MXU_DTYPE), w_down[l].astype(MXU_DTYPE), ln)
        yp, sp = _layer(yp, jnp.zeros((bp, 0, N_KV_HEADS, HEAD_DIM), dtype), jnp.zeros((bp, 0, N_KV_HEADS, HEAD_DIM), dtype),
                        jnp.zeros((bp, 0, IDX_DIM), dtype), jnp.zeros((bp, SSD_HEADS, SSD_HEAD_DIM, D_STATE), dtype),
                        jnp.zeros((bp, _HALO, CONV_DIM), dtype), *weights,
                        tq=min(ATTN_TILE, yp.shape[1]), **tiles)
        ys, ss = _layer(ys, cache_k[l], cache_v[l], cache_kidx[l], state_ssm[l], state_conv[l], *weights,
                        tq=min(ATTN_TILE, ys.shape[1]), **tiles)
        st_p.append(sp)
        st_s.append(ss)
    stack = lambda sts, i: jnp.stack([s[i] for s in sts])
    return (yp, ys) + tuple(stack(st_p, i) for i in range(5)) + tuple(stack(st_s, i) for i in range(5))
```

```python
import functools

import jax
import jax.numpy as jnp
from jax import lax
from jax.experimental import pallas as pl
from jax.experimental.pallas import tpu as pltpu

F32 = jnp.float32
I32 = jnp.int32
MXU_DTYPE = jnp.bfloat16

D_MODEL = 1024
CHUNK = 64
SSD_INNER = 2048
SSD_HEAD_DIM = 64
SSD_HEADS = 32
SSD_GROUPS = 4
HEADS_PER_GROUP = SSD_HEADS // SSD_GROUPS
D_STATE = 128
CONV_W = 4
CONV_DIM = SSD_INNER + 2 * SSD_GROUPS * D_STATE
N_HEADS = 16
N_KV_HEADS = 4
Q_PER_KV = N_HEADS // N_KV_HEADS
HEAD_DIM = 64
KV_DIM = N_KV_HEADS * HEAD_DIM
IDX_HEADS = 8
IDX_DIM = 64
TOPK_MAX = 256
REL_BUCKETS = 32
REL_MAX_DIST = 128
D_FF = 4 * D_MODEL
DEPTH = 1
ALPHA = (2.0 * DEPTH) ** 0.25
LN_EPS = 1e-5
RMS_EPS = 1e-5
IN_SIZES = (SSD_INNER, CONV_DIM, SSD_HEADS, N_HEADS * HEAD_DIM, KV_DIM, KV_DIM,
            IDX_HEADS * IDX_DIM, IDX_DIM, IDX_HEADS, D_MODEL, D_MODEL)

LANES = 128
VMEM_LIMIT_BYTES = 56 * 1024 * 1024

SM_KI = 0
SM_DT = IDX_DIM
SM_WI = IDX_DIM + SSD_HEADS
SM_PAD = LANES - (IDX_DIM + SSD_HEADS + IDX_HEADS)

NEG_BIG = -1e30
INT_MIN = -2 ** 31
INT_MAX = 2 ** 31 - 1
KEY_NEG_INF = -0x7F800000


def _cparams(n_axes):
    return pltpu.CompilerParams(dimension_semantics=("arbitrary",) * n_axes,
                                vmem_limit_bytes=VMEM_LIMIT_BYTES)


def _resident(shape, index_map):
    return pl.BlockSpec(shape, index_map, pipeline_mode=pl.Buffered(1))


def _per_batch(shape, index_map, steps_per_batch):
    if steps_per_batch > 1:
        return _resident(shape, index_map)
    return pl.BlockSpec(shape, index_map)


def _dot(a, b):
    return jnp.dot(a.astype(MXU_DTYPE), b.astype(MXU_DTYPE), preferred_element_type=F32)


def _dot_exact(a, b):
    return jnp.dot(a, b, preferred_element_type=F32, precision=lax.Precision.HIGHEST)


def _sigmoid(x):
    return 1.0 / (1.0 + jnp.exp(-x))


def _softplus(x):
    return jnp.maximum(x, 0.0) + jnp.log1p(jnp.exp(-jnp.abs(x)))


_PROJ_WIDTHS = (SSD_INNER, CONV_DIM, N_HEADS * HEAD_DIM, KV_DIM, KV_DIM, IDX_HEADS * IDX_DIM,
                D_MODEL, D_MODEL, LANES)
_PROJ_WIDTH = sum(_PROJ_WIDTHS)
_PROJ_CHUNK = 512


def _prep_w_in(w_in):
    offs = [0]
    for s in IN_SIZES:
        offs.append(offs[-1] + s)
    z, xbc, dt, q, k, v, qi, ki, wi, gs, ga = [w_in[:, offs[i]:offs[i + 1]] for i in range(len(IN_SIZES))]
    small = jnp.concatenate([ki, dt, wi, jnp.zeros((D_MODEL, SM_PAD), w_in.dtype)], axis=1)
    w = jnp.concatenate([z, xbc, q * HEAD_DIM ** -0.5, k, v, qi * IDX_DIM ** -0.5, gs, ga, small], axis=1)
    return w.astype(MXU_DTYPE)


def _inproj_kernel(x_ref, w_ref, z_ref, xbc_ref, q_ref, k_ref, v_ref, qi_ref, gs_ref, ga_ref, sm_ref):
    xb = x_ref[...].astype(MXU_DTYPE)
    outs = (z_ref, xbc_ref, q_ref, k_ref, v_ref, qi_ref, gs_ref, ga_ref, sm_ref)
    c0 = 0
    for ref, width in zip(outs, _PROJ_WIDTHS):
        for o in range(0, width, _PROJ_CHUNK):
            n = min(_PROJ_CHUNK, width - o)
            r = jnp.dot(xb, w_ref[:, c0 + o:c0 + o + n], preferred_element_type=F32)
            ref[:, o:o + n] = r.astype(ref.dtype)
        c0 += width


def _inproj(x2d, w_prep, tm):
    m = x2d.shape[0]
    dtypes = (F32, F32, MXU_DTYPE, F32, F32, MXU_DTYPE, F32, F32, F32)
    out_shape = tuple(jax.ShapeDtypeStruct((m, w), d) for w, d in zip(_PROJ_WIDTHS, dtypes))
    out_specs = tuple(pl.BlockSpec((tm, w), lambda i: (i, 0)) for w in _PROJ_WIDTHS)
    return pl.pallas_call(
        _inproj_kernel,
        grid=(m // tm,),
        in_specs=[pl.BlockSpec((tm, D_MODEL), lambda i: (i, 0)),
                  _resident((D_MODEL, _PROJ_WIDTH), lambda i: (0, 0))],
        out_specs=out_specs,
        out_shape=out_shape,
        compiler_params=_cparams(1),
        name="inproj",
    )(x2d, w_prep)


_HALO = CONV_W - 1
_XP_BASE = 8


def _ssd_kernel(xbc_ref, z_ref, sm_ref, dtT_ref, conv0_ref, ssm0_ref, convw_ref, convb_ref,
                hp_ref, hpT_ref, dfull_ref, nw_ref, expand_ref,
                yn_ref, ssm_out_ref, xp_ref, st_ref, y_ref, *, lc, t_valid):
    c = pl.program_id(1)
    nc = pl.num_programs(1)

    @pl.when(c == 0)
    def _():
        xp_ref[_XP_BASE - _HALO:_XP_BASE, :] = conv0_ref[0]
        st_ref[...] = ssm0_ref[0]

    xp_ref[_XP_BASE:_XP_BASE + lc, :] = xbc_ref[0]
    conv = convb_ref[...]
    for i in range(CONV_W):
        conv = conv + xp_ref[_XP_BASE - _HALO + i:_XP_BASE - _HALO + i + lc, :] * convw_ref[i:i + 1, :]
    xp_ref[_XP_BASE - _HALO:_XP_BASE, :] = xp_ref[_XP_BASE + lc - _HALO:_XP_BASE + lc, :]
    act = conv * _sigmoid(conv)
    xs = act[:, :SSD_INNER]
    bm = act[:, SSD_INNER:SSD_INNER + SSD_GROUPS * D_STATE]
    cm = act[:, SSD_INNER + SSD_GROUPS * D_STATE:]

    dt = _softplus(sm_ref[0][:, SM_DT:SM_DT + SSD_HEADS] + hp_ref[0:1, :])
    dt_t = _softplus(dtT_ref[0] + hpT_ref[:, 0:1])
    if t_valid < lc:
        dt = jnp.where(lax.broadcasted_iota(I32, (lc, SSD_HEADS), 0) < t_valid, dt, 0.0)
        dt_t = jnp.where(lax.broadcasted_iota(I32, (SSD_HEADS, lc), 1) < t_valid, dt_t, 0.0)
    a = dt * (-jnp.exp(hp_ref[1:2, :]))
    a_t = dt_t * (-jnp.exp(hpT_ref[:, 1:2]))
    ri = lax.broadcasted_iota(I32, (lc, lc), 0)
    ci = lax.broadcasted_iota(I32, (lc, lc), 1)
    causal = ci <= ri
    acum = _dot_exact(causal.astype(F32), a)
    acum_t = _dot_exact(a_t, (ri <= ci).astype(F32))
    e_a = jnp.exp(acum)
    w_end = jnp.exp(acum[lc - 1:lc, :] - acum) * dt
    e_a_full = _dot_exact(e_a, expand_ref[...])
    w_end_full = _dot_exact(w_end, expand_ref[...])
    xw = xs * w_end_full
    lane = lax.broadcasted_iota(I32, (lc, 2 * SSD_HEAD_DIM), 1)
    gw = HEADS_PER_GROUP * SSD_HEAD_DIM

    for g in range(SSD_GROUPS):
        b_g = bm[:, g * D_STATE:(g + 1) * D_STATE]
        c_g = cm[:, g * D_STATE:(g + 1) * D_STATE].astype(MXU_DTYPE)
        b_gt = b_g.T.astype(MXU_DTYPE)
        cb = jnp.dot(c_g, b_gt, preferred_element_type=F32)
        state = st_ref[g]
        y_off = _dot(c_g, state) * e_a_full[:, g * gw:(g + 1) * gw]
        st_ref[g] = state * e_a_full[lc - 1:lc, g * gw:(g + 1) * gw] + _dot(b_gt, xw[:, g * gw:(g + 1) * gw])
        for jp in range(HEADS_PER_GROUP // 2):
            h0 = g * HEADS_PER_GROUP + 2 * jp
            ms = []
            for h in (h0, h0 + 1):
                seg = acum[:, h:h + 1] - acum_t[h:h + 1, :]
                decay = jnp.exp(jnp.where(causal, seg, -jnp.inf))
                ms.append((cb * decay * dt_t[h:h + 1, :]).astype(MXU_DTYPE))
            c0 = h0 * SSD_HEAD_DIM
            x_pair = xs[:, c0:c0 + 2 * SSD_HEAD_DIM]
            r = jnp.dot(jnp.concatenate(ms, axis=0), x_pair.astype(MXU_DTYPE), preferred_element_type=F32)
            y_diag = jnp.where(lane < SSD_HEAD_DIM, r[:lc], r[lc:])
            o0 = 2 * jp * SSD_HEAD_DIM
            y_ref[:, c0:c0 + 2 * SSD_HEAD_DIM] = (y_diag + y_off[:, o0:o0 + 2 * SSD_HEAD_DIM]
                                                  + dfull_ref[:, c0:c0 + 2 * SSD_HEAD_DIM] * x_pair)

    zv = z_ref[0]
    hg = y_ref[...] * (zv * _sigmoid(zv))
    for g in range(SSD_GROUPS):
        sl = hg[:, g * gw:(g + 1) * gw]
        ms = jnp.mean(sl * sl, axis=-1, keepdims=True)
        yn_ref[0, :, g * gw:(g + 1) * gw] = (sl * lax.rsqrt(ms + RMS_EPS) * nw_ref[:, g * gw:(g + 1) * gw]).astype(yn_ref.dtype)

    @pl.when(c == nc - 1)
    def _():
        ssm_out_ref[0] = st_ref[...]


def _ssd(xbc, z, sm, dt_t, conv0, ssm0_t, conv_w, conv_b, dt_bias, a_log, d_skip, norm_w, lc, t_valid):
    b, t, _ = xbc.shape
    nc = t // lc
    hp = jnp.stack([dt_bias, a_log]).astype(F32)
    d_full = jnp.repeat(d_skip.astype(F32), SSD_HEAD_DIM)[None, :]
    expand = jnp.repeat(jnp.eye(SSD_HEADS, dtype=F32), SSD_HEAD_DIM, axis=1)
    gw = HEADS_PER_GROUP * SSD_HEAD_DIM
    const2 = lambda bi, ci: (0, 0)
    kern = functools.partial(_ssd_kernel, lc=lc, t_valid=t_valid)
    return pl.pallas_call(
        kern,
        grid=(b, nc),
        in_specs=[pl.BlockSpec((1, lc, CONV_DIM), lambda bi, ci: (bi, ci, 0)),
                  pl.BlockSpec((1, lc, SSD_INNER), lambda bi, ci: (bi, ci, 0)),
                  pl.BlockSpec((1, lc, LANES), lambda bi, ci: (bi, ci, 0)),
                  pl.BlockSpec((1, SSD_HEADS, lc), lambda bi, ci: (bi, 0, ci)),
                  pl.BlockSpec((1, _HALO, CONV_DIM), lambda bi, ci: (bi, 0, 0)),
                  pl.BlockSpec((1, SSD_GROUPS, D_STATE, gw), lambda bi, ci: (bi, 0, 0, 0)),
                  pl.BlockSpec((CONV_W, CONV_DIM), const2),
                  pl.BlockSpec((1, CONV_DIM), const2),
                  pl.BlockSpec((2, SSD_HEADS), const2),
                  pl.BlockSpec((SSD_HEADS, 2), const2),
                  pl.BlockSpec((1, SSD_INNER), const2),
                  pl.BlockSpec((1, SSD_INNER), const2),
                  pl.BlockSpec((SSD_HEADS, SSD_INNER), const2)],
        out_specs=(pl.BlockSpec((1, lc, SSD_INNER), lambda bi, ci: (bi, ci, 0)),
                   pl.BlockSpec((1, SSD_GROUPS, D_STATE, gw), lambda bi, ci: (bi, 0, 0, 0))),
        out_shape=(jax.ShapeDtypeStruct((b, t, SSD_INNER), MXU_DTYPE),
                   jax.ShapeDtypeStruct((b, SSD_GROUPS, D_STATE, gw), F32)),
        scratch_shapes=[pltpu.VMEM((_XP_BASE + lc, CONV_DIM), F32),
                        pltpu.VMEM((SSD_GROUPS, D_STATE, gw), F32),
                        pltpu.VMEM((lc, SSD_INNER), F32)],
        compiler_params=_cparams(2),
        name="ssd",
    )(xbc, z, sm, dt_t, conv0, ssm0_t, conv_w.astype(F32), conv_b.astype(F32)[None, :], hp, hp.T,
      d_full, norm_w.astype(F32)[None, :], expand)


def _state_to_kernel_layout(s):
    b = s.shape[0]
    s = s.reshape(b, SSD_GROUPS, HEADS_PER_GROUP, SSD_HEAD_DIM, D_STATE)
    return s.transpose(0, 1, 4, 2, 3).reshape(b, SSD_GROUPS, D_STATE, HEADS_PER_GROUP * SSD_HEAD_DIM)


def _state_from_kernel_layout(s):
    b = s.shape[0]
    s = s.reshape(b, SSD_GROUPS, D_STATE, HEADS_PER_GROUP, SSD_HEAD_DIM)
    return s.transpose(0, 1, 3, 4, 2).reshape(b, SSD_HEADS, SSD_HEAD_DIM, D_STATE)


_T5_LOG_STARTS = (12, 16, 23, 32, 46, 64, 91)
_T5_FAR_BUCKET = REL_BUCKETS // 2 - 1


def _bias_kernel(rb_ref, out_ref, *, tk):
    half = REL_BUCKETS // 2
    max_exact = half // 2
    ri = lax.broadcasted_iota(I32, (tk, tk), 0)
    ci = lax.broadcasted_iota(I32, (tk, tk), 1)
    for t in range(2):
        rel = ci - ri - (1 - t) * tk
        n = jnp.abs(rel)
        large = jnp.full((tk, tk), max_exact, I32)
        for start in _T5_LOG_STARTS:
            large = large + (n >= start).astype(I32)
        bucket = jnp.where(rel > 0, half, 0) + jnp.where(n < max_exact, n, large)
        for h in range(N_HEADS):
            acc = jnp.zeros((tk, tk), F32)
            for bkt in range(REL_BUCKETS):
                acc = jnp.where(bucket == bkt, rb_ref[bkt, h], acc)
            out_ref[t, h] = acc - rb_ref[_T5_FAR_BUCKET, h]


def _bias_tiles(rel_bias, tk):
    return pl.pallas_call(
        functools.partial(_bias_kernel, tk=tk),
        in_specs=[pl.BlockSpec(memory_space=pltpu.SMEM)],
        out_specs=pl.BlockSpec(memory_space=pltpu.VMEM),
        out_shape=jax.ShapeDtypeStruct((2, N_HEADS, tk, tk), F32),
        name="bias_tiles",
    )(rel_bias.astype(F32))


_MAX_BISECTIONS = 40
_BF16_EXACT_COUNT = 256
_SCORE_COLS = 256


def _select_kernel(qi_ref, sm_ref, kit_ref, mask_ref, key_ref, coarse_ref, w_ref, *, tq, wk, s_len, q_start, n_sel):
    qb = pl.program_id(1)
    n_blocks = mask_ref.shape[2] // wk
    nt = wk // LANES
    p0 = q_start + qb * tq
    pos = p0 + lax.broadcasted_iota(I32, (tq, 1), 0)
    vis_end = (pos // CHUNK + 1) * CHUNK
    vis_max = jnp.minimum(((p0 + tq - 1) // CHUNK + 1) * CHUNK, s_len)
    nkb = (vis_max + wk - 1) // wk

    wi = sm_ref[0][:, SM_WI:SM_WI + IDX_HEADS] * IDX_HEADS ** -0.5
    for hh in range(IDX_HEADS):
        w_ref[hh] = jnp.broadcast_to(wi[:, hh:hh + 1], (tq, LANES))
    qi = qi_ref[0]
    lane_k = lax.broadcasted_iota(I32, (1, LANES), 1)

    def score_block(j, masked):
        k0 = pl.multiple_of(j * wk, wk)
        for c in range(wk // _SCORE_COLS):
            c0 = k0 + c * _SCORE_COLS
            kt = kit_ref[0, :, pl.ds(c0, _SCORE_COLS)]
            scs = [jnp.zeros((tq, LANES), F32) for _ in range(_SCORE_COLS // LANES)]
            for hh in range(IDX_HEADS):
                d = jnp.dot(qi[:, hh * IDX_DIM:(hh + 1) * IDX_DIM], kt, preferred_element_type=F32)
                d = jnp.maximum(d, 0.0)
                w = w_ref[hh]
                scs = [s + w * d[:, u * LANES:(u + 1) * LANES] for u, s in enumerate(scs)]
            for u, sc in enumerate(scs):
                if masked:
                    key_pos = c0 + u * LANES + lane_k
                    sc = jnp.where((key_pos < vis_end) & (key_pos < s_len), sc, -jnp.inf)
                bits = lax.bitcast_convert_type(sc, I32)
                sl = pl.ds(c0 + u * LANES, LANES)
                key_ref[:, sl] = jnp.where(bits < 0, INT_MIN - bits, bits)
                coarse_ref[:, sl] = lax.bitcast_convert_type(bits & -65536, F32).astype(jnp.bfloat16)

    def open_body(j, carry):
        score_block(j, masked=False)
        return carry

    def edge_body(j, carry):
        score_block(j, masked=True)
        return carry

    n_open = jnp.minimum(p0, s_len) // wk
    lax.fori_loop(0, n_open, open_body, 0)
    lax.fori_loop(n_open, nkb, edge_body, 0)

    def count_ge(thr):
        thr_b = jnp.broadcast_to(thr, (tq, LANES))

        def body(j, acc):
            k0 = pl.multiple_of(j * wk, wk)
            for u in range(nt):
                acc = acc + (key_ref[:, pl.ds(k0 + u * LANES, LANES)] >= thr_b).astype(I32)
            return acc

        acc = lax.fori_loop(0, nkb, body, jnp.zeros((tq, LANES), I32))
        return jnp.sum(acc, axis=1, keepdims=True)

    def count_ge_coarse(thr16):
        pattern = jnp.where(thr16 < 0, thr16 ^ 0x7FFF, thr16) << 16
        thr_b = jnp.broadcast_to(lax.bitcast_convert_type(pattern, F32), (tq, LANES)).astype(jnp.bfloat16)
        one = jnp.ones((tq, LANES), jnp.bfloat16)
        zero = jnp.zeros((tq, LANES), jnp.bfloat16)

        def body(j, acc):
            k0 = pl.multiple_of(j * wk, wk)
            for u in range(nt):
                acc = acc + jnp.where(coarse_ref[:, pl.ds(k0 + u * LANES, LANES)] >= thr_b, one, zero)
            return acc

        acc = lax.fori_loop(0, nkb, body, zero)
        return jnp.sum(acc.astype(F32), axis=1, keepdims=True).astype(I32)

    def unfinished(lo, hi, cnt_lo):
        return (cnt_lo > n_sel) & (hi - 1 > lo)

    def bisect(count_fn, lo, hi, cnt_lo, cnt_hi):
        def cond(st):
            return (st[0] < _MAX_BISECTIONS) & (st[1] > 0)

        def body(st):
            it, _, lo, hi, cnt_lo, cnt_hi = st
            live = unfinished(lo, hi, cnt_lo)
            mid = (lo >> 1) + (hi >> 1) + (lo & hi & 1)
            cnt = count_fn(jnp.where(live, mid, lo))
            up = live & (cnt >= n_sel)
            dn = live & (cnt < n_sel)
            lo = jnp.where(up, mid, lo)
            cnt_lo = jnp.where(up, cnt, cnt_lo)
            hi = jnp.where(dn, mid, hi)
            cnt_hi = jnp.where(dn, cnt, cnt_hi)
            return (it + 1, jnp.max(unfinished(lo, hi, cnt_lo).astype(I32)), lo, hi, cnt_lo, cnt_hi)

        st0 = (jnp.int32(0), jnp.max(unfinished(lo, hi, cnt_lo).astype(I32)), lo, hi, cnt_lo, cnt_hi)
        return lax.while_loop(cond, body, st0)[2:]

    lo_fin = (KEY_NEG_INF + 1) >> 16
    cnt_fin = count_ge_coarse(jnp.full((tq, 1), lo_fin, I32))
    cnt_pos = count_ge_coarse(jnp.zeros((tq, 1), I32))
    pos_enough = cnt_pos >= n_sel
    lo_c = jnp.where(pos_enough, 0, lo_fin).astype(I32)
    cnt_lo_c = jnp.where(pos_enough, cnt_pos, cnt_fin)
    hi_c = jnp.where(pos_enough, INT_MAX >> 16, 0).astype(I32)
    cnt_hi_c = jnp.where(pos_enough, 0, cnt_pos)
    lo_c, hi_c, _, _ = bisect(count_ge_coarse, lo_c, hi_c, cnt_lo_c, cnt_hi_c)
    lo, hi = jnp.maximum(lo_c << 16, KEY_NEG_INF + 1), hi_c << 16
    cnt_lo, cnt_hi = count_ge(lo), count_ge(hi)
    bad = (cnt_lo < n_sel) | (cnt_hi >= n_sel)
    lo_all = jnp.full((tq, 1), KEY_NEG_INF + 1, I32)
    cnt_all = lax.cond(jnp.max(bad.astype(I32)) > 0, lambda: count_ge(lo_all), lambda: jnp.zeros((tq, 1), I32))
    lo = jnp.where(bad, lo_all, lo)
    cnt_lo = jnp.where(bad, cnt_all, cnt_lo)
    hi = jnp.where(bad, INT_MAX, hi)
    cnt_hi = jnp.where(bad, 0, cnt_hi)
    tau, _, cnt_tau, cnt_above = bisect(count_ge, lo, hi, cnt_lo, cnt_hi)

    tie_rows = cnt_tau > n_sel
    has_ties = jnp.max(tie_rows.astype(I32)) > 0

    @pl.when(jnp.logical_not(has_ties))
    def _():
        tau_b = jnp.broadcast_to(tau, (tq, LANES))

        def body(j, carry):
            k0 = pl.multiple_of(j * wk, wk)
            for u in range(nt):
                sl = pl.ds(k0 + u * LANES, LANES)
                mask_ref[0, :, sl] = (key_ref[:, sl] >= tau_b).astype(I32).astype(mask_ref.dtype)
            return carry
        lax.fori_loop(0, nkb, body, 0)

    @pl.when(has_ties)
    def _():
        upper = (lax.broadcasted_iota(I32, (wk, wk), 0) <= lax.broadcasted_iota(I32, (wk, wk), 1)).astype(MXU_DTYPE)
        need = jnp.where(tie_rows, n_sel - cnt_above, wk * n_blocks).astype(F32)

        def body(j, seen):
            k0 = pl.multiple_of(j * wk, wk)
            blk = key_ref[:, pl.ds(k0, wk)]
            eq = blk == tau
            rank = seen + jnp.dot(eq.astype(F32).astype(MXU_DTYPE), upper, preferred_element_type=F32)
            sel = (blk > tau) | (eq & (rank <= need))
            mask_ref[0, :, pl.ds(k0, wk)] = sel.astype(I32).astype(mask_ref.dtype)
            return seen + jnp.sum(eq.astype(F32), axis=1, keepdims=True)
        lax.fori_loop(0, nkb, body, jnp.zeros((tq, 1), F32))

    def zero_body(j, carry):
        k0 = pl.multiple_of(j * wk, wk)
        mask_ref[0, :, pl.ds(k0, wk)] = jnp.zeros((tq, wk), mask_ref.dtype)
        return carry

    lax.fori_loop(nkb, n_blocks, zero_body, 0)


def _select(qi, sm, kit, tq, wk, s_len, q_start, n_sel):
    b, t, _ = qi.shape
    s_pad = kit.shape[2]
    assert s_pad // LANES <= _BF16_EXACT_COUNT, "per-lane coarse counts must stay exact in bfloat16"
    kern = functools.partial(_select_kernel, tq=tq, wk=wk, s_len=s_len, q_start=q_start, n_sel=n_sel)
    return pl.pallas_call(
        kern,
        grid=(b, t // tq),
        in_specs=[pl.BlockSpec((1, tq, IDX_HEADS * IDX_DIM), lambda bi, qi_: (bi, qi_, 0)),
                  pl.BlockSpec((1, tq, LANES), lambda bi, qi_: (bi, qi_, 0)),
                  _per_batch((1, IDX_DIM, s_pad), lambda bi, qi_: (bi, 0, 0), t // tq)],
        out_specs=pl.BlockSpec((1, tq, s_pad), lambda bi, qi_: (bi, qi_, 0)),
        out_shape=jax.ShapeDtypeStruct((b, t, s_pad), jnp.int8),
        scratch_shapes=[pltpu.VMEM((tq, s_pad), I32),
                        pltpu.VMEM((tq, s_pad), jnp.bfloat16),
                        pltpu.VMEM((IDX_HEADS, tq, LANES), F32)],
        compiler_params=_cparams(2),
        name="select",
    )(qi, sm, kit)


_V_AUG = 2 * HEAD_DIM


def _attn_kernel(q_ref, kt_ref, v_ref, mask_ref, bias_ref, o_ref, m_ref, acc_ref, qg_ref,
                 *, tq, tk, wide, q_start):
    qb = pl.program_id(1)
    jd = (q_start + qb * tq) // tk

    for g in range(N_KV_HEADS):
        for r in range(Q_PER_KV):
            c0 = (g * Q_PER_KV + r) * HEAD_DIM
            qg_ref[g, r * tq:(r + 1) * tq, :] = q_ref[0, :, c0:c0 + HEAD_DIM]
    m_ref[...] = jnp.full(m_ref.shape, NEG_BIG, F32)
    acc_ref[...] = jnp.zeros(acc_ref.shape, F32)

    def process(k0, width, near):
        nt = width // LANES
        mb = jnp.where(mask_ref[0, :, pl.ds(k0, width)].astype(I32) != 0, 0.0, NEG_BIG)
        mb = jnp.concatenate([mb] * Q_PER_KV, axis=0)
        for g in range(N_KV_HEADS):
            kt = kt_ref[0, g * HEAD_DIM:(g + 1) * HEAD_DIM, pl.ds(k0, width)]
            s = jnp.dot(qg_ref[g], kt, preferred_element_type=F32) + mb
            if near is not None:
                s = s + bias_ref[near, g]
            tiles = [s[:, u * LANES:(u + 1) * LANES] for u in range(nt)]
            mx = tiles[0]
            for u in range(1, nt):
                mx = jnp.maximum(mx, tiles[u])
            m_old = m_ref[g]
            m_new = jnp.maximum(m_old, jnp.max(mx, axis=1, keepdims=True))
            alpha = jnp.exp(m_old - m_new)
            p = jnp.concatenate([jnp.exp(tl - m_new).astype(MXU_DTYPE) for tl in tiles], axis=1)
            vt = v_ref[0, pl.ds(k0, width), g * _V_AUG:(g + 1) * _V_AUG]
            acc_ref[g] = alpha * acc_ref[g] + jnp.dot(p, vt, preferred_element_type=F32)
            m_ref[g] = m_new

    n_far = jnp.maximum(jd - 1, 0)
    per_wide = wide // tk
    n_wide = n_far // per_wide

    def wide_body(j, carry):
        process(pl.multiple_of(j * wide, wide), wide, None)
        return carry

    def far_body(j, carry):
        process(pl.multiple_of(j * tk, tk), tk, None)
        return carry

    lax.fori_loop(0, n_wide, wide_body, 0)
    lax.fori_loop(n_wide * per_wide, n_far, far_body, 0)

    @pl.when(jd >= 1)
    def _():
        process(pl.multiple_of((jd - 1) * tk, tk), tk, 0)

    process(pl.multiple_of(jd * tk, tk), tk, 1)

    for g in range(N_KV_HEADS):
        acc = acc_ref[g]
        o = acc[:, :HEAD_DIM] / acc[:, HEAD_DIM:HEAD_DIM + 1]
        for r in range(Q_PER_KV):
            c0 = (g * Q_PER_KV + r) * HEAD_DIM
            o_ref[0, :, c0:c0 + HEAD_DIM] = o[r * tq:(r + 1) * tq].astype(o_ref.dtype)


def _attention(q, kt, v_aug, mask, bias, tq, tk, wide, q_start):
    b, t, _ = q.shape
    s_pad = kt.shape[2]
    rows = Q_PER_KV * tq
    kern = functools.partial(_attn_kernel, tq=tq, tk=tk, wide=wide, q_start=q_start)
    return pl.pallas_call(
        kern,
        grid=(b, t // tq),
        in_specs=[pl.BlockSpec((1, tq, N_HEADS * HEAD_DIM), lambda bi, qi_: (bi, qi_, 0)),
                  _per_batch((1, KV_DIM, s_pad), lambda bi, qi_: (bi, 0, 0), t // tq),
                  _per_batch((1, s_pad, N_KV_HEADS * _V_AUG), lambda bi, qi_: (bi, 0, 0), t // tq),
                  pl.BlockSpec((1, tq, s_pad), lambda bi, qi_: (bi, qi_, 0)),
                  _resident((2, N_KV_HEADS, rows, tk), lambda bi, qi_: (0, 0, 0, 0))],
        out_specs=pl.BlockSpec((1, tq, N_HEADS * HEAD_DIM), lambda bi, qi_: (bi, qi_, 0)),
        out_shape=jax.ShapeDtypeStruct((b, t, N_HEADS * HEAD_DIM), MXU_DTYPE),
        scratch_shapes=[pltpu.VMEM((N_KV_HEADS, rows, LANES), F32),
                        pltpu.VMEM((N_KV_HEADS, rows, _V_AUG), F32),
                        pltpu.VMEM((N_KV_HEADS, rows, HEAD_DIM), MXU_DTYPE)],
        compiler_params=_cparams(2),
        name="attention",
    )(q, kt, v_aug, mask, bias)


def _layer_norm(x, g, b):
    mu = jnp.mean(x, axis=-1, keepdims=True)
    xc = x - mu
    var = jnp.mean(xc * xc, axis=-1, keepdims=True)
    return xc * lax.rsqrt(var + LN_EPS) * g + b


def _post_kernel(o_ref, yn_ref, gs_ref, ga_ref, x_ref, wso_ref, wao_ref, wout_ref, wup_ref, wdown_ref,
                 ln_ref, out_ref):
    y1 = jnp.dot(yn_ref[...], wso_ref[...], preferred_element_type=F32)
    y2 = jnp.dot(o_ref[...], wao_ref[...], preferred_element_type=F32)
    mix = _sigmoid(gs_ref[...]) * y1 + _sigmoid(ga_ref[...]) * y2
    mixed = _dot(mix, wout_ref[...])
    h = _layer_norm(ALPHA * x_ref[...] + mixed, ln_ref[0:1, :], ln_ref[1:2, :])
    u = jnp.maximum(_dot(h, wup_ref[...]), 0.0)
    f = _dot(u * u, wdown_ref[...])
    out_ref[...] = _layer_norm(ALPHA * h + f, ln_ref[2:3, :], ln_ref[3:4, :])


def _post(o, yn, gs, ga, x2d, w_ssd_o, w_attn_o, w_out, w_up, w_down, ln, tm):
    m = x2d.shape[0]
    row = lambda w: pl.BlockSpec((tm, w), lambda i: (i, 0))
    const = lambda i: (0, 0)
    return pl.pallas_call(
        _post_kernel,
        grid=(m // tm,),
        in_specs=[row(N_HEADS * HEAD_DIM), row(SSD_INNER), row(D_MODEL), row(D_MODEL), row(D_MODEL),
                  _resident((SSD_INNER, D_MODEL), const),
                  _resident((N_HEADS * HEAD_DIM, D_MODEL), const),
                  _resident((D_MODEL, D_MODEL), const),
                  _resident((D_MODEL, D_FF), const),
                  _resident((D_FF, D_MODEL), const),
                  _resident((4, D_MODEL), const)],
        out_specs=row(D_MODEL),
        out_shape=jax.ShapeDtypeStruct((m, D_MODEL), F32),
        compiler_params=_cparams(1),
        name="post",
    )(o, yn, gs, ga, x2d, w_ssd_o, w_attn_o, w_out, w_up, w_down, ln)


def _round_up(n, m):
    return (n + m - 1) // m * m


def _layer(x, cache_k, cache_v, cache_kidx, state_ssm, state_conv, bias_tiles, w_prep, conv_w, conv_b,
           dt_bias, a_log, d_skip, ssd_norm_w, w_ssd_o, w_attn_o, w_out, w_up, w_down, ln,
           *, tq, tk, wide, tqs, wk, lc, tm):
    b, t, _ = x.shape
    past = cache_k.shape[1]
    s_len = past + t
    m = b * t
    assert m % tm == 0 and t % tq == 0 and past % tk == 0 and tk % LANES == 0 and wide % tk == 0
    assert t % tqs == 0 and wk % LANES == 0
    assert tq == tk or t == tq, "every query tile must start on a key-tile boundary"
    assert t % lc == 0 or t < lc
    x2d = x.reshape(m, D_MODEL)

    z, xbc, q, k, v, qi, gs, ga, sm = _inproj(x2d, w_prep, tm)

    t_pad = _round_up(t, lc)
    pad_t = lambda a: jnp.pad(a.reshape(b, t, a.shape[-1]), ((0, 0), (0, t_pad - t), (0, 0)))
    sm3 = pad_t(sm)
    dt_t = sm3[:, :, SM_DT:SM_DT + SSD_HEADS].transpose(0, 2, 1)
    yn, ssm_new = _ssd(pad_t(xbc), pad_t(z), sm3, dt_t, state_conv.astype(F32),
                       _state_to_kernel_layout(state_ssm.astype(F32)), conv_w, conv_b, dt_bias, a_log, d_skip,
                       ssd_norm_w, lc, min(t, lc))
    yn = yn[:, :t].reshape(m, SSD_INNER)
    new_ssm = _state_from_kernel_layout(ssm_new)
    xbc3 = xbc.reshape(b, t, CONV_DIM)
    new_conv = jnp.concatenate([state_conv.astype(F32), xbc3], axis=1)[:, -_HALO:]

    k3 = k.reshape(b, t, KV_DIM)
    v3 = v.reshape(b, t, KV_DIM)
    ki3 = sm.reshape(b, t, LANES)[:, :, SM_KI:SM_KI + IDX_DIM]
    s_pad = _round_up(s_len, max(wide, wk))
    assert s_pad % wide == 0 and s_pad % wk == 0
    pad_s = lambda a: jnp.pad(a, ((0, 0), (0, s_pad - s_len)) + ((0, 0),) * (a.ndim - 2))
    k_all = pad_s(jnp.concatenate([cache_k.reshape(b, past, KV_DIM).astype(MXU_DTYPE), k3.astype(MXU_DTYPE)], axis=1))
    v_all = jnp.concatenate([cache_v.astype(MXU_DTYPE), v3.reshape(b, t, N_KV_HEADS, HEAD_DIM).astype(MXU_DTYPE)], axis=1)
    v_aug = pad_s(jnp.concatenate([v_all, jnp.ones((b, s_len, N_KV_HEADS, 1), MXU_DTYPE),
                                   jnp.zeros((b, s_len, N_KV_HEADS, _V_AUG - HEAD_DIM - 1), MXU_DTYPE)], axis=3))
    ki_all = pad_s(jnp.concatenate([cache_kidx.astype(MXU_DTYPE), ki3.astype(MXU_DTYPE)], axis=1))
    n_sel = min(TOPK_MAX, s_len // 4)
    mask = _select(qi.reshape(b, t, -1), sm.reshape(b, t, LANES), ki_all.transpose(0, 2, 1),
                   tqs, wk, s_len, past, n_sel)
    bias = bias_tiles[:, :, :tq, :].reshape(2, N_KV_HEADS, Q_PER_KV * tq, tk)
    o = _attention(q.reshape(b, t, -1), k_all.transpose(0, 2, 1), v_aug.reshape(b, s_pad, N_KV_HEADS * _V_AUG),
                   mask, bias, tq, tk, wide, past)

    out = _post(o.reshape(m, -1), yn, gs, ga, x2d, w_ssd_o, w_attn_o, w_out, w_up, w_down, ln, tm)
    return out.reshape(b, t, D_MODEL), (k3.reshape(b, t, N_KV_HEADS, HEAD_DIM), v3.reshape(b, t, N_KV_HEADS, HEAD_DIM),
                                        ki3, new_ssm, new_conv)


ATTN_TILE = 128
WIDE_TILE = 1024
WIDE_TILE_SHORT = 512
SELECT_ROWS = 128
SELECT_WIDTH = 512
SSD_CHUNK = 128
ROW_TILE = 256


def kernel(x_prompt, x_sample, cache_k, cache_v, cache_kidx, state_ssm, state_conv, rel_bias, w_in, conv_w, conv_b,
           dt_bias, a_log, d_skip, ssd_norm_w, w_ssd_o, w_attn_o, w_out, ln1_g, ln1_b, w_up, w_down, ln2_g, ln2_b):
    bp = x_prompt.shape[0]
    dtype = x_prompt.dtype
    bias_tiles = _bias_tiles(rel_bias, ATTN_TILE)
    yp, ys = x_prompt, x_sample
    st_p, st_s = [], []
    tiles = dict(tk=ATTN_TILE, wk=SELECT_WIDTH, lc=SSD_CHUNK, tm=ROW_TILE)
    for l in range(DEPTH):
        w_prep = _prep_w_in(w_in[l])
        ln = jnp.stack([ln1_g[l], ln1_b[l], ln2_g[l], ln2_b[l]]).astype(F32)
        weights = (bias_tiles, w_prep, conv_w[l], conv_b[l], dt_bias[l], a_log[l], d_skip[l], ssd_norm_w[l],
                   w_ssd_o[l].astype(MXU_DTYPE), w_attn_o[l].astype(MXU_DTYPE), w_out[l].astype(MXU_DTYPE),
                   w_up[l].astype(MXU_DTYPE), w_down[l].astype(MXU_DTYPE), ln)
        yp, sp = _layer(yp, jnp.zeros((bp, 0, N_KV_HEADS, HEAD_DIM), dtype), jnp.zeros((bp, 0, N_KV_HEADS, HEAD_DIM), dtype),
                        jnp.zeros((bp, 0, IDX_DIM), dtype), jnp.zeros((bp, SSD_HEADS, SSD_HEAD_DIM, D_STATE), dtype),
                        jnp.zeros((bp, _HALO, CONV_DIM), dtype), *weights,
                        tq=min(ATTN_TILE, yp.shape[1]), tqs=min(SELECT_ROWS, yp.shape[1]), wide=WIDE_TILE, **tiles)
        ys, ss = _layer(ys, cache_k[l], cache_v[l], cache_kidx[l], state_ssm[l], state_conv[l], *weights,
                        tq=min(ATTN_TILE, ys.shape[1]), tqs=min(SELECT_ROWS, ys.shape[1]), wide=WIDE_TILE_SHORT, **tiles)
        st_p.append(sp)
        st_s.append(ss)
    stack = lambda sts, i: jnp.stack([s[i] for s in sts])
    return (yp, ys) + tuple(stack(st_p, i) for i in range(5)) + tuple(stack(st_s, i) for i in range(5))
```

```python
import functools

import jax
import jax.numpy as jnp
from jax import lax
from jax.experimental import pallas as pl
from jax.experimental.pallas import tpu as pltpu

F32 = jnp.float32
I32 = jnp.int32
MXU_DTYPE = jnp.bfloat16

D_MODEL = 1024
CHUNK = 64
SSD_INNER = 2048
SSD_HEAD_DIM = 64
SSD_HEADS = 32
SSD_GROUPS = 4
HEADS_PER_GROUP = SSD_HEADS // SSD_GROUPS
D_STATE = 128
CONV_W = 4
CONV_DIM = SSD_INNER + 2 * SSD_GROUPS * D_STATE
N_HEADS = 16
N_KV_HEADS = 4
Q_PER_KV = N_HEADS // N_KV_HEADS
HEAD_DIM = 64
KV_DIM = N_KV_HEADS * HEAD_DIM
IDX_HEADS = 8
IDX_DIM = 64
TOPK_MAX = 256
REL_BUCKETS = 32
REL_MAX_DIST = 128
D_FF = 4 * D_MODEL
DEPTH = 1
ALPHA = (2.0 * DEPTH) ** 0.25
LN_EPS = 1e-5
RMS_EPS = 1e-5
IN_SIZES = (SSD_INNER, CONV_DIM, SSD_HEADS, N_HEADS * HEAD_DIM, KV_DIM, KV_DIM,
            IDX_HEADS * IDX_DIM, IDX_DIM, IDX_HEADS, D_MODEL, D_MODEL)

LANES = 128
VMEM_LIMIT_BYTES = 56 * 1024 * 1024

SM_KI = 0
SM_DT = IDX_DIM
SM_WI = IDX_DIM + SSD_HEADS
SM_PAD = LANES - (IDX_DIM + SSD_HEADS + IDX_HEADS)

NEG_BIG = -1e30
INT_MIN = -2 ** 31
INT_MAX = 2 ** 31 - 1
KEY_NEG_INF = -0x7F800000


def _cparams(n_axes):
    return pltpu.CompilerParams(dimension_semantics=("arbitrary",) * n_axes,
                                vmem_limit_bytes=VMEM_LIMIT_BYTES)


def _resident(shape, index_map):
    return pl.BlockSpec(shape, index_map, pipeline_mode=pl.Buffered(1))


def _per_batch(shape, index_map, steps_per_batch):
    if steps_per_batch > 1:
        return _resident(shape, index_map)
    return pl.BlockSpec(shape, index_map)


def _dot(a, b):
    return jnp.dot(a.astype(MXU_DTYPE), b.astype(MXU_DTYPE), preferred_element_type=F32)


def _dot_exact(a, b):
    return jnp.dot(a, b, preferred_element_type=F32, precision=lax.Precision.HIGHEST)


def _sigmoid(x):
    return 1.0 / (1.0 + jnp.exp(-x))


def _softplus(x):
    return jnp.maximum(x, 0.0) + jnp.log1p(jnp.exp(-jnp.abs(x)))


_PROJ_WIDTHS = (SSD_INNER, CONV_DIM, N_HEADS * HEAD_DIM, KV_DIM, KV_DIM, IDX_HEADS * IDX_DIM,
                D_MODEL, D_MODEL, LANES)
_PROJ_WIDTH = sum(_PROJ_WIDTHS)
_PROJ_CHUNK = 512


def _prep_w_in(w_in):
    offs = [0]
    for s in IN_SIZES:
        offs.append(offs[-1] + s)
    z, xbc, dt, q, k, v, qi, ki, wi, gs, ga = [w_in[:, offs[i]:offs[i + 1]] for i in range(len(IN_SIZES))]
    small = jnp.concatenate([ki, dt, wi, jnp.zeros((D_MODEL, SM_PAD), w_in.dtype)], axis=1)
    w = jnp.concatenate([z, xbc, q * HEAD_DIM ** -0.5, k, v, qi * IDX_DIM ** -0.5, gs, ga, small], axis=1)
    return w.astype(MXU_DTYPE)


def _inproj_kernel(x_ref, w_ref, z_ref, xbc_ref, q_ref, k_ref, v_ref, qi_ref, gs_ref, ga_ref, sm_ref):
    xb = x_ref[...].astype(MXU_DTYPE)
    outs = (z_ref, xbc_ref, q_ref, k_ref, v_ref, qi_ref, gs_ref, ga_ref, sm_ref)
    c0 = 0
    for ref, width in zip(outs, _PROJ_WIDTHS):
        for o in range(0, width, _PROJ_CHUNK):
            n = min(_PROJ_CHUNK, width - o)
            r = jnp.dot(xb, w_ref[:, c0 + o:c0 + o + n], preferred_element_type=F32)
            ref[:, o:o + n] = r.astype(ref.dtype)
        c0 += width


def _inproj(x2d, w_prep, tm):
    m = x2d.shape[0]
    dtypes = (F32, F32, MXU_DTYPE, F32, F32, MXU_DTYPE, F32, F32, F32)
    out_shape = tuple(jax.ShapeDtypeStruct((m, w), d) for w, d in zip(_PROJ_WIDTHS, dtypes))
    out_specs = tuple(pl.BlockSpec((tm, w), lambda i: (i, 0)) for w in _PROJ_WIDTHS)
    return pl.pallas_call(
        _inproj_kernel,
        grid=(m // tm,),
        in_specs=[pl.BlockSpec((tm, D_MODEL), lambda i: (i, 0)),
                  _resident((D_MODEL, _PROJ_WIDTH), lambda i: (0, 0))],
        out_specs=out_specs,
        out_shape=out_shape,
        compiler_params=_cparams(1),
        name="inproj",
    )(x2d, w_prep)


_HALO = CONV_W - 1
_XP_BASE = 8


def _ssd_kernel(xbc_ref, z_ref, sm_ref, dtT_ref, conv0_ref, ssm0_ref, convw_ref, convb_ref,
                hp_ref, hpT_ref, dfull_ref, nw_ref, expand_ref,
                yn_ref, ssm_out_ref, xp_ref, st_ref, y_ref, *, lc, t_valid):
    c = pl.program_id(1)
    nc = pl.num_programs(1)

    @pl.when(c == 0)
    def _():
        xp_ref[_XP_BASE - _HALO:_XP_BASE, :] = conv0_ref[0]
        st_ref[...] = ssm0_ref[0]

    xp_ref[_XP_BASE:_XP_BASE + lc, :] = xbc_ref[0]
    conv = convb_ref[...]
    for i in range(CONV_W):
        conv = conv + xp_ref[_XP_BASE - _HALO + i:_XP_BASE - _HALO + i + lc, :] * convw_ref[i:i + 1, :]
    xp_ref[_XP_BASE - _HALO:_XP_BASE, :] = xp_ref[_XP_BASE + lc - _HALO:_XP_BASE + lc, :]
    act = conv * _sigmoid(conv)
    xs = act[:, :SSD_INNER]
    bm = act[:, SSD_INNER:SSD_INNER + SSD_GROUPS * D_STATE]
    cm = act[:, SSD_INNER + SSD_GROUPS * D_STATE:]

    dt = _softplus(sm_ref[0][:, SM_DT:SM_DT + SSD_HEADS] + hp_ref[0:1, :])
    dt_t = _softplus(dtT_ref[0] + hpT_ref[:, 0:1])
    if t_valid < lc:
        dt = jnp.where(lax.broadcasted_iota(I32, (lc, SSD_HEADS), 0) < t_valid, dt, 0.0)
        dt_t = jnp.where(lax.broadcasted_iota(I32, (SSD_HEADS, lc), 1) < t_valid, dt_t, 0.0)
    a = dt * (-jnp.exp(hp_ref[1:2, :]))
    a_t = dt_t * (-jnp.exp(hpT_ref[:, 1:2]))
    ri = lax.broadcasted_iota(I32, (lc, lc), 0)
    ci = lax.broadcasted_iota(I32, (lc, lc), 1)
    causal = ci <= ri
    acum = _dot_exact(causal.astype(F32), a)
    acum_t = _dot_exact(a_t, (ri <= ci).astype(F32))
    e_a = jnp.exp(acum)
    w_end = jnp.exp(acum[lc - 1:lc, :] - acum) * dt
    e_a_full = _dot_exact(e_a, expand_ref[...])
    w_end_full = _dot_exact(w_end, expand_ref[...])
    xw = xs * w_end_full
    lane = lax.broadcasted_iota(I32, (lc, 2 * SSD_HEAD_DIM), 1)
    gw = HEADS_PER_GROUP * SSD_HEAD_DIM

    for g in range(SSD_GROUPS):
        b_g = bm[:, g * D_STATE:(g + 1) * D_STATE]
        c_g = cm[:, g * D_STATE:(g + 1) * D_STATE].astype(MXU_DTYPE)
        b_gt = b_g.T.astype(MXU_DTYPE)
        cb = jnp.dot(c_g, b_gt, preferred_element_type=F32)
        state = st_ref[g]
        y_off = _dot(c_g, state) * e_a_full[:, g * gw:(g + 1) * gw]
        st_ref[g] = state * e_a_full[lc - 1:lc, g * gw:(g + 1) * gw] + _dot(b_gt, xw[:, g * gw:(g + 1) * gw])
        for jp in range(HEADS_PER_GROUP // 2):
            h0 = g * HEADS_PER_GROUP + 2 * jp
            ms = []
            for h in (h0, h0 + 1):
                seg = acum[:, h:h + 1] - acum_t[h:h + 1, :]
                decay = jnp.exp(jnp.where(causal, seg, -jnp.inf))
                ms.append((cb * decay * dt_t[h:h + 1, :]).astype(MXU_DTYPE))
            c0 = h0 * SSD_HEAD_DIM
            x_pair = xs[:, c0:c0 + 2 * SSD_HEAD_DIM]
            r = jnp.dot(jnp.concatenate(ms, axis=0), x_pair.astype(MXU_DTYPE), preferred_element_type=F32)
            y_diag = jnp.where(lane < SSD_HEAD_DIM, r[:lc], r[lc:])
            o0 = 2 * jp * SSD_HEAD_DIM
            y_ref[:, c0:c0 + 2 * SSD_HEAD_DIM] = (y_diag + y_off[:, o0:o0 + 2 * SSD_HEAD_DIM]
                                                  + dfull_ref[:, c0:c0 + 2 * SSD_HEAD_DIM] * x_pair)

    zv = z_ref[0]
    hg = y_ref[...] * (zv * _sigmoid(zv))
    for g in range(SSD_GROUPS):
        sl = hg[:, g * gw:(g + 1) * gw]
        ms = jnp.mean(sl * sl, axis=-1, keepdims=True)
        yn_ref[0, :, g * gw:(g + 1) * gw] = (sl * lax.rsqrt(ms + RMS_EPS) * nw_ref[:, g * gw:(g + 1) * gw]).astype(yn_ref.dtype)

    @pl.when(c == nc - 1)
    def _():
        ssm_out_ref[0] = st_ref[...]


def _ssd(xbc, z, sm, dt_t, conv0, ssm0_t, conv_w, conv_b, dt_bias, a_log, d_skip, norm_w, lc, t_valid):
    b, t, _ = xbc.shape
    nc = t // lc
    hp = jnp.stack([dt_bias, a_log]).astype(F32)
    d_full = jnp.repeat(d_skip.astype(F32), SSD_HEAD_DIM)[None, :]
    expand = jnp.repeat(jnp.eye(SSD_HEADS, dtype=F32), SSD_HEAD_DIM, axis=1)
    gw = HEADS_PER_GROUP * SSD_HEAD_DIM
    const2 = lambda bi, ci: (0, 0)
    kern = functools.partial(_ssd_kernel, lc=lc, t_valid=t_valid)
    return pl.pallas_call(
        kern,
        grid=(b, nc),
        in_specs=[pl.BlockSpec((1, lc, CONV_DIM), lambda bi, ci: (bi, ci, 0)),
                  pl.BlockSpec((1, lc, SSD_INNER), lambda bi, ci: (bi, ci, 0)),
                  pl.BlockSpec((1, lc, LANES), lambda bi, ci: (bi, ci, 0)),
                  pl.BlockSpec((1, SSD_HEADS, lc), lambda bi, ci: (bi, 0, ci)),
                  pl.BlockSpec((1, _HALO, CONV_DIM), lambda bi, ci: (bi, 0, 0)),
                  pl.BlockSpec((1, SSD_GROUPS, D_STATE, gw), lambda bi, ci: (bi, 0, 0, 0)),
                  pl.BlockSpec((CONV_W, CONV_DIM), const2),
                  pl.BlockSpec((1, CONV_DIM), const2),
                  pl.BlockSpec((2, SSD_HEADS), const2),
                  pl.BlockSpec((SSD_HEADS, 2), const2),
                  pl.BlockSpec((1, SSD_INNER), const2),
                  pl.BlockSpec((1, SSD_INNER), const2),
                  pl.BlockSpec((SSD_HEADS, SSD_INNER), const2)],
        out_specs=(pl.BlockSpec((1, lc, SSD_INNER), lambda bi, ci: (bi, ci, 0)),
                   pl.BlockSpec((1, SSD_GROUPS, D_STATE, gw), lambda bi, ci: (bi, 0, 0, 0))),
        out_shape=(jax.ShapeDtypeStruct((b, t, SSD_INNER), MXU_DTYPE),
                   jax.ShapeDtypeStruct((b, SSD_GROUPS, D_STATE, gw), F32)),
        scratch_shapes=[pltpu.VMEM((_XP_BASE + lc, CONV_DIM), F32),
                        pltpu.VMEM((SSD_GROUPS, D_STATE, gw), F32),
                        pltpu.VMEM((lc, SSD_INNER), F32)],
        compiler_params=_cparams(2),
        name="ssd",
    )(xbc, z, sm, dt_t, conv0, ssm0_t, conv_w.astype(F32), conv_b.astype(F32)[None, :], hp, hp.T,
      d_full, norm_w.astype(F32)[None, :], expand)


def _state_to_kernel_layout(s):
    b = s.shape[0]
    s = s.reshape(b, SSD_GROUPS, HEADS_PER_GROUP, SSD_HEAD_DIM, D_STATE)
    return s.transpose(0, 1, 4, 2, 3).reshape(b, SSD_GROUPS, D_STATE, HEADS_PER_GROUP * SSD_HEAD_DIM)


def _state_from_kernel_layout(s):
    b = s.shape[0]
    s = s.reshape(b, SSD_GROUPS, D_STATE, HEADS_PER_GROUP, SSD_HEAD_DIM)
    return s.transpose(0, 1, 3, 4, 2).reshape(b, SSD_HEADS, SSD_HEAD_DIM, D_STATE)


_T5_LOG_STARTS = (12, 16, 23, 32, 46, 64, 91)
_T5_FAR_BUCKET = REL_BUCKETS // 2 - 1


def _bias_kernel(rb_ref, out_ref, *, tk):
    half = REL_BUCKETS // 2
    max_exact = half // 2
    ri = lax.broadcasted_iota(I32, (tk, tk), 0)
    ci = lax.broadcasted_iota(I32, (tk, tk), 1)
    for t in range(2):
        rel = ci - ri - (1 - t) * tk
        n = jnp.abs(rel)
        large = jnp.full((tk, tk), max_exact, I32)
        for start in _T5_LOG_STARTS:
            large = large + (n >= start).astype(I32)
        bucket = jnp.where(rel > 0, half, 0) + jnp.where(n < max_exact, n, large)
        for h in range(N_HEADS):
            acc = jnp.zeros((tk, tk), F32)
            for bkt in range(REL_BUCKETS):
                acc = jnp.where(bucket == bkt, rb_ref[bkt, h], acc)
            out_ref[t, h] = acc - rb_ref[_T5_FAR_BUCKET, h]


def _bias_tiles(rel_bias, tk):
    return pl.pallas_call(
        functools.partial(_bias_kernel, tk=tk),
        in_specs=[pl.BlockSpec(memory_space=pltpu.SMEM)],
        out_specs=pl.BlockSpec(memory_space=pltpu.VMEM),
        out_shape=jax.ShapeDtypeStruct((2, N_HEADS, tk, tk), F32),
        name="bias_tiles",
    )(rel_bias.astype(F32))


_MAX_BISECTIONS = 40
_SCORE_COLS = 256


def _select_kernel(qi_ref, sm_ref, kit_ref, mask_ref, key_ref, w_ref, *, tq, wk, s_len, q_start, n_sel):
    qb = pl.program_id(1)
    n_blocks = mask_ref.shape[2] // wk
    nt = wk // LANES
    p0 = q_start + qb * tq
    pos = p0 + lax.broadcasted_iota(I32, (tq, 1), 0)
    vis_end = (pos // CHUNK + 1) * CHUNK
    vis_max = jnp.minimum(((p0 + tq - 1) // CHUNK + 1) * CHUNK, s_len)
    nkb = (vis_max + wk - 1) // wk

    wi = sm_ref[0][:, SM_WI:SM_WI + IDX_HEADS] * IDX_HEADS ** -0.5
    for hh in range(IDX_HEADS):
        w_ref[hh] = jnp.broadcast_to(wi[:, hh:hh + 1], (tq, LANES))
    qi = qi_ref[0]
    lane_k = lax.broadcasted_iota(I32, (1, LANES), 1)

    def score_block(j, top, masked):
        m1, m2 = top
        k0 = pl.multiple_of(j * wk, wk)
        for c in range(wk // _SCORE_COLS):
            c0 = k0 + c * _SCORE_COLS
            kt = kit_ref[0, :, pl.ds(c0, _SCORE_COLS)]
            scs = [jnp.zeros((tq, LANES), F32) for _ in range(_SCORE_COLS // LANES)]
            for hh in range(IDX_HEADS):
                d = jnp.dot(qi[:, hh * IDX_DIM:(hh + 1) * IDX_DIM], kt, preferred_element_type=F32)
                d = jnp.maximum(d, 0.0)
                w = w_ref[hh]
                scs = [s + w * d[:, u * LANES:(u + 1) * LANES] for u, s in enumerate(scs)]
            for u, sc in enumerate(scs):
                if masked:
                    key_pos = c0 + u * LANES + lane_k
                    sc = jnp.where((key_pos < vis_end) & (key_pos < s_len), sc, -jnp.inf)
                bits = lax.bitcast_convert_type(sc, I32)
                sl = pl.ds(c0 + u * LANES, LANES)
                key = jnp.where(bits < 0, INT_MIN - bits, bits)
                key_ref[:, sl] = key
                m2 = jnp.maximum(m2, jnp.minimum(m1, key))
                m1 = jnp.maximum(m1, key)
        return m1, m2

    n_open = jnp.minimum(p0, s_len) // wk
    top = (jnp.full((tq, LANES), KEY_NEG_INF, I32),) * 2
    top = lax.fori_loop(0, n_open, functools.partial(score_block, masked=False), top)
    m1, m2 = lax.fori_loop(n_open, nkb, functools.partial(score_block, masked=True), top)

    def count_ge(thr):
        thr_b = jnp.broadcast_to(thr, (tq, LANES))

        def body(j, acc):
            k0 = pl.multiple_of(j * wk, wk)
            for u in range(nt):
                acc = acc + (key_ref[:, pl.ds(k0 + u * LANES, LANES)] >= thr_b).astype(I32)
            return acc

        acc = lax.fori_loop(0, nkb, body, jnp.zeros((tq, LANES), I32))
        return jnp.sum(acc, axis=1, keepdims=True)

    def unfinished(lo, hi, cnt_lo):
        return (cnt_lo > n_sel) & (hi - 1 > lo)

    def bisect(count_fn, lo, hi, cnt_lo, cnt_hi):
        def cond(st):
            return (st[0] < _MAX_BISECTIONS) & (st[1] > 0)

        def body(st):
            it, _, lo, hi, cnt_lo, cnt_hi = st
            live = unfinished(lo, hi, cnt_lo)
            mid = (lo >> 1) + (hi >> 1) + (lo & hi & 1)
            cnt = count_fn(jnp.where(live, mid, lo))
            up = live & (cnt >= n_sel)
            dn = live & (cnt < n_sel)
            lo = jnp.where(up, mid, lo)
            cnt_lo = jnp.where(up, cnt, cnt_lo)
            hi = jnp.where(dn, mid, hi)
            cnt_hi = jnp.where(dn, cnt, cnt_hi)
            return (it + 1, jnp.max(unfinished(lo, hi, cnt_lo).astype(I32)), lo, hi, cnt_lo, cnt_hi)

        st0 = (jnp.int32(0), jnp.max(unfinished(lo, hi, cnt_lo).astype(I32)), lo, hi, cnt_lo, cnt_hi)
        return lax.while_loop(cond, body, st0)[2:]

    lo = jnp.maximum(jnp.min(m2, axis=1, keepdims=True), KEY_NEG_INF + 1)
    hi = jnp.max(m1, axis=1, keepdims=True) + 1
    tau, _, cnt_tau, cnt_above = bisect(count_ge, lo, hi, count_ge(lo), jnp.zeros((tq, 1), I32))

    tie_rows = cnt_tau > n_sel
    has_ties = jnp.max(tie_rows.astype(I32)) > 0

    @pl.when(jnp.logical_not(has_ties))
    def _():
        tau_b = jnp.broadcast_to(tau, (tq, LANES))

        def body(j, carry):
            k0 = pl.multiple_of(j * wk, wk)
            for u in range(nt):
                sl = pl.ds(k0 + u * LANES, LANES)
                mask_ref[0, :, sl] = (key_ref[:, sl] >= tau_b).astype(I32).astype(mask_ref.dtype)
            return carry
        lax.fori_loop(0, nkb, body, 0)

    @pl.when(has_ties)
    def _():
        upper = (lax.broadcasted_iota(I32, (wk, wk), 0) <= lax.broadcasted_iota(I32, (wk, wk), 1)).astype(MXU_DTYPE)
        need = jnp.where(tie_rows, n_sel - cnt_above, wk * n_blocks).astype(F32)

        def body(j, seen):
            k0 = pl.multiple_of(j * wk, wk)
            blk = key_ref[:, pl.ds(k0, wk)]
            eq = blk == tau
            rank = seen + jnp.dot(eq.astype(F32).astype(MXU_DTYPE), upper, preferred_element_type=F32)
            sel = (blk > tau) | (eq & (rank <= need))
            mask_ref[0, :, pl.ds(k0, wk)] = sel.astype(I32).astype(mask_ref.dtype)
            return seen + jnp.sum(eq.astype(F32), axis=1, keepdims=True)
        lax.fori_loop(0, nkb, body, jnp.zeros((tq, 1), F32))

    def zero_body(j, carry):
        k0 = pl.multiple_of(j * wk, wk)
        mask_ref[0, :, pl.ds(k0, wk)] = jnp.zeros((tq, wk), mask_ref.dtype)
        return carry

    lax.fori_loop(nkb, n_blocks, zero_body, 0)


def _select(qi, sm, kit, tq, wk, s_len, q_start, n_sel):
    b, t, _ = qi.shape
    s_pad = kit.shape[2]
    assert n_sel <= 2 * LANES, "the starting bracket takes two candidates per lane column"
    kern = functools.partial(_select_kernel, tq=tq, wk=wk, s_len=s_len, q_start=q_start, n_sel=n_sel)
    return pl.pallas_call(
        kern,
        grid=(b, t // tq),
        in_specs=[pl.BlockSpec((1, tq, IDX_HEADS * IDX_DIM), lambda bi, qi_: (bi, qi_, 0)),
                  pl.BlockSpec((1, tq, LANES), lambda bi, qi_: (bi, qi_, 0)),
                  _per_batch((1, IDX_DIM, s_pad), lambda bi, qi_: (bi, 0, 0), t // tq)],
        out_specs=pl.BlockSpec((1, tq, s_pad), lambda bi, qi_: (bi, qi_, 0)),
        out_shape=jax.ShapeDtypeStruct((b, t, s_pad), jnp.int8),
        scratch_shapes=[pltpu.VMEM((tq, s_pad), I32),
                        pltpu.VMEM((IDX_HEADS, tq, LANES), F32)],
        compiler_params=_cparams(2),
        name="select",
    )(qi, sm, kit)


_V_AUG = 2 * HEAD_DIM


def _attn_kernel(q_ref, kt_ref, v_ref, mask_ref, bias_ref, o_ref, m_ref, acc_ref, qg_ref,
                 *, tq, tk, wide, q_start):
    qb = pl.program_id(1)
    jd = (q_start + qb * tq) // tk

    for g in range(N_KV_HEADS):
        for r in range(Q_PER_KV):
            c0 = (g * Q_PER_KV + r) * HEAD_DIM
            qg_ref[g, r * tq:(r + 1) * tq, :] = q_ref[0, :, c0:c0 + HEAD_DIM]
    m_ref[...] = jnp.full(m_ref.shape, NEG_BIG, F32)
    acc_ref[...] = jnp.zeros(acc_ref.shape, F32)

    def process(k0, width, near):
        nt = width // LANES
        mb = jnp.where(mask_ref[0, :, pl.ds(k0, width)].astype(I32) != 0, 0.0, NEG_BIG)
        mb = jnp.concatenate([mb] * Q_PER_KV, axis=0)
        for g in range(N_KV_HEADS):
            kt = kt_ref[0, g * HEAD_DIM:(g + 1) * HEAD_DIM, pl.ds(k0, width)]
            s = jnp.dot(qg_ref[g], kt, preferred_element_type=F32) + mb
            if near is not None:
                s = s + bias_ref[near, g]
            tiles = [s[:, u * LANES:(u + 1) * LANES] for u in range(nt)]
            mx = tiles[0]
            for u in range(1, nt):
                mx = jnp.maximum(mx, tiles[u])
            m_old = m_ref[g]
            m_new = jnp.maximum(m_old, jnp.max(mx, axis=1, keepdims=True))
            alpha = jnp.exp(m_old - m_new)
            p = jnp.concatenate([jnp.exp(tl - m_new).astype(MXU_DTYPE) for tl in tiles], axis=1)
            vt = v_ref[0, pl.ds(k0, width), g * _V_AUG:(g + 1) * _V_AUG]
            acc_ref[g] = alpha * acc_ref[g] + jnp.dot(p, vt, preferred_element_type=F32)
            m_ref[g] = m_new

    n_far = jnp.maximum(jd - 1, 0)
    per_wide = wide // tk
    n_wide = n_far // per_wide

    def wide_body(j, carry):
        process(pl.multiple_of(j * wide, wide), wide, None)
        return carry

    def far_body(j, carry):
        process(pl.multiple_of(j * tk, tk), tk, None)
        return carry

    lax.fori_loop(0, n_wide, wide_body, 0)
    lax.fori_loop(n_wide * per_wide, n_far, far_body, 0)

    @pl.when(jd >= 1)
    def _():
        process(pl.multiple_of((jd - 1) * tk, tk), tk, 0)

    process(pl.multiple_of(jd * tk, tk), tk, 1)

    for g in range(N_KV_HEADS):
        acc = acc_ref[g]
        o = acc[:, :HEAD_DIM] / acc[:, HEAD_DIM:HEAD_DIM + 1]
        for r in range(Q_PER_KV):
            c0 = (g * Q_PER_KV + r) * HEAD_DIM
            o_ref[0, :, c0:c0 + HEAD_DIM] = o[r * tq:(r + 1) * tq].astype(o_ref.dtype)


def _attention(q, kt, v_aug, mask, bias, tq, tk, wide, q_start):
    b, t, _ = q.shape
    s_pad = kt.shape[2]
    rows = Q_PER_KV * tq
    kern = functools.partial(_attn_kernel, tq=tq, tk=tk, wide=wide, q_start=q_start)
    return pl.pallas_call(
        kern,
        grid=(b, t // tq),
        in_specs=[pl.BlockSpec((1, tq, N_HEADS * HEAD_DIM), lambda bi, qi_: (bi, qi_, 0)),
                  _per_batch((1, KV_DIM, s_pad), lambda bi, qi_: (bi, 0, 0), t // tq),
                  _per_batch((1, s_pad, N_KV_HEADS * _V_AUG), lambda bi, qi_: (bi, 0, 0), t // tq),
                  pl.BlockSpec((1, tq, s_pad), lambda bi, qi_: (bi, qi_, 0)),
                  _resident((2, N_KV_HEADS, rows, tk), lambda bi, qi_: (0, 0, 0, 0))],
        out_specs=pl.BlockSpec((1, tq, N_HEADS * HEAD_DIM), lambda bi, qi_: (bi, qi_, 0)),
        out_shape=jax.ShapeDtypeStruct((b, t, N_HEADS * HEAD_DIM), MXU_DTYPE),
        scratch_shapes=[pltpu.VMEM((N_KV_HEADS, rows, LANES), F32),
                        pltpu.VMEM((N_KV_HEADS, rows, _V_AUG), F32),
                        pltpu.VMEM((N_KV_HEADS, rows, HEAD_DIM), MXU_DTYPE)],
        compiler_params=_cparams(2),
        name="attention",
    )(q, kt, v_aug, mask, bias)


def _layer_norm(x, g, b):
    mu = jnp.mean(x, axis=-1, keepdims=True)
    xc = x - mu
    var = jnp.mean(xc * xc, axis=-1, keepdims=True)
    return xc * lax.rsqrt(var + LN_EPS) * g + b


def _post_kernel(o_ref, yn_ref, gs_ref, ga_ref, x_ref, wso_ref, wao_ref, wout_ref, wup_ref, wdown_ref,
                 ln_ref, out_ref):
    y1 = jnp.dot(yn_ref[...], wso_ref[...], preferred_element_type=F32)
    y2 = jnp.dot(o_ref[...], wao_ref[...], preferred_element_type=F32)
    mix = _sigmoid(gs_ref[...]) * y1 + _sigmoid(ga_ref[...]) * y2
    mixed = _dot(mix, wout_ref[...])
    h = _layer_norm(ALPHA * x_ref[...] + mixed, ln_ref[0:1, :], ln_ref[1:2, :])
    u = jnp.maximum(_dot(h, wup_ref[...]), 0.0)
    f = _dot(u * u, wdown_ref[...])
    out_ref[...] = _layer_norm(ALPHA * h + f, ln_ref[2:3, :], ln_ref[3:4, :])


def _post(o, yn, gs, ga, x2d, w_ssd_o, w_attn_o, w_out, w_up, w_down, ln, tm):
    m = x2d.shape[0]
    row = lambda w: pl.BlockSpec((tm, w), lambda i: (i, 0))
    const = lambda i: (0, 0)
    return pl.pallas_call(
        _post_kernel,
        grid=(m // tm,),
        in_specs=[row(N_HEADS * HEAD_DIM), row(SSD_INNER), row(D_MODEL), row(D_MODEL), row(D_MODEL),
                  _resident((SSD_INNER, D_MODEL), const),
                  _resident((N_HEADS * HEAD_DIM, D_MODEL), const),
                  _resident((D_MODEL, D_MODEL), const),
                  _resident((D_MODEL, D_FF), const),
                  _resident((D_FF, D_MODEL), const),
                  _resident((4, D_MODEL), const)],
        out_specs=row(D_MODEL),
        out_shape=jax.ShapeDtypeStruct((m, D_MODEL), F32),
        compiler_params=_cparams(1),
        name="post",
    )(o, yn, gs, ga, x2d, w_ssd_o, w_attn_o, w_out, w_up, w_down, ln)


def _round_up(n, m):
    return (n + m - 1) // m * m


def _layer(x, cache_k, cache_v, cache_kidx, state_ssm, state_conv, bias_tiles, w_prep, conv_w, conv_b,
           dt_bias, a_log, d_skip, ssd_norm_w, w_ssd_o, w_attn_o, w_out, w_up, w_down, ln,
           *, tq, tk, wide, tqs, wk, lc, tm):
    b, t, _ = x.shape
    past = cache_k.shape[1]
    s_len = past + t
    m = b * t
    assert m % tm == 0 and t % tq == 0 and past % tk == 0 and tk % LANES == 0 and wide % tk == 0
    assert t % tqs == 0 and wk % LANES == 0
    assert tq == tk or t == tq, "every query tile must start on a key-tile boundary"
    assert t % lc == 0 or t < lc
    x2d = x.reshape(m, D_MODEL)

    z, xbc, q, k, v, qi, gs, ga, sm = _inproj(x2d, w_prep, tm)

    t_pad = _round_up(t, lc)
    pad_t = lambda a: jnp.pad(a.reshape(b, t, a.shape[-1]), ((0, 0), (0, t_pad - t), (0, 0)))
    sm3 = pad_t(sm)
    dt_t = sm3[:, :, SM_DT:SM_DT + SSD_HEADS].transpose(0, 2, 1)
    yn, ssm_new = _ssd(pad_t(xbc), pad_t(z), sm3, dt_t, state_conv.astype(F32),
                       _state_to_kernel_layout(state_ssm.astype(F32)), conv_w, conv_b, dt_bias, a_log, d_skip,
                       ssd_norm_w, lc, min(t, lc))
    yn = yn[:, :t].reshape(m, SSD_INNER)
    new_ssm = _state_from_kernel_layout(ssm_new)
    xbc3 = xbc.reshape(b, t, CONV_DIM)
    new_conv = jnp.concatenate([state_conv.astype(F32), xbc3], axis=1)[:, -_HALO:]

    k3 = k.reshape(b, t, KV_DIM)
    v3 = v.reshape(b, t, KV_DIM)
    ki3 = sm.reshape(b, t, LANES)[:, :, SM_KI:SM_KI + IDX_DIM]
    s_pad = _round_up(s_len, max(wide, wk))
    assert s_pad % wide == 0 and s_pad % wk == 0
    pad_s = lambda a: jnp.pad(a, ((0, 0), (0, s_pad - s_len)) + ((0, 0),) * (a.ndim - 2))
    k_all = pad_s(jnp.concatenate([cache_k.reshape(b, past, KV_DIM).astype(MXU_DTYPE), k3.astype(MXU_DTYPE)], axis=1))
    v_all = jnp.concatenate([cache_v.astype(MXU_DTYPE), v3.reshape(b, t, N_KV_HEADS, HEAD_DIM).astype(MXU_DTYPE)], axis=1)
    v_aug = pad_s(jnp.concatenate([v_all, jnp.ones((b, s_len, N_KV_HEADS, 1), MXU_DTYPE),
                                   jnp.zeros((b, s_len, N_KV_HEADS, _V_AUG - HEAD_DIM - 1), MXU_DTYPE)], axis=3))
    ki_all = pad_s(jnp.concatenate([cache_kidx.astype(MXU_DTYPE), ki3.astype(MXU_DTYPE)], axis=1))
    n_sel = min(TOPK_MAX, s_len // 4)
    mask = _select(qi.reshape(b, t, -1), sm.reshape(b, t, LANES), ki_all.transpose(0, 2, 1),
                   tqs, wk, s_len, past, n_sel)
    bias = bias_tiles[:, :, :tq, :].reshape(2, N_KV_HEADS, Q_PER_KV * tq, tk)
    o = _attention(q.reshape(b, t, -1), k_all.transpose(0, 2, 1), v_aug.reshape(b, s_pad, N_KV_HEADS * _V_AUG),
                   mask, bias, tq, tk, wide, past)

    out = _post(o.reshape(m, -1), yn, gs, ga, x2d, w_ssd_o, w_attn_o, w_out, w_up, w_down, ln, tm)
    return out.reshape(b, t, D_MODEL), (k3.reshape(b, t, N_KV_HEADS, HEAD_DIM), v3.reshape(b, t, N_KV_HEADS, HEAD_DIM),
                                        ki3, new_ssm, new_conv)


ATTN_TILE = 128
WIDE_TILE = 1024
WIDE_TILE_SHORT = 512
SELECT_ROWS = 128
SELECT_WIDTH = 512
SSD_CHUNK = 128
ROW_TILE = 256


def kernel(x_prompt, x_sample, cache_k, cache_v, cache_kidx, state_ssm, state_conv, rel_bias, w_in, conv_w, conv_b,
           dt_bias, a_log, d_skip, ssd_norm_w, w_ssd_o, w_attn_o, w_out, ln1_g, ln1_b, w_up, w_down, ln2_g, ln2_b):
    bp = x_prompt.shape[0]
    dtype = x_prompt.dtype
    bias_tiles = _bias_tiles(rel_bias, ATTN_TILE)
    yp, ys = x_prompt, x_sample
    st_p, st_s = [], []
    tiles = dict(tk=ATTN_TILE, wk=SELECT_WIDTH, lc=SSD_CHUNK, tm=ROW_TILE)
    for l in range(DEPTH):
        w_prep = _prep_w_in(w_in[l])
        ln = jnp.stack([ln1_g[l], ln1_b[l], ln2_g[l], ln2_b[l]]).astype(F32)
        weights = (bias_tiles, w_prep, conv_w[l], conv_b[l], dt_bias[l], a_log[l], d_skip[l], ssd_norm_w[l],
                   w_ssd_o[l].astype(MXU_DTYPE), w_attn_o[l].astype(MXU_DTYPE), w_out[l].astype(MXU_DTYPE),
                   w_up[l].astype(MXU_DTYPE), w_down[l].astype(MXU_DTYPE), ln)
        yp, sp = _layer(yp, jnp.zeros((bp, 0, N_KV_HEADS, HEAD_DIM), dtype), jnp.zeros((bp, 0, N_KV_HEADS, HEAD_DIM), dtype),
                        jnp.zeros((bp, 0, IDX_DIM), dtype), jnp.zeros((bp, SSD_HEADS, SSD_HEAD_DIM, D_STATE), dtype),
                        jnp.zeros((bp, _HALO, CONV_DIM), dtype), *weights,
                        tq=min(ATTN_TILE, yp.shape[1]), tqs=min(SELECT_ROWS, yp.shape[1]), wide=WIDE_TILE, **tiles)
        ys, ss = _layer(ys, cache_k[l], cache_v[l], cache_kidx[l], state_ssm[l], state_conv[l], *weights,
                        tq=min(ATTN_TILE, ys.shape[1]), tqs=min(SELECT_ROWS, ys.shape[1]), wide=WIDE_TILE_SHORT, **tiles)
        st_p.append(sp)
        st_s.append(ss)
    stack = lambda sts, i: jnp.stack([s[i] for s in sts])
    return (yp, ys) + tuple(stack(st_p, i) for i in range(5)) + tuple(stack(st_s, i) for i in range(5))
```

```python
import functools

import jax
import jax.numpy as jnp
from jax import lax
from jax.experimental import pallas as pl
from jax.experimental.pallas import tpu as pltpu

F32 = jnp.float32
I32 = jnp.int32
MXU_DTYPE = jnp.bfloat16

D_MODEL = 1024
CHUNK = 64
SSD_INNER = 2048
SSD_HEAD_DIM = 64
SSD_HEADS = 32
SSD_GROUPS = 4
HEADS_PER_GROUP = SSD_HEADS // SSD_GROUPS
D_STATE = 128
CONV_W = 4
CONV_DIM = SSD_INNER + 2 * SSD_GROUPS * D_STATE
N_HEADS = 16
N_KV_HEADS = 4
Q_PER_KV = N_HEADS // N_KV_HEADS
HEAD_DIM = 64
KV_DIM = N_KV_HEADS * HEAD_DIM
IDX_HEADS = 8
IDX_DIM = 64
TOPK_MAX = 256
REL_BUCKETS = 32
REL_MAX_DIST = 128
D_FF = 4 * D_MODEL
DEPTH = 1
ALPHA = (2.0 * DEPTH) ** 0.25
LN_EPS = 1e-5
RMS_EPS = 1e-5
IN_SIZES = (SSD_INNER, CONV_DIM, SSD_HEADS, N_HEADS * HEAD_DIM, KV_DIM, KV_DIM,
            IDX_HEADS * IDX_DIM, IDX_DIM, IDX_HEADS, D_MODEL, D_MODEL)

LANES = 128
VMEM_LIMIT_BYTES = 56 * 1024 * 1024

SM_KI = 0
SM_DT = IDX_DIM
SM_WI = IDX_DIM + SSD_HEADS
SM_PAD = LANES - (IDX_DIM + SSD_HEADS + IDX_HEADS)

NEG_BIG = -1e30
INT_MIN = -2 ** 31
INT_MAX = 2 ** 31 - 1
KEY_NEG_INF = -0x7F800000


def _cparams(n_axes):
    return pltpu.CompilerParams(dimension_semantics=("arbitrary",) * n_axes,
                                vmem_limit_bytes=VMEM_LIMIT_BYTES)


def _resident(shape, index_map):
    return pl.BlockSpec(shape, index_map, pipeline_mode=pl.Buffered(1))


def _per_batch(shape, index_map, steps_per_batch):
    if steps_per_batch > 1:
        return _resident(shape, index_map)
    return pl.BlockSpec(shape, index_map)


def _dot(a, b):
    return jnp.dot(a.astype(MXU_DTYPE), b.astype(MXU_DTYPE), preferred_element_type=F32)


def _dot_exact(a, b):
    return jnp.dot(a, b, preferred_element_type=F32, precision=lax.Precision.HIGHEST)


def _sigmoid(x):
    return 1.0 / (1.0 + jnp.exp(-x))


def _softplus(x):
    return jnp.maximum(x, 0.0) + jnp.log1p(jnp.exp(-jnp.abs(x)))


_PROJ_WIDTHS = (SSD_INNER, CONV_DIM, N_HEADS * HEAD_DIM, KV_DIM, KV_DIM, IDX_HEADS * IDX_DIM,
                D_MODEL, D_MODEL, LANES)
_PROJ_WIDTH = sum(_PROJ_WIDTHS)
_PROJ_CHUNK = 512


def _prep_w_in(w_in):
    offs = [0]
    for s in IN_SIZES:
        offs.append(offs[-1] + s)
    z, xbc, dt, q, k, v, qi, ki, wi, gs, ga = [w_in[:, offs[i]:offs[i + 1]] for i in range(len(IN_SIZES))]
    small = jnp.concatenate([ki, dt, wi, jnp.zeros((D_MODEL, SM_PAD), w_in.dtype)], axis=1)
    w = jnp.concatenate([z, xbc, q * HEAD_DIM ** -0.5, k, v, qi * IDX_DIM ** -0.5, gs, ga, small], axis=1)
    return w.astype(MXU_DTYPE)


def _inproj_kernel(x_ref, w_ref, z_ref, xbc_ref, q_ref, k_ref, v_ref, qi_ref, gs_ref, ga_ref, sm_ref):
    xb = x_ref[...].astype(MXU_DTYPE)
    outs = (z_ref, xbc_ref, q_ref, k_ref, v_ref, qi_ref, gs_ref, ga_ref, sm_ref)
    c0 = 0
    for ref, width in zip(outs, _PROJ_WIDTHS):
        for o in range(0, width, _PROJ_CHUNK):
            n = min(_PROJ_CHUNK, width - o)
            r = jnp.dot(xb, w_ref[:, c0 + o:c0 + o + n], preferred_element_type=F32)
            ref[:, o:o + n] = r.astype(ref.dtype)
        c0 += width


def _inproj(x2d, w_prep, tm):
    m = x2d.shape[0]
    dtypes = (F32, F32, MXU_DTYPE, F32, F32, MXU_DTYPE, F32, F32, F32)
    out_shape = tuple(jax.ShapeDtypeStruct((m, w), d) for w, d in zip(_PROJ_WIDTHS, dtypes))
    out_specs = tuple(pl.BlockSpec((tm, w), lambda i: (i, 0)) for w in _PROJ_WIDTHS)
    return pl.pallas_call(
        _inproj_kernel,
        grid=(m // tm,),
        in_specs=[pl.BlockSpec((tm, D_MODEL), lambda i: (i, 0)),
                  _resident((D_MODEL, _PROJ_WIDTH), lambda i: (0, 0))],
        out_specs=out_specs,
        out_shape=out_shape,
        compiler_params=_cparams(1),
        name="inproj",
    )(x2d, w_prep)


_HALO = CONV_W - 1
_XP_BASE = 8


def _ssd_kernel(xbc_ref, z_ref, sm_ref, dtT_ref, conv0_ref, ssm0_ref, convw_ref, convb_ref,
                hp_ref, hpT_ref, dfull_ref, nw_ref, expand_ref,
                yn_ref, ssm_out_ref, xp_ref, st_ref, y_ref, *, lc, t_valid):
    c = pl.program_id(1)
    nc = pl.num_programs(1)

    @pl.when(c == 0)
    def _():
        xp_ref[_XP_BASE - _HALO:_XP_BASE, :] = conv0_ref[0]
        st_ref[...] = ssm0_ref[0]

    xp_ref[_XP_BASE:_XP_BASE + lc, :] = xbc_ref[0]
    conv = convb_ref[...]
    for i in range(CONV_W):
        conv = conv + xp_ref[_XP_BASE - _HALO + i:_XP_BASE - _HALO + i + lc, :] * convw_ref[i:i + 1, :]
    xp_ref[_XP_BASE - _HALO:_XP_BASE, :] = xp_ref[_XP_BASE + lc - _HALO:_XP_BASE + lc, :]
    act = conv * _sigmoid(conv)
    xs = act[:, :SSD_INNER]
    bm = act[:, SSD_INNER:SSD_INNER + SSD_GROUPS * D_STATE]
    cm = act[:, SSD_INNER + SSD_GROUPS * D_STATE:]

    dt = _softplus(sm_ref[0][:, SM_DT:SM_DT + SSD_HEADS] + hp_ref[0:1, :])
    dt_t = _softplus(dtT_ref[0] + hpT_ref[:, 0:1])
    if t_valid < lc:
        dt = jnp.where(lax.broadcasted_iota(I32, (lc, SSD_HEADS), 0) < t_valid, dt, 0.0)
        dt_t = jnp.where(lax.broadcasted_iota(I32, (SSD_HEADS, lc), 1) < t_valid, dt_t, 0.0)
    a = dt * (-jnp.exp(hp_ref[1:2, :]))
    a_t = dt_t * (-jnp.exp(hpT_ref[:, 1:2]))
    ri = lax.broadcasted_iota(I32, (lc, lc), 0)
    ci = lax.broadcasted_iota(I32, (lc, lc), 1)
    causal = ci <= ri
    acum = _dot_exact(causal.astype(F32), a)
    acum_t = _dot_exact(a_t, (ri <= ci).astype(F32))
    e_a = jnp.exp(acum)
    w_end = jnp.exp(acum[lc - 1:lc, :] - acum) * dt
    e_a_full = _dot_exact(e_a, expand_ref[...])
    w_end_full = _dot_exact(w_end, expand_ref[...])
    xw = xs * w_end_full
    lane = lax.broadcasted_iota(I32, (lc, 2 * SSD_HEAD_DIM), 1)
    gw = HEADS_PER_GROUP * SSD_HEAD_DIM

    for g in range(SSD_GROUPS):
        b_g = bm[:, g * D_STATE:(g + 1) * D_STATE]
        c_g = cm[:, g * D_STATE:(g + 1) * D_STATE].astype(MXU_DTYPE)
        b_gt = b_g.T.astype(MXU_DTYPE)
        cb = jnp.dot(c_g, b_gt, preferred_element_type=F32)
        state = st_ref[g]
        y_off = _dot(c_g, state) * e_a_full[:, g * gw:(g + 1) * gw]
        st_ref[g] = state * e_a_full[lc - 1:lc, g * gw:(g + 1) * gw] + _dot(b_gt, xw[:, g * gw:(g + 1) * gw])
        for jp in range(HEADS_PER_GROUP // 2):
            h0 = g * HEADS_PER_GROUP + 2 * jp
            ms = []
            for h in (h0, h0 + 1):
                seg = acum[:, h:h + 1] - acum_t[h:h + 1, :]
                decay = jnp.exp(jnp.where(causal, seg, -jnp.inf))
                ms.append((cb * decay * dt_t[h:h + 1, :]).astype(MXU_DTYPE))
            c0 = h0 * SSD_HEAD_DIM
            x_pair = xs[:, c0:c0 + 2 * SSD_HEAD_DIM]
            r = jnp.dot(jnp.concatenate(ms, axis=0), x_pair.astype(MXU_DTYPE), preferred_element_type=F32)
            y_diag = jnp.where(lane < SSD_HEAD_DIM, r[:lc], r[lc:])
            o0 = 2 * jp * SSD_HEAD_DIM
            y_ref[:, c0:c0 + 2 * SSD_HEAD_DIM] = (y_diag + y_off[:, o0:o0 + 2 * SSD_HEAD_DIM]
                                                  + dfull_ref[:, c0:c0 + 2 * SSD_HEAD_DIM] * x_pair)

    zv = z_ref[0]
    hg = y_ref[...] * (zv * _sigmoid(zv))
    for g in range(SSD_GROUPS):
        sl = hg[:, g * gw:(g + 1) * gw]
        ms = jnp.mean(sl * sl, axis=-1, keepdims=True)
        yn_ref[0, :, g * gw:(g + 1) * gw] = (sl * lax.rsqrt(ms + RMS_EPS) * nw_ref[:, g * gw:(g + 1) * gw]).astype(yn_ref.dtype)

    @pl.when(c == nc - 1)
    def _():
        ssm_out_ref[0] = st_ref[...]


def _ssd(xbc, z, sm, dt_t, conv0, ssm0_t, conv_w, conv_b, dt_bias, a_log, d_skip, norm_w, lc, t_valid):
    b, t, _ = xbc.shape
    nc = t // lc
    hp = jnp.stack([dt_bias, a_log]).astype(F32)
    d_full = jnp.repeat(d_skip.astype(F32), SSD_HEAD_DIM)[None, :]
    expand = jnp.repeat(jnp.eye(SSD_HEADS, dtype=F32), SSD_HEAD_DIM, axis=1)
    gw = HEADS_PER_GROUP * SSD_HEAD_DIM
    const2 = lambda bi, ci: (0, 0)
    kern = functools.partial(_ssd_kernel, lc=lc, t_valid=t_valid)
    return pl.pallas_call(
        kern,
        grid=(b, nc),
        in_specs=[pl.BlockSpec((1, lc, CONV_DIM), lambda bi, ci: (bi, ci, 0)),
                  pl.BlockSpec((1, lc, SSD_INNER), lambda bi, ci: (bi, ci, 0)),
                  pl.BlockSpec((1, lc, LANES), lambda bi, ci: (bi, ci, 0)),
                  pl.BlockSpec((1, SSD_HEADS, lc), lambda bi, ci: (bi, 0, ci)),
                  pl.BlockSpec((1, _HALO, CONV_DIM), lambda bi, ci: (bi, 0, 0)),
                  pl.BlockSpec((1, SSD_GROUPS, D_STATE, gw), lambda bi, ci: (bi, 0, 0, 0)),
                  pl.BlockSpec((CONV_W, CONV_DIM), const2),
                  pl.BlockSpec((1, CONV_DIM), const2),
                  pl.BlockSpec((2, SSD_HEADS), const2),
                  pl.BlockSpec((SSD_HEADS, 2), const2),
                  pl.BlockSpec((1, SSD_INNER), const2),
                  pl.BlockSpec((1, SSD_INNER), const2),
                  pl.BlockSpec((SSD_HEADS, SSD_INNER), const2)],
        out_specs=(pl.BlockSpec((1, lc, SSD_INNER), lambda bi, ci: (bi, ci, 0)),
                   pl.BlockSpec((1, SSD_GROUPS, D_STATE, gw), lambda bi, ci: (bi, 0, 0, 0))),
        out_shape=(jax.ShapeDtypeStruct((b, t, SSD_INNER), MXU_DTYPE),
                   jax.ShapeDtypeStruct((b, SSD_GROUPS, D_STATE, gw), F32)),
        scratch_shapes=[pltpu.VMEM((_XP_BASE + lc, CONV_DIM), F32),
                        pltpu.VMEM((SSD_GROUPS, D_STATE, gw), F32),
                        pltpu.VMEM((lc, SSD_INNER), F32)],
        compiler_params=_cparams(2),
        name="ssd",
    )(xbc, z, sm, dt_t, conv0, ssm0_t, conv_w.astype(F32), conv_b.astype(F32)[None, :], hp, hp.T,
      d_full, norm_w.astype(F32)[None, :], expand)


def _state_to_kernel_layout(s):
    b = s.shape[0]
    s = s.reshape(b, SSD_GROUPS, HEADS_PER_GROUP, SSD_HEAD_DIM, D_STATE)
    return s.transpose(0, 1, 4, 2, 3).reshape(b, SSD_GROUPS, D_STATE, HEADS_PER_GROUP * SSD_HEAD_DIM)


def _state_from_kernel_layout(s):
    b = s.shape[0]
    s = s.reshape(b, SSD_GROUPS, D_STATE, HEADS_PER_GROUP, SSD_HEAD_DIM)
    return s.transpose(0, 1, 3, 4, 2).reshape(b, SSD_HEADS, SSD_HEAD_DIM, D_STATE)


_T5_LOG_STARTS = (12, 16, 23, 32, 46, 64, 91)
_T5_FAR_BUCKET = REL_BUCKETS // 2 - 1


def _bias_kernel(rb_ref, out_ref, *, tk):
    half = REL_BUCKETS // 2
    max_exact = half // 2
    ri = lax.broadcasted_iota(I32, (tk, tk), 0)
    ci = lax.broadcasted_iota(I32, (tk, tk), 1)
    for t in range(2):
        rel = ci - ri - (1 - t) * tk
        n = jnp.abs(rel)
        large = jnp.full((tk, tk), max_exact, I32)
        for start in _T5_LOG_STARTS:
            large = large + (n >= start).astype(I32)
        bucket = jnp.where(rel > 0, half, 0) + jnp.where(n < max_exact, n, large)
        for h in range(N_HEADS):
            acc = jnp.zeros((tk, tk), F32)
            for bkt in range(REL_BUCKETS):
                acc = jnp.where(bucket == bkt, rb_ref[bkt, h], acc)
            out_ref[t, h] = acc - rb_ref[_T5_FAR_BUCKET, h]


def _bias_tiles(rel_bias, tk):
    return pl.pallas_call(
        functools.partial(_bias_kernel, tk=tk),
        in_specs=[pl.BlockSpec(memory_space=pltpu.SMEM)],
        out_specs=pl.BlockSpec(memory_space=pltpu.VMEM),
        out_shape=jax.ShapeDtypeStruct((2, N_HEADS, tk, tk), F32),
        name="bias_tiles",
    )(rel_bias.astype(F32))


_MAX_BISECTIONS = 40
_SCORE_COLS = 256


def _select_kernel(qi_ref, sm_ref, kit_ref, mask_ref, key_ref, w_ref, *, tq, wk, s_len, q_start, n_sel):
    qb = pl.program_id(1)
    n_blocks = mask_ref.shape[2] // wk
    nt = wk // LANES
    p0 = q_start + qb * tq
    pos = p0 + lax.broadcasted_iota(I32, (tq, 1), 0)
    vis_end = (pos // CHUNK + 1) * CHUNK
    vis_max = jnp.minimum(((p0 + tq - 1) // CHUNK + 1) * CHUNK, s_len)
    nkb = (vis_max + wk - 1) // wk

    wi = sm_ref[0][:, SM_WI:SM_WI + IDX_HEADS] * IDX_HEADS ** -0.5
    for hh in range(IDX_HEADS):
        w_ref[hh] = jnp.broadcast_to(wi[:, hh:hh + 1], (tq, LANES))
    lane_k = lax.broadcasted_iota(I32, (1, LANES), 1)

    def to_key(score):
        bits = lax.bitcast_convert_type(score, I32)
        return jnp.where(bits < 0, INT_MIN - bits, bits)

    def score_block(j, top, masked):
        m1, m2 = top
        k0 = pl.multiple_of(j * wk, wk)
        for c in range(wk // _SCORE_COLS):
            c0 = k0 + c * _SCORE_COLS
            kt = kit_ref[0, :, pl.ds(c0, _SCORE_COLS)]
            d = jnp.maximum(jnp.dot(qi_ref[0, 0], kt, preferred_element_type=F32), 0.0)
            for u in range(_SCORE_COLS // LANES):
                sc = jnp.zeros((tq, LANES), F32)
                for hh in range(IDX_HEADS):
                    sc = sc + w_ref[hh] * d[hh * tq:(hh + 1) * tq, u * LANES:(u + 1) * LANES]
                if masked:
                    key_pos = c0 + u * LANES + lane_k
                    sc = jnp.where((key_pos < vis_end) & (key_pos < s_len), sc, -jnp.inf)
                key_ref[:, pl.ds(c0 + u * LANES, LANES)] = to_key(sc)
                m2 = jnp.maximum(m2, jnp.minimum(m1, sc))
                m1 = jnp.maximum(m1, sc)
        return m1, m2

    n_open = jnp.minimum(p0, s_len) // wk
    top = (jnp.full((tq, LANES), -jnp.inf, F32),) * 2
    top = lax.fori_loop(0, n_open, functools.partial(score_block, masked=False), top)
    m1, m2 = lax.fori_loop(n_open, nkb, functools.partial(score_block, masked=True), top)

    def count_ge(thr):
        thr_b = jnp.broadcast_to(thr, (tq, LANES))

        def body(j, acc):
            k0 = pl.multiple_of(j * wk, wk)
            for u in range(nt):
                acc = acc + (key_ref[:, pl.ds(k0 + u * LANES, LANES)] >= thr_b).astype(I32)
            return acc

        acc = lax.fori_loop(0, nkb, body, jnp.zeros((tq, LANES), I32))
        return jnp.sum(acc, axis=1, keepdims=True)

    def unfinished(lo, hi, cnt_lo):
        return (cnt_lo > n_sel) & (hi - 1 > lo)

    def bisect(count_fn, lo, hi, cnt_lo, cnt_hi):
        def cond(st):
            return (st[0] < _MAX_BISECTIONS) & (st[1] > 0)

        def body(st):
            it, _, lo, hi, cnt_lo, cnt_hi = st
            live = unfinished(lo, hi, cnt_lo)
            mid = (lo >> 1) + (hi >> 1) + (lo & hi & 1)
            cnt = count_fn(jnp.where(live, mid, lo))
            up = live & (cnt >= n_sel)
            dn = live & (cnt < n_sel)
            lo = jnp.where(up, mid, lo)
            cnt_lo = jnp.where(up, cnt, cnt_lo)
            hi = jnp.where(dn, mid, hi)
            cnt_hi = jnp.where(dn, cnt, cnt_hi)
            return (it + 1, jnp.max(unfinished(lo, hi, cnt_lo).astype(I32)), lo, hi, cnt_lo, cnt_hi)

        st0 = (jnp.int32(0), jnp.max(unfinished(lo, hi, cnt_lo).astype(I32)), lo, hi, cnt_lo, cnt_hi)
        return lax.while_loop(cond, body, st0)[2:]

    lo = jnp.maximum(to_key(jnp.min(m2, axis=1, keepdims=True)), KEY_NEG_INF + 1)
    hi = to_key(jnp.max(m1, axis=1, keepdims=True)) + 1
    tau, _, cnt_tau, cnt_above = bisect(count_ge, lo, hi, count_ge(lo), jnp.zeros((tq, 1), I32))

    tie_rows = cnt_tau > n_sel
    has_ties = jnp.max(tie_rows.astype(I32)) > 0

    @pl.when(jnp.logical_not(has_ties))
    def _():
        tau_b = jnp.broadcast_to(tau, (tq, LANES))

        def body(j, carry):
            k0 = pl.multiple_of(j * wk, wk)
            for u in range(nt):
                sl = pl.ds(k0 + u * LANES, LANES)
                mask_ref[0, :, sl] = (key_ref[:, sl] >= tau_b).astype(I32).astype(mask_ref.dtype)
            return carry
        lax.fori_loop(0, nkb, body, 0)

    @pl.when(has_ties)
    def _():
        upper = (lax.broadcasted_iota(I32, (wk, wk), 0) <= lax.broadcasted_iota(I32, (wk, wk), 1)).astype(MXU_DTYPE)
        need = jnp.where(tie_rows, n_sel - cnt_above, wk * n_blocks).astype(F32)

        def body(j, seen):
            k0 = pl.multiple_of(j * wk, wk)
            blk = key_ref[:, pl.ds(k0, wk)]
            eq = blk == tau
            rank = seen + jnp.dot(eq.astype(F32).astype(MXU_DTYPE), upper, preferred_element_type=F32)
            sel = (blk > tau) | (eq & (rank <= need))
            mask_ref[0, :, pl.ds(k0, wk)] = sel.astype(I32).astype(mask_ref.dtype)
            return seen + jnp.sum(eq.astype(F32), axis=1, keepdims=True)
        lax.fori_loop(0, nkb, body, jnp.zeros((tq, 1), F32))

    def zero_body(j, carry):
        k0 = pl.multiple_of(j * wk, wk)
        mask_ref[0, :, pl.ds(k0, wk)] = jnp.zeros((tq, wk), mask_ref.dtype)
        return carry

    lax.fori_loop(nkb, n_blocks, zero_body, 0)


def _select(qi, sm, kit, tq, wk, s_len, q_start, n_sel):
    b, t, _ = qi.shape
    s_pad = kit.shape[2]
    assert n_sel <= 2 * LANES, "the starting bracket takes two candidates per lane column"
    kern = functools.partial(_select_kernel, tq=tq, wk=wk, s_len=s_len, q_start=q_start, n_sel=n_sel)
    qi = qi.reshape(b, t // tq, tq, IDX_HEADS, IDX_DIM).transpose(0, 1, 3, 2, 4).reshape(b, t // tq, IDX_HEADS * tq, IDX_DIM)
    return pl.pallas_call(
        kern,
        grid=(b, t // tq),
        in_specs=[pl.BlockSpec((1, 1, IDX_HEADS * tq, IDX_DIM), lambda bi, qi_: (bi, qi_, 0, 0)),
                  pl.BlockSpec((1, tq, LANES), lambda bi, qi_: (bi, qi_, 0)),
                  _per_batch((1, IDX_DIM, s_pad), lambda bi, qi_: (bi, 0, 0), t // tq)],
        out_specs=pl.BlockSpec((1, tq, s_pad), lambda bi, qi_: (bi, qi_, 0)),
        out_shape=jax.ShapeDtypeStruct((b, t, s_pad), jnp.int8),
        scratch_shapes=[pltpu.VMEM((tq, s_pad), I32),
                        pltpu.VMEM((IDX_HEADS, tq, LANES), F32)],
        compiler_params=_cparams(2),
        name="select",
    )(qi, sm, kit)


_V_AUG = 2 * HEAD_DIM


def _attn_kernel(q_ref, kt_ref, v_ref, mask_ref, bias_ref, o_ref, m_ref, acc_ref, qg_ref,
                 *, tq, tk, wide, q_start):
    qb = pl.program_id(1)
    jd = (q_start + qb * tq) // tk

    for g in range(N_KV_HEADS):
        for r in range(Q_PER_KV):
            c0 = (g * Q_PER_KV + r) * HEAD_DIM
            qg_ref[g, r * tq:(r + 1) * tq, :] = q_ref[0, :, c0:c0 + HEAD_DIM]
    m_ref[...] = jnp.full(m_ref.shape, NEG_BIG, F32)
    acc_ref[...] = jnp.zeros(acc_ref.shape, F32)

    def process(k0, width, near):
        nt = width // LANES
        mb = jnp.where(mask_ref[0, :, pl.ds(k0, width)].astype(I32) != 0, 0.0, NEG_BIG)
        mb = jnp.concatenate([mb] * Q_PER_KV, axis=0)
        for g in range(N_KV_HEADS):
            kt = kt_ref[0, g * HEAD_DIM:(g + 1) * HEAD_DIM, pl.ds(k0, width)]
            s = jnp.dot(qg_ref[g], kt, preferred_element_type=F32) + mb
            if near is not None:
                s = s + bias_ref[near, g]
            tiles = [s[:, u * LANES:(u + 1) * LANES] for u in range(nt)]
            mx = tiles[0]
            for u in range(1, nt):
                mx = jnp.maximum(mx, tiles[u])
            m_old = m_ref[g]
            m_new = jnp.maximum(m_old, jnp.max(mx, axis=1, keepdims=True))
            alpha = jnp.exp(m_old - m_new)
            p = jnp.concatenate([jnp.exp(tl - m_new).astype(MXU_DTYPE) for tl in tiles], axis=1)
            vt = v_ref[0, pl.ds(k0, width), g * _V_AUG:(g + 1) * _V_AUG]
            acc_ref[g] = alpha * acc_ref[g] + jnp.dot(p, vt, preferred_element_type=F32)
            m_ref[g] = m_new

    n_far = jnp.maximum(jd - 1, 0)
    per_wide = wide // tk
    n_wide = n_far // per_wide

    def wide_body(j, carry):
        process(pl.multiple_of(j * wide, wide), wide, None)
        return carry

    def far_body(j, carry):
        process(pl.multiple_of(j * tk, tk), tk, None)
        return carry

    lax.fori_loop(0, n_wide, wide_body, 0)
    lax.fori_loop(n_wide * per_wide, n_far, far_body, 0)

    @pl.when(jd >= 1)
    def _():
        process(pl.multiple_of((jd - 1) * tk, tk), tk, 0)

    process(pl.multiple_of(jd * tk, tk), tk, 1)

    for g in range(N_KV_HEADS):
        acc = acc_ref[g]
        o = acc[:, :HEAD_DIM] / acc[:, HEAD_DIM:HEAD_DIM + 1]
        for r in range(Q_PER_KV):
            c0 = (g * Q_PER_KV + r) * HEAD_DIM
            o_ref[0, :, c0:c0 + HEAD_DIM] = o[r * tq:(r + 1) * tq].astype(o_ref.dtype)


def _attention(q, kt, v_aug, mask, bias, tq, tk, wide, q_start):
    b, t, _ = q.shape
    s_pad = kt.shape[2]
    rows = Q_PER_KV * tq
    kern = functools.partial(_attn_kernel, tq=tq, tk=tk, wide=wide, q_start=q_start)
    return pl.pallas_call(
        kern,
        grid=(b, t // tq),
        in_specs=[pl.BlockSpec((1, tq, N_HEADS * HEAD_DIM), lambda bi, qi_: (bi, qi_, 0)),
                  _per_batch((1, KV_DIM, s_pad), lambda bi, qi_: (bi, 0, 0), t // tq),
                  _per_batch((1, s_pad, N_KV_HEADS * _V_AUG), lambda bi, qi_: (bi, 0, 0), t // tq),
                  pl.BlockSpec((1, tq, s_pad), lambda bi, qi_: (bi, qi_, 0)),
                  _resident((2, N_KV_HEADS, rows, tk), lambda bi, qi_: (0, 0, 0, 0))],
        out_specs=pl.BlockSpec((1, tq, N_HEADS * HEAD_DIM), lambda bi, qi_: (bi, qi_, 0)),
        out_shape=jax.ShapeDtypeStruct((b, t, N_HEADS * HEAD_DIM), MXU_DTYPE),
        scratch_shapes=[pltpu.VMEM((N_KV_HEADS, rows, LANES), F32),
                        pltpu.VMEM((N_KV_HEADS, rows, _V_AUG), F32),
                        pltpu.VMEM((N_KV_HEADS, rows, HEAD_DIM), MXU_DTYPE)],
        compiler_params=_cparams(2),
        name="attention",
    )(q, kt, v_aug, mask, bias)


def _layer_norm(x, g, b):
    mu = jnp.mean(x, axis=-1, keepdims=True)
    xc = x - mu
    var = jnp.mean(xc * xc, axis=-1, keepdims=True)
    return xc * lax.rsqrt(var + LN_EPS) * g + b


def _post_kernel(o_ref, yn_ref, gs_ref, ga_ref, x_ref, wso_ref, wao_ref, wout_ref, wup_ref, wdown_ref,
                 ln_ref, out_ref):
    y1 = jnp.dot(yn_ref[...], wso_ref[...], preferred_element_type=F32)
    y2 = jnp.dot(o_ref[...], wao_ref[...], preferred_element_type=F32)
    mix = _sigmoid(gs_ref[...]) * y1 + _sigmoid(ga_ref[...]) * y2
    mixed = _dot(mix, wout_ref[...])
    h = _layer_norm(ALPHA * x_ref[...] + mixed, ln_ref[0:1, :], ln_ref[1:2, :])
    u = jnp.maximum(_dot(h, wup_ref[...]), 0.0)
    f = _dot(u * u, wdown_ref[...])
    out_ref[...] = _layer_norm(ALPHA * h + f, ln_ref[2:3, :], ln_ref[3:4, :])


def _post(o, yn, gs, ga, x2d, w_ssd_o, w_attn_o, w_out, w_up, w_down, ln, tm):
    m = x2d.shape[0]
    row = lambda w: pl.BlockSpec((tm, w), lambda i: (i, 0))
    const = lambda i: (0, 0)
    return pl.pallas_call(
        _post_kernel,
        grid=(m // tm,),
        in_specs=[row(N_HEADS * HEAD_DIM), row(SSD_INNER), row(D_MODEL), row(D_MODEL), row(D_MODEL),
                  _resident((SSD_INNER, D_MODEL), const),
                  _resident((N_HEADS * HEAD_DIM, D_MODEL), const),
                  _resident((D_MODEL, D_MODEL), const),
                  _resident((D_MODEL, D_FF), const),
                  _resident((D_FF, D_MODEL), const),
                  _resident((4, D_MODEL), const)],
        out_specs=row(D_MODEL),
        out_shape=jax.ShapeDtypeStruct((m, D_MODEL), F32),
        compiler_params=_cparams(1),
        name="post",
    )(o, yn, gs, ga, x2d, w_ssd_o, w_attn_o, w_out, w_up, w_down, ln)


def _round_up(n, m):
    return (n + m - 1) // m * m


def _layer(x, cache_k, cache_v, cache_kidx, state_ssm, state_conv, bias_tiles, w_prep, conv_w, conv_b,
           dt_bias, a_log, d_skip, ssd_norm_w, w_ssd_o, w_attn_o, w_out, w_up, w_down, ln,
           *, tq, tk, wide, tqs, wk, lc, tm):
    b, t, _ = x.shape
    past = cache_k.shape[1]
    s_len = past + t
    m = b * t
    assert m % tm == 0 and t % tq == 0 and past % tk == 0 and tk % LANES == 0 and wide % tk == 0
    assert t % tqs == 0 and wk % LANES == 0
    assert tq == tk or t == tq, "every query tile must start on a key-tile boundary"
    assert t % lc == 0 or t < lc
    x2d = x.reshape(m, D_MODEL)

    z, xbc, q, k, v, qi, gs, ga, sm = _inproj(x2d, w_prep, tm)

    t_pad = _round_up(t, lc)
    pad_t = lambda a: jnp.pad(a.reshape(b, t, a.shape[-1]), ((0, 0), (0, t_pad - t), (0, 0)))
    sm3 = pad_t(sm)
    dt_t = sm3[:, :, SM_DT:SM_DT + SSD_HEADS].transpose(0, 2, 1)
    yn, ssm_new = _ssd(pad_t(xbc), pad_t(z), sm3, dt_t, state_conv.astype(F32),
                       _state_to_kernel_layout(state_ssm.astype(F32)), conv_w, conv_b, dt_bias, a_log, d_skip,
                       ssd_norm_w, lc, min(t, lc))
    yn = yn[:, :t].reshape(m, SSD_INNER)
    new_ssm = _state_from_kernel_layout(ssm_new)
    xbc3 = xbc.reshape(b, t, CONV_DIM)
    new_conv = jnp.concatenate([state_conv.astype(F32), xbc3], axis=1)[:, -_HALO:]

    k3 = k.reshape(b, t, KV_DIM)
    v3 = v.reshape(b, t, KV_DIM)
    ki3 = sm.reshape(b, t, LANES)[:, :, SM_KI:SM_KI + IDX_DIM]
    s_pad = _round_up(s_len, max(wide, wk))
    assert s_pad % wide == 0 and s_pad % wk == 0
    pad_s = lambda a: jnp.pad(a, ((0, 0), (0, s_pad - s_len)) + ((0, 0),) * (a.ndim - 2))
    k_all = pad_s(jnp.concatenate([cache_k.reshape(b, past, KV_DIM).astype(MXU_DTYPE), k3.astype(MXU_DTYPE)], axis=1))
    v_all = jnp.concatenate([cache_v.astype(MXU_DTYPE), v3.reshape(b, t, N_KV_HEADS, HEAD_DIM).astype(MXU_DTYPE)], axis=1)
    v_aug = pad_s(jnp.concatenate([v_all, jnp.ones((b, s_len, N_KV_HEADS, 1), MXU_DTYPE),
                                   jnp.zeros((b, s_len, N_KV_HEADS, _V_AUG - HEAD_DIM - 1), MXU_DTYPE)], axis=3))
    ki_all = pad_s(jnp.concatenate([cache_kidx.astype(MXU_DTYPE), ki3.astype(MXU_DTYPE)], axis=1))
    n_sel = min(TOPK_MAX, s_len // 4)
    mask = _select(qi.reshape(b, t, -1), sm.reshape(b, t, LANES), ki_all.transpose(0, 2, 1),
                   tqs, wk, s_len, past, n_sel)
    bias = bias_tiles[:, :, :tq, :].reshape(2, N_KV_HEADS, Q_PER_KV * tq, tk)
    o = _attention(q.reshape(b, t, -1), k_all.transpose(0, 2, 1), v_aug.reshape(b, s_pad, N_KV_HEADS * _V_AUG),
                   mask, bias, tq, tk, wide, past)

    out = _post(o.reshape(m, -1), yn, gs, ga, x2d, w_ssd_o, w_attn_o, w_out, w_up, w_down, ln, tm)
    return out.reshape(b, t, D_MODEL), (k3.reshape(b, t, N_KV_HEADS, HEAD_DIM), v3.reshape(b, t, N_KV_HEADS, HEAD_DIM),
                                        ki3, new_ssm, new_conv)


ATTN_TILE = 128
WIDE_TILE = 1024
WIDE_TILE_SHORT = 512
SELECT_ROWS = 128
SELECT_WIDTH = 1024
SSD_CHUNK = 128
ROW_TILE = 256


def kernel(x_prompt, x_sample, cache_k, cache_v, cache_kidx, state_ssm, state_conv, rel_bias, w_in, conv_w, conv_b,
           dt_bias, a_log, d_skip, ssd_norm_w, w_ssd_o, w_attn_o, w_out, ln1_g, ln1_b, w_up, w_down, ln2_g, ln2_b):
    bp = x_prompt.shape[0]
    dtype = x_prompt.dtype
    bias_tiles = _bias_tiles(rel_bias, ATTN_TILE)
    yp, ys = x_prompt, x_sample
    st_p, st_s = [], []
    tiles = dict(tk=ATTN_TILE, wk=SELECT_WIDTH, lc=SSD_CHUNK, tm=ROW_TILE)
    for l in range(DEPTH):
        w_prep = _prep_w_in(w_in[l])
        ln = jnp.stack([ln1_g[l], ln1_b[l], ln2_g[l], ln2_b[l]]).astype(F32)
        weights = (bias_tiles, w_prep, conv_w[l], conv_b[l], dt_bias[l], a_log[l], d_skip[l], ssd_norm_w[l],
                   w_ssd_o[l].astype(MXU_DTYPE), w_attn_o[l].astype(MXU_DTYPE), w_out[l].astype(MXU_DTYPE),
                   w_up[l].astype(MXU_DTYPE), w_down[l].astype(MXU_DTYPE), ln)
        yp, sp = _layer(yp, jnp.zeros((bp, 0, N_KV_HEADS, HEAD_DIM), dtype), jnp.zeros((bp, 0, N_KV_HEADS, HEAD_DIM), dtype),
                        jnp.zeros((bp, 0, IDX_DIM), dtype), jnp.zeros((bp, SSD_HEADS, SSD_HEAD_DIM, D_STATE), dtype),
                        jnp.zeros((bp, _HALO, CONV_DIM), dtype), *weights,
                        tq=min(ATTN_TILE, yp.shape[1]), tqs=min(SELECT_ROWS, yp.shape[1]), wide=WIDE_TILE, **tiles)
        ys, ss = _layer(ys, cache_k[l], cache_v[l], cache_kidx[l], state_ssm[l], state_conv[l], *weights,
                        tq=min(ATTN_TILE, ys.shape[1]), tqs=min(SELECT_ROWS, ys.shape[1]), wide=WIDE_TILE_SHORT, **tiles)
        st_p.append(sp)
        st_s.append(ss)
    stack = lambda sts, i: jnp.stack([s[i] for s in sts])
    return (yp, ys) + tuple(stack(st_p, i) for i in range(5)) + tuple(stack(st_s, i) for i in range(5))
```

```python
import functools

import jax
import jax.numpy as jnp
from jax import lax
from jax.experimental import pallas as pl
from jax.experimental.pallas import tpu as pltpu

F32 = jnp.float32
I32 = jnp.int32
MXU_DTYPE = jnp.bfloat16

D_MODEL = 1024
CHUNK = 64
SSD_INNER = 2048
SSD_HEAD_DIM = 64
SSD_HEADS = 32
SSD_GROUPS = 4
HEADS_PER_GROUP = SSD_HEADS // SSD_GROUPS
D_STATE = 128
CONV_W = 4
CONV_DIM = SSD_INNER + 2 * SSD_GROUPS * D_STATE
N_HEADS = 16
N_KV_HEADS = 4
Q_PER_KV = N_HEADS // N_KV_HEADS
HEAD_DIM = 64
KV_DIM = N_KV_HEADS * HEAD_DIM
IDX_HEADS = 8
IDX_DIM = 64
TOPK_MAX = 256
REL_BUCKETS = 32
REL_MAX_DIST = 128
D_FF = 4 * D_MODEL
DEPTH = 1
ALPHA = (2.0 * DEPTH) ** 0.25
LN_EPS = 1e-5
RMS_EPS = 1e-5
IN_SIZES = (SSD_INNER, CONV_DIM, SSD_HEADS, N_HEADS * HEAD_DIM, KV_DIM, KV_DIM,
            IDX_HEADS * IDX_DIM, IDX_DIM, IDX_HEADS, D_MODEL, D_MODEL)

LANES = 128
VMEM_LIMIT_BYTES = 56 * 1024 * 1024

SM_KI = 0
SM_DT = IDX_DIM
SM_WI = IDX_DIM + SSD_HEADS
SM_PAD = LANES - (IDX_DIM + SSD_HEADS + IDX_HEADS)

NEG_BIG = -1e30
INT_MIN = -2 ** 31
INT_MAX = 2 ** 31 - 1
KEY_NEG_INF = -0x7F800000


def _cparams(n_axes):
    return pltpu.CompilerParams(dimension_semantics=("arbitrary",) * n_axes,
                                vmem_limit_bytes=VMEM_LIMIT_BYTES)


def _resident(shape, index_map):
    return pl.BlockSpec(shape, index_map, pipeline_mode=pl.Buffered(1))


def _per_batch(shape, index_map, steps_per_batch):
    if steps_per_batch > 1:
        return _resident(shape, index_map)
    return pl.BlockSpec(shape, index_map)


def _dot(a, b):
    return jnp.dot(a.astype(MXU_DTYPE), b.astype(MXU_DTYPE), preferred_element_type=F32)


def _dot_exact(a, b):
    return jnp.dot(a, b, preferred_element_type=F32, precision=lax.Precision.HIGHEST)


def _sigmoid(x):
    return 0.5 * jnp.tanh(0.5 * x) + 0.5


def _softplus(x):
    return jnp.maximum(x, 0.0) + jnp.log1p(jnp.exp(-jnp.abs(x)))


_PROJ_WIDTHS = (SSD_INNER, CONV_DIM, N_HEADS * HEAD_DIM, KV_DIM, KV_DIM, IDX_HEADS * IDX_DIM,
                D_MODEL, D_MODEL, LANES)
_PROJ_WIDTH = sum(_PROJ_WIDTHS)
_PROJ_CHUNK = 512


def _prep_w_in(w_in):
    offs = [0]
    for s in IN_SIZES:
        offs.append(offs[-1] + s)
    z, xbc, dt, q, k, v, qi, ki, wi, gs, ga = [w_in[:, offs[i]:offs[i + 1]] for i in range(len(IN_SIZES))]
    small = jnp.concatenate([ki, dt, wi, jnp.zeros((D_MODEL, SM_PAD), w_in.dtype)], axis=1)
    w = jnp.concatenate([z, xbc, q * HEAD_DIM ** -0.5, k, v, qi * IDX_DIM ** -0.5, gs, ga, small], axis=1)
    return w.astype(MXU_DTYPE)


def _inproj_kernel(x_ref, w_ref, z_ref, xbc_ref, q_ref, k_ref, v_ref, qi_ref, gs_ref, ga_ref, sm_ref):
    xb = x_ref[...].astype(MXU_DTYPE)
    outs = (z_ref, xbc_ref, q_ref, k_ref, v_ref, qi_ref, gs_ref, ga_ref, sm_ref)
    c0 = 0
    for ref, width in zip(outs, _PROJ_WIDTHS):
        for o in range(0, width, _PROJ_CHUNK):
            n = min(_PROJ_CHUNK, width - o)
            r = jnp.dot(xb, w_ref[:, c0 + o:c0 + o + n], preferred_element_type=F32)
            ref[:, o:o + n] = r.astype(ref.dtype)
        c0 += width


def _inproj(x2d, w_prep, tm):
    m = x2d.shape[0]
    dtypes = (F32, F32, MXU_DTYPE, F32, F32, MXU_DTYPE, F32, F32, F32)
    out_shape = tuple(jax.ShapeDtypeStruct((m, w), d) for w, d in zip(_PROJ_WIDTHS, dtypes))
    out_specs = tuple(pl.BlockSpec((tm, w), lambda i: (i, 0)) for w in _PROJ_WIDTHS)
    return pl.pallas_call(
        _inproj_kernel,
        grid=(m // tm,),
        in_specs=[pl.BlockSpec((tm, D_MODEL), lambda i: (i, 0)),
                  _resident((D_MODEL, _PROJ_WIDTH), lambda i: (0, 0))],
        out_specs=out_specs,
        out_shape=out_shape,
        compiler_params=_cparams(1),
        name="inproj",
    )(x2d, w_prep)


_HALO = CONV_W - 1
_XP_BASE = 8


def _ssd_kernel(xbc_ref, z_ref, sm_ref, dtT_ref, conv0_ref, ssm0_ref, convw_ref, convb_ref,
                hp_ref, hpT_ref, dfull_ref, nw_ref,
                yn_ref, ssm_out_ref, xp_ref, st_ref, y_ref, *, lc, t_valid):
    c = pl.program_id(1)
    nc = pl.num_programs(1)

    @pl.when(c == 0)
    def _():
        xp_ref[_XP_BASE - _HALO:_XP_BASE, :] = conv0_ref[0]
        st_ref[...] = ssm0_ref[0]

    xp_ref[_XP_BASE:_XP_BASE + lc, :] = xbc_ref[0]
    staged = xp_ref[...]
    conv = convb_ref[...]
    for i in range(CONV_W):
        back = _HALO - i
        shifted = pltpu.roll(staged, back, axis=0) if back else staged
        conv = conv + shifted[_XP_BASE:, :] * convw_ref[i:i + 1, :]
    xp_ref[_XP_BASE - _HALO:_XP_BASE, :] = xp_ref[_XP_BASE + lc - _HALO:_XP_BASE + lc, :]
    act = conv * _sigmoid(conv)
    xs = act[:, :SSD_INNER]
    bm = act[:, SSD_INNER:SSD_INNER + SSD_GROUPS * D_STATE]
    cm = act[:, SSD_INNER + SSD_GROUPS * D_STATE:]

    dt = _softplus(sm_ref[0][:, SM_DT:SM_DT + SSD_HEADS] + hp_ref[0:1, :])
    dt_t = _softplus(dtT_ref[0] + hpT_ref[:, 0:1])
    if t_valid < lc:
        dt = jnp.where(lax.broadcasted_iota(I32, (lc, SSD_HEADS), 0) < t_valid, dt, 0.0)
        dt_t = jnp.where(lax.broadcasted_iota(I32, (SSD_HEADS, lc), 1) < t_valid, dt_t, 0.0)
    a = dt * (-jnp.exp(hp_ref[1:2, :]))
    a_t = dt_t * (-jnp.exp(hpT_ref[:, 1:2]))
    ri = lax.broadcasted_iota(I32, (lc, lc), 0)
    ci = lax.broadcasted_iota(I32, (lc, lc), 1)
    causal = ci <= ri
    acum = _dot_exact(causal.astype(F32), a)
    acum_t = _dot_exact(a_t, (ri <= ci).astype(F32))
    e_a = jnp.exp(acum)
    w_end = jnp.exp(acum[lc - 1:lc, :] - acum) * dt
    lane = lax.broadcasted_iota(I32, (lc, 2 * SSD_HEAD_DIM), 1)
    first_head = lane < SSD_HEAD_DIM
    gw = HEADS_PER_GROUP * SSD_HEAD_DIM

    def per_head_lanes(v, h0):
        return jnp.where(first_head, v[:, h0:h0 + 1], v[:, h0 + 1:h0 + 2])

    for g in range(SSD_GROUPS):
        b_g = bm[:, g * D_STATE:(g + 1) * D_STATE]
        c_g = cm[:, g * D_STATE:(g + 1) * D_STATE].astype(MXU_DTYPE)
        b_gt = b_g.T.astype(MXU_DTYPE)
        cb = jnp.dot(c_g, b_gt, preferred_element_type=F32)
        state = st_ref[g]
        y_off = _dot(c_g, state)
        xw, e_last = [], []
        for jp in range(HEADS_PER_GROUP // 2):
            h0 = g * HEADS_PER_GROUP + 2 * jp
            ms = []
            for h in (h0, h0 + 1):
                seg = acum[:, h:h + 1] - acum_t[h:h + 1, :]
                decay = jnp.exp(jnp.where(causal, seg, -jnp.inf))
                ms.append((cb * decay * dt_t[h:h + 1, :]).astype(MXU_DTYPE))
            c0 = h0 * SSD_HEAD_DIM
            x_pair = xs[:, c0:c0 + 2 * SSD_HEAD_DIM]
            r = jnp.dot(jnp.concatenate(ms, axis=0), x_pair.astype(MXU_DTYPE), preferred_element_type=F32)
            y_diag = jnp.where(first_head, r[:lc], r[lc:])
            o0 = 2 * jp * SSD_HEAD_DIM
            e_pair = per_head_lanes(e_a, h0)
            y_ref[:, c0:c0 + 2 * SSD_HEAD_DIM] = (y_diag + y_off[:, o0:o0 + 2 * SSD_HEAD_DIM] * e_pair
                                                  + dfull_ref[:, c0:c0 + 2 * SSD_HEAD_DIM] * x_pair)
            xw.append((x_pair * per_head_lanes(w_end, h0)).astype(MXU_DTYPE))
            e_last.append(e_pair[lc - 1:lc, :])
        st_ref[g] = (state * jnp.concatenate(e_last, axis=1)
                     + jnp.dot(b_gt, jnp.concatenate(xw, axis=1), preferred_element_type=F32))

    zv = z_ref[0]
    hg = y_ref[...] * (zv * _sigmoid(zv))
    for g in range(SSD_GROUPS):
        sl = hg[:, g * gw:(g + 1) * gw]
        ms = jnp.mean(sl * sl, axis=-1, keepdims=True)
        yn_ref[0, :, g * gw:(g + 1) * gw] = (sl * lax.rsqrt(ms + RMS_EPS) * nw_ref[:, g * gw:(g + 1) * gw]).astype(yn_ref.dtype)

    @pl.when(c == nc - 1)
    def _():
        ssm_out_ref[0] = st_ref[...]


def _ssd(xbc, z, sm, dt_t, conv0, ssm0_t, conv_w, conv_b, dt_bias, a_log, d_skip, norm_w, lc, t_valid):
    b, t, _ = xbc.shape
    nc = t // lc
    hp = jnp.stack([dt_bias, a_log]).astype(F32)
    d_full = jnp.repeat(d_skip.astype(F32), SSD_HEAD_DIM)[None, :]
    gw = HEADS_PER_GROUP * SSD_HEAD_DIM
    const2 = lambda bi, ci: (0, 0)
    kern = functools.partial(_ssd_kernel, lc=lc, t_valid=t_valid)
    return pl.pallas_call(
        kern,
        grid=(b, nc),
        in_specs=[pl.BlockSpec((1, lc, CONV_DIM), lambda bi, ci: (bi, ci, 0)),
                  pl.BlockSpec((1, lc, SSD_INNER), lambda bi, ci: (bi, ci, 0)),
                  pl.BlockSpec((1, lc, LANES), lambda bi, ci: (bi, ci, 0)),
                  pl.BlockSpec((1, SSD_HEADS, lc), lambda bi, ci: (bi, 0, ci)),
                  pl.BlockSpec((1, _HALO, CONV_DIM), lambda bi, ci: (bi, 0, 0)),
                  pl.BlockSpec((1, SSD_GROUPS, D_STATE, gw), lambda bi, ci: (bi, 0, 0, 0)),
                  pl.BlockSpec((CONV_W, CONV_DIM), const2),
                  pl.BlockSpec((1, CONV_DIM), const2),
                  pl.BlockSpec((2, SSD_HEADS), const2),
                  pl.BlockSpec((SSD_HEADS, 2), const2),
                  pl.BlockSpec((1, SSD_INNER), const2),
                  pl.BlockSpec((1, SSD_INNER), const2)],
        out_specs=(pl.BlockSpec((1, lc, SSD_INNER), lambda bi, ci: (bi, ci, 0)),
                   pl.BlockSpec((1, SSD_GROUPS, D_STATE, gw), lambda bi, ci: (bi, 0, 0, 0))),
        out_shape=(jax.ShapeDtypeStruct((b, t, SSD_INNER), MXU_DTYPE),
                   jax.ShapeDtypeStruct((b, SSD_GROUPS, D_STATE, gw), F32)),
        scratch_shapes=[pltpu.VMEM((_XP_BASE + lc, CONV_DIM), F32),
                        pltpu.VMEM((SSD_GROUPS, D_STATE, gw), F32),
                        pltpu.VMEM((lc, SSD_INNER), F32)],
        compiler_params=_cparams(2),
        name="ssd",
    )(xbc, z, sm, dt_t, conv0, ssm0_t, conv_w.astype(F32), conv_b.astype(F32)[None, :], hp, hp.T,
      d_full, norm_w.astype(F32)[None, :])


def _state_to_kernel_layout(s):
    b = s.shape[0]
    s = s.reshape(b, SSD_GROUPS, HEADS_PER_GROUP, SSD_HEAD_DIM, D_STATE)
    return s.transpose(0, 1, 4, 2, 3).reshape(b, SSD_GROUPS, D_STATE, HEADS_PER_GROUP * SSD_HEAD_DIM)


def _state_from_kernel_layout(s):
    b = s.shape[0]
    s = s.reshape(b, SSD_GROUPS, D_STATE, HEADS_PER_GROUP, SSD_HEAD_DIM)
    return s.transpose(0, 1, 3, 4, 2).reshape(b, SSD_HEADS, SSD_HEAD_DIM, D_STATE)


_T5_LOG_STARTS = (12, 16, 23, 32, 46, 64, 91)
_T5_FAR_BUCKET = REL_BUCKETS // 2 - 1


def _bias_kernel(rb_ref, out_ref, *, tk):
    half = REL_BUCKETS // 2
    max_exact = half // 2
    ri = lax.broadcasted_iota(I32, (tk, tk), 0)
    ci = lax.broadcasted_iota(I32, (tk, tk), 1)
    for t in range(2):
        rel = ci - ri - (1 - t) * tk
        n = jnp.abs(rel)
        large = jnp.full((tk, tk), max_exact, I32)
        for start in _T5_LOG_STARTS:
            large = large + (n >= start).astype(I32)
        bucket = jnp.where(rel > 0, half, 0) + jnp.where(n < max_exact, n, large)
        for h in range(N_HEADS):
            acc = jnp.zeros((tk, tk), F32)
            for bkt in range(REL_BUCKETS):
                acc = jnp.where(bucket == bkt, rb_ref[bkt, h], acc)
            out_ref[t, h] = acc - rb_ref[_T5_FAR_BUCKET, h]


def _bias_tiles(rel_bias, tk):
    return pl.pallas_call(
        functools.partial(_bias_kernel, tk=tk),
        in_specs=[pl.BlockSpec(memory_space=pltpu.SMEM)],
        out_specs=pl.BlockSpec(memory_space=pltpu.VMEM),
        out_shape=jax.ShapeDtypeStruct((2, N_HEADS, tk, tk), F32),
        name="bias_tiles",
    )(rel_bias.astype(F32))


_MAX_BISECTIONS = 40
_SCORE_COLS = 256


def _select_kernel(qi_ref, sm_ref, kit_ref, mask_ref, key_ref, w_ref, *, tq, wk, s_len, q_start, n_sel):
    qb = pl.program_id(1)
    n_blocks = mask_ref.shape[2] // wk
    nt = wk // LANES
    p0 = q_start + qb * tq
    pos = p0 + lax.broadcasted_iota(I32, (tq, 1), 0)
    vis_end = (pos // CHUNK + 1) * CHUNK
    vis_max = jnp.minimum(((p0 + tq - 1) // CHUNK + 1) * CHUNK, s_len)
    nkb = (vis_max + wk - 1) // wk

    wi = sm_ref[0][:, SM_WI:SM_WI + IDX_HEADS] * IDX_HEADS ** -0.5
    for hh in range(IDX_HEADS):
        w_ref[hh] = jnp.broadcast_to(wi[:, hh:hh + 1], (tq, LANES))
    lane_k = lax.broadcasted_iota(I32, (1, LANES), 1)

    def to_key(score):
        bits = lax.bitcast_convert_type(score, I32)
        return jnp.where(bits < 0, INT_MIN - bits, bits)

    def score_block(j, top, masked):
        m1, m2 = top
        k0 = pl.multiple_of(j * wk, wk)
        for c in range(wk // _SCORE_COLS):
            c0 = k0 + c * _SCORE_COLS
            kt = kit_ref[0, :, pl.ds(c0, _SCORE_COLS)]
            d = jnp.maximum(jnp.dot(qi_ref[0, 0], kt, preferred_element_type=F32), 0.0)
            for u in range(_SCORE_COLS // LANES):
                sc = jnp.zeros((tq, LANES), F32)
                for hh in range(IDX_HEADS):
                    sc = sc + w_ref[hh] * d[hh * tq:(hh + 1) * tq, u * LANES:(u + 1) * LANES]
                if masked:
                    key_pos = c0 + u * LANES + lane_k
                    sc = jnp.where((key_pos < vis_end) & (key_pos < s_len), sc, -jnp.inf)
                key_ref[:, pl.ds(c0 + u * LANES, LANES)] = to_key(sc)
                m2 = jnp.maximum(m2, jnp.minimum(m1, sc))
                m1 = jnp.maximum(m1, sc)
        return m1, m2

    n_open = jnp.minimum(p0, s_len) // wk
    top = (jnp.full((tq, LANES), -jnp.inf, F32),) * 2
    top = lax.fori_loop(0, n_open, functools.partial(score_block, masked=False), top)
    m1, m2 = lax.fori_loop(n_open, nkb, functools.partial(score_block, masked=True), top)

    def count_ge(thr):
        thr_b = jnp.broadcast_to(thr, (tq, LANES))

        def body(j, acc):
            k0 = pl.multiple_of(j * wk, wk)
            for u in range(nt):
                acc = acc + (key_ref[:, pl.ds(k0 + u * LANES, LANES)] >= thr_b).astype(I32)
            return acc

        acc = lax.fori_loop(0, nkb, body, jnp.zeros((tq, LANES), I32))
        return jnp.sum(acc, axis=1, keepdims=True)

    def unfinished(lo, hi, cnt_lo):
        return (cnt_lo > n_sel) & (hi - 1 > lo)

    def bisect(count_fn, lo, hi, cnt_lo, cnt_hi):
        def cond(st):
            return (st[0] < _MAX_BISECTIONS) & (st[1] > 0)

        def body(st):
            it, _, lo, hi, cnt_lo, cnt_hi = st
            live = unfinished(lo, hi, cnt_lo)
            mid = (lo >> 1) + (hi >> 1) + (lo & hi & 1)
            cnt = count_fn(jnp.where(live, mid, lo))
            up = live & (cnt >= n_sel)
            dn = live & (cnt < n_sel)
            lo = jnp.where(up, mid, lo)
            cnt_lo = jnp.where(up, cnt, cnt_lo)
            hi = jnp.where(dn, mid, hi)
            cnt_hi = jnp.where(dn, cnt, cnt_hi)
            return (it + 1, jnp.max(unfinished(lo, hi, cnt_lo).astype(I32)), lo, hi, cnt_lo, cnt_hi)

        st0 = (jnp.int32(0), jnp.max(unfinished(lo, hi, cnt_lo).astype(I32)), lo, hi, cnt_lo, cnt_hi)
        return lax.while_loop(cond, body, st0)[2:]

    lo = jnp.maximum(to_key(jnp.min(m2, axis=1, keepdims=True)), KEY_NEG_INF + 1)
    hi = to_key(jnp.max(m1, axis=1, keepdims=True)) + 1
    tau, _, cnt_tau, cnt_above = bisect(count_ge, lo, hi, count_ge(lo), jnp.zeros((tq, 1), I32))

    tie_rows = cnt_tau > n_sel
    has_ties = jnp.max(tie_rows.astype(I32)) > 0

    @pl.when(jnp.logical_not(has_ties))
    def _():
        tau_b = jnp.broadcast_to(tau, (tq, LANES))

        def body(j, carry):
            k0 = pl.multiple_of(j * wk, wk)
            for u in range(nt):
                sl = pl.ds(k0 + u * LANES, LANES)
                mask_ref[0, :, sl] = (key_ref[:, sl] >= tau_b).astype(I32).astype(mask_ref.dtype)
            return carry
        lax.fori_loop(0, nkb, body, 0)

    @pl.when(has_ties)
    def _():
        upper = (lax.broadcasted_iota(I32, (wk, wk), 0) <= lax.broadcasted_iota(I32, (wk, wk), 1)).astype(MXU_DTYPE)
        need = jnp.where(tie_rows, n_sel - cnt_above, wk * n_blocks).astype(F32)

        def body(j, seen):
            k0 = pl.multiple_of(j * wk, wk)
            blk = key_ref[:, pl.ds(k0, wk)]
            eq = blk == tau
            rank = seen + jnp.dot(eq.astype(F32).astype(MXU_DTYPE), upper, preferred_element_type=F32)
            sel = (blk > tau) | (eq & (rank <= need))
            mask_ref[0, :, pl.ds(k0, wk)] = sel.astype(I32).astype(mask_ref.dtype)
            return seen + jnp.sum(eq.astype(F32), axis=1, keepdims=True)
        lax.fori_loop(0, nkb, body, jnp.zeros((tq, 1), F32))

    def zero_body(j, carry):
        k0 = pl.multiple_of(j * wk, wk)
        mask_ref[0, :, pl.ds(k0, wk)] = jnp.zeros((tq, wk), mask_ref.dtype)
        return carry

    lax.fori_loop(nkb, n_blocks, zero_body, 0)


def _select(qi, sm, kit, tq, wk, s_len, q_start, n_sel):
    b, t, _ = qi.shape
    s_pad = kit.shape[2]
    assert n_sel <= 2 * LANES, "the starting bracket takes two candidates per lane column"
    kern = functools.partial(_select_kernel, tq=tq, wk=wk, s_len=s_len, q_start=q_start, n_sel=n_sel)
    qi = qi.reshape(b, t // tq, tq, IDX_HEADS, IDX_DIM).transpose(0, 1, 3, 2, 4).reshape(b, t // tq, IDX_HEADS * tq, IDX_DIM)
    return pl.pallas_call(
        kern,
        grid=(b, t // tq),
        in_specs=[pl.BlockSpec((1, 1, IDX_HEADS * tq, IDX_DIM), lambda bi, qi_: (bi, qi_, 0, 0)),
                  pl.BlockSpec((1, tq, LANES), lambda bi, qi_: (bi, qi_, 0)),
                  _per_batch((1, IDX_DIM, s_pad), lambda bi, qi_: (bi, 0, 0), t // tq)],
        out_specs=pl.BlockSpec((1, tq, s_pad), lambda bi, qi_: (bi, qi_, 0)),
        out_shape=jax.ShapeDtypeStruct((b, t, s_pad), jnp.int8),
        scratch_shapes=[pltpu.VMEM((tq, s_pad), I32),
                        pltpu.VMEM((IDX_HEADS, tq, LANES), F32)],
        compiler_params=_cparams(2),
        name="select",
    )(qi, sm, kit)


_V_AUG = 2 * HEAD_DIM
_V_ROWS = 512


def _attn_kernel(q_ref, kt_ref, v_ref, mask_ref, bias_ref, o_ref, m_ref, acc_ref, qg_ref, vaug_ref,
                 *, tq, tk, wide, q_start):
    qb = pl.program_id(1)
    jd = (q_start + qb * tq) // tk

    @pl.when(qb == 0)
    def _():
        one_hot = (lax.broadcasted_iota(I32, (_V_ROWS, _V_AUG - HEAD_DIM), 1) == 0).astype(vaug_ref.dtype)

        def body(i, carry):
            r0 = pl.multiple_of(i * _V_ROWS, _V_ROWS)
            blk = v_ref[0, pl.ds(r0, _V_ROWS), :]
            for g in range(N_KV_HEADS):
                vaug_ref[pl.ds(r0, _V_ROWS), g * _V_AUG:g * _V_AUG + HEAD_DIM] = blk[:, g * HEAD_DIM:(g + 1) * HEAD_DIM]
                vaug_ref[pl.ds(r0, _V_ROWS), g * _V_AUG + HEAD_DIM:(g + 1) * _V_AUG] = one_hot
            return carry

        lax.fori_loop(0, vaug_ref.shape[0] // _V_ROWS, body, 0)

    for g in range(N_KV_HEADS):
        for r in range(Q_PER_KV):
            c0 = (g * Q_PER_KV + r) * HEAD_DIM
            qg_ref[g, r * tq:(r + 1) * tq, :] = q_ref[0, :, c0:c0 + HEAD_DIM]
    m_ref[...] = jnp.full(m_ref.shape, NEG_BIG, F32)
    acc_ref[...] = jnp.zeros(acc_ref.shape, F32)

    def process(k0, width, near):
        nt = width // LANES
        mb = jnp.where(mask_ref[0, :, pl.ds(k0, width)].astype(I32) != 0, 0.0, NEG_BIG)
        mb = jnp.concatenate([mb] * Q_PER_KV, axis=0)
        for g in range(N_KV_HEADS):
            kt = kt_ref[0, g * HEAD_DIM:(g + 1) * HEAD_DIM, pl.ds(k0, width)]
            s = jnp.dot(qg_ref[g], kt, preferred_element_type=F32) + mb
            if near is not None:
                s = s + bias_ref[near, g]
            tiles = [s[:, u * LANES:(u + 1) * LANES] for u in range(nt)]
            mx = tiles[0]
            for u in range(1, nt):
                mx = jnp.maximum(mx, tiles[u])
            m_old = m_ref[g]
            m_new = jnp.maximum(m_old, jnp.max(mx, axis=1, keepdims=True))
            alpha = jnp.exp(m_old - m_new)
            p = jnp.concatenate([jnp.exp(tl - m_new).astype(MXU_DTYPE) for tl in tiles], axis=1)
            vt = vaug_ref[pl.ds(k0, width), g * _V_AUG:(g + 1) * _V_AUG]
            acc_ref[g] = alpha * acc_ref[g] + jnp.dot(p, vt, preferred_element_type=F32)
            m_ref[g] = m_new

    n_far = jnp.maximum(jd - 1, 0)
    per_wide = wide // tk
    n_wide = n_far // per_wide

    def wide_body(j, carry):
        process(pl.multiple_of(j * wide, wide), wide, None)
        return carry

    def far_body(j, carry):
        process(pl.multiple_of(j * tk, tk), tk, None)
        return carry

    lax.fori_loop(0, n_wide, wide_body, 0)
    lax.fori_loop(n_wide * per_wide, n_far, far_body, 0)

    @pl.when(jd >= 1)
    def _():
        process(pl.multiple_of((jd - 1) * tk, tk), tk, 0)

    process(pl.multiple_of(jd * tk, tk), tk, 1)

    for g in range(N_KV_HEADS):
        acc = acc_ref[g]
        o = acc[:, :HEAD_DIM] / acc[:, HEAD_DIM:HEAD_DIM + 1]
        for r in range(Q_PER_KV):
            c0 = (g * Q_PER_KV + r) * HEAD_DIM
            o_ref[0, :, c0:c0 + HEAD_DIM] = o[r * tq:(r + 1) * tq].astype(o_ref.dtype)


def _attention(q, kt, v, mask, bias, tq, tk, wide, q_start):
    b, t, _ = q.shape
    s_pad = kt.shape[2]
    assert s_pad % _V_ROWS == 0
    rows = Q_PER_KV * tq
    kern = functools.partial(_attn_kernel, tq=tq, tk=tk, wide=wide, q_start=q_start)
    return pl.pallas_call(
        kern,
        grid=(b, t // tq),
        in_specs=[pl.BlockSpec((1, tq, N_HEADS * HEAD_DIM), lambda bi, qi_: (bi, qi_, 0)),
                  _per_batch((1, KV_DIM, s_pad), lambda bi, qi_: (bi, 0, 0), t // tq),
                  _per_batch((1, s_pad, KV_DIM), lambda bi, qi_: (bi, 0, 0), t // tq),
                  pl.BlockSpec((1, tq, s_pad), lambda bi, qi_: (bi, qi_, 0)),
                  _resident((2, N_KV_HEADS, rows, tk), lambda bi, qi_: (0, 0, 0, 0))],
        out_specs=pl.BlockSpec((1, tq, N_HEADS * HEAD_DIM), lambda bi, qi_: (bi, qi_, 0)),
        out_shape=jax.ShapeDtypeStruct((b, t, N_HEADS * HEAD_DIM), MXU_DTYPE),
        scratch_shapes=[pltpu.VMEM((N_KV_HEADS, rows, LANES), F32),
                        pltpu.VMEM((N_KV_HEADS, rows, _V_AUG), F32),
                        pltpu.VMEM((N_KV_HEADS, rows, HEAD_DIM), MXU_DTYPE),
                        pltpu.VMEM((s_pad, N_KV_HEADS * _V_AUG), MXU_DTYPE)],
        compiler_params=_cparams(2),
        name="attention",
    )(q, kt, v, mask, bias)


def _layer_norm(x, g, b):
    mu = jnp.mean(x, axis=-1, keepdims=True)
    xc = x - mu
    var = jnp.mean(xc * xc, axis=-1, keepdims=True)
    return xc * lax.rsqrt(var + LN_EPS) * g + b


def _post_kernel(o_ref, yn_ref, gs_ref, ga_ref, x_ref, wso_ref, wao_ref, wout_ref, wup_ref, wdown_ref,
                 ln_ref, out_ref):
    y1 = jnp.dot(yn_ref[...], wso_ref[...], preferred_element_type=F32)
    y2 = jnp.dot(o_ref[...], wao_ref[...], preferred_element_type=F32)
    mix = _sigmoid(gs_ref[...]) * y1 + _sigmoid(ga_ref[...]) * y2
    mixed = _dot(mix, wout_ref[...])
    h = _layer_norm(ALPHA * x_ref[...] + mixed, ln_ref[0:1, :], ln_ref[1:2, :])
    u = jnp.maximum(_dot(h, wup_ref[...]), 0.0)
    f = _dot(u * u, wdown_ref[...])
    out_ref[...] = _layer_norm(ALPHA * h + f, ln_ref[2:3, :], ln_ref[3:4, :])


def _post(o, yn, gs, ga, x2d, w_ssd_o, w_attn_o, w_out, w_up, w_down, ln, tm):
    m = x2d.shape[0]
    row = lambda w: pl.BlockSpec((tm, w), lambda i: (i, 0))
    const = lambda i: (0, 0)
    return pl.pallas_call(
        _post_kernel,
        grid=(m // tm,),
        in_specs=[row(N_HEADS * HEAD_DIM), row(SSD_INNER), row(D_MODEL), row(D_MODEL), row(D_MODEL),
                  _resident((SSD_INNER, D_MODEL), const),
                  _resident((N_HEADS * HEAD_DIM, D_MODEL), const),
                  _resident((D_MODEL, D_MODEL), const),
                  _resident((D_MODEL, D_FF), const),
                  _resident((D_FF, D_MODEL), const),
                  _resident((4, D_MODEL), const)],
        out_specs=row(D_MODEL),
        out_shape=jax.ShapeDtypeStruct((m, D_MODEL), F32),
        compiler_params=_cparams(1),
        name="post",
    )(o, yn, gs, ga, x2d, w_ssd_o, w_attn_o, w_out, w_up, w_down, ln)


def _round_up(n, m):
    return (n + m - 1) // m * m


def _layer(x, cache_k, cache_v, cache_kidx, state_ssm, state_conv, bias_tiles, w_prep, conv_w, conv_b,
           dt_bias, a_log, d_skip, ssd_norm_w, w_ssd_o, w_attn_o, w_out, w_up, w_down, ln,
           *, tq, tk, wide, tqs, wk, lc, tm):
    b, t, _ = x.shape
    past = cache_k.shape[1]
    s_len = past + t
    m = b * t
    assert m % tm == 0 and t % tq == 0 and past % tk == 0 and tk % LANES == 0 and wide % tk == 0
    assert t % tqs == 0 and wk % LANES == 0
    assert tq == tk or t == tq, "every query tile must start on a key-tile boundary"
    assert t % lc == 0 or t < lc
    x2d = x.reshape(m, D_MODEL)

    z, xbc, q, k, v, qi, gs, ga, sm = _inproj(x2d, w_prep, tm)

    t_pad = _round_up(t, lc)
    pad_t = lambda a: jnp.pad(a.reshape(b, t, a.shape[-1]), ((0, 0), (0, t_pad - t), (0, 0)))
    sm3 = pad_t(sm)
    dt_t = sm3[:, :, SM_DT:SM_DT + SSD_HEADS].transpose(0, 2, 1)
    yn, ssm_new = _ssd(pad_t(xbc), pad_t(z), sm3, dt_t, state_conv.astype(F32),
                       _state_to_kernel_layout(state_ssm.astype(F32)), conv_w, conv_b, dt_bias, a_log, d_skip,
                       ssd_norm_w, lc, min(t, lc))
    yn = yn[:, :t].reshape(m, SSD_INNER)
    new_ssm = _state_from_kernel_layout(ssm_new)
    xbc3 = xbc.reshape(b, t, CONV_DIM)
    new_conv = jnp.concatenate([state_conv.astype(F32), xbc3], axis=1)[:, -_HALO:]

    k3 = k.reshape(b, t, KV_DIM)
    v3 = v.reshape(b, t, KV_DIM)
    ki3 = sm.reshape(b, t, LANES)[:, :, SM_KI:SM_KI + IDX_DIM]
    s_pad = _round_up(s_len, max(wide, wk))
    assert s_pad % wide == 0 and s_pad % wk == 0
    pad_s = lambda a: jnp.pad(a, ((0, 0), (0, s_pad - s_len)) + ((0, 0),) * (a.ndim - 2))
    k_all = pad_s(jnp.concatenate([cache_k.reshape(b, past, KV_DIM).astype(MXU_DTYPE), k3.astype(MXU_DTYPE)], axis=1))
    v_all = pad_s(jnp.concatenate([cache_v.reshape(b, past, KV_DIM).astype(MXU_DTYPE), v3.astype(MXU_DTYPE)], axis=1))
    ki_all = pad_s(jnp.concatenate([cache_kidx.astype(MXU_DTYPE), ki3.astype(MXU_DTYPE)], axis=1))
    n_sel = min(TOPK_MAX, s_len // 4)
    mask = _select(qi.reshape(b, t, -1), sm.reshape(b, t, LANES), ki_all.transpose(0, 2, 1),
                   tqs, wk, s_len, past, n_sel)
    bias = bias_tiles[:, :, :tq, :].reshape(2, N_KV_HEADS, Q_PER_KV * tq, tk)
    o = _attention(q.reshape(b, t, -1), k_all.transpose(0, 2, 1), v_all, mask, bias, tq, tk, wide, past)

    out = _post(o.reshape(m, -1), yn, gs, ga, x2d, w_ssd_o, w_attn_o, w_out, w_up, w_down, ln, tm)
    return out.reshape(b, t, D_MODEL), (k3.reshape(b, t, N_KV_HEADS, HEAD_DIM), v3.reshape(b, t, N_KV_HEADS, HEAD_DIM),
                                        ki3, new_ssm, new_conv)


ATTN_TILE = 128
WIDE_TILE = 1024
WIDE_TILE_SHORT = 512
SELECT_ROWS = 128
SELECT_WIDTH = 1024
SSD_CHUNK = 128
ROW_TILE = 256


def kernel(x_prompt, x_sample, cache_k, cache_v, cache_kidx, state_ssm, state_conv, rel_bias, w_in, conv_w, conv_b,
           dt_bias, a_log, d_skip, ssd_norm_w, w_ssd_o, w_attn_o, w_out, ln1_g, ln1_b, w_up, w_down, ln2_g, ln2_b):
    bp = x_prompt.shape[0]
    dtype = x_prompt.dtype
    bias_tiles = _bias_tiles(rel_bias, ATTN_TILE)
    yp, ys = x_prompt, x_sample
    st_p, st_s = [], []
    tiles = dict(tk=ATTN_TILE, wk=SELECT_WIDTH, lc=SSD_CHUNK, tm=ROW_TILE)
    for l in range(DEPTH):
        w_prep = _prep_w_in(w_in[l])
        ln = jnp.stack([ln1_g[l], ln1_b[l], ln2_g[l], ln2_b[l]]).astype(F32)
        weights = (bias_tiles, w_prep, conv_w[l], conv_b[l], dt_bias[l], a_log[l], d_skip[l], ssd_norm_w[l],
                   w_ssd_o[l].astype(MXU_DTYPE), w_attn_o[l].astype(MXU_DTYPE), w_out[l].astype(MXU_DTYPE),
                   w_up[l].astype(MXU_DTYPE), w_down[l].astype(MXU_DTYPE), ln)
        yp, sp = _layer(yp, jnp.zeros((bp, 0, N_KV_HEADS, HEAD_DIM), dtype), jnp.zeros((bp, 0, N_KV_HEADS, HEAD_DIM), dtype),
                        jnp.zeros((bp, 0, IDX_DIM), dtype), jnp.zeros((bp, SSD_HEADS, SSD_HEAD_DIM, D_STATE), dtype),
                        jnp.zeros((bp, _HALO, CONV_DIM), dtype), *weights,
                        tq=min(ATTN_TILE, yp.shape[1]), tqs=min(SELECT_ROWS, yp.shape[1]), wide=WIDE_TILE, **tiles)
        ys, ss = _layer(ys, cache_k[l], cache_v[l], cache_kidx[l], state_ssm[l], state_conv[l], *weights,
                        tq=min(ATTN_TILE, ys.shape[1]), tqs=min(SELECT_ROWS, ys.shape[1]), wide=WIDE_TILE_SHORT, **tiles)
        st_p.append(sp)
        st_s.append(ss)
    stack = lambda sts, i: jnp.stack([s[i] for s in sts])
    return (yp, ys) + tuple(stack(st_p, i) for i in range(5)) + tuple(stack(st_s, i) for i in range(5))
```

```python
import functools

import jax
import jax.numpy as jnp
from jax import lax
from jax.experimental import pallas as pl
from jax.experimental.pallas import tpu as pltpu

F32 = jnp.float32
I32 = jnp.int32
MXU_DTYPE = jnp.bfloat16

D_MODEL = 1024
CHUNK = 64
SSD_INNER = 2048
SSD_HEAD_DIM = 64
SSD_HEADS = 32
SSD_GROUPS = 4
HEADS_PER_GROUP = SSD_HEADS // SSD_GROUPS
D_STATE = 128
CONV_W = 4
CONV_DIM = SSD_INNER + 2 * SSD_GROUPS * D_STATE
N_HEADS = 16
N_KV_HEADS = 4
Q_PER_KV = N_HEADS // N_KV_HEADS
HEAD_DIM = 64
KV_DIM = N_KV_HEADS * HEAD_DIM
IDX_HEADS = 8
IDX_DIM = 64
TOPK_MAX = 256
REL_BUCKETS = 32
REL_MAX_DIST = 128
D_FF = 4 * D_MODEL
DEPTH = 1
ALPHA = (2.0 * DEPTH) ** 0.25
LN_EPS = 1e-5
RMS_EPS = 1e-5
IN_SIZES = (SSD_INNER, CONV_DIM, SSD_HEADS, N_HEADS * HEAD_DIM, KV_DIM, KV_DIM,
            IDX_HEADS * IDX_DIM, IDX_DIM, IDX_HEADS, D_MODEL, D_MODEL)

LANES = 128
VMEM_LIMIT_BYTES = 56 * 1024 * 1024

SM_KI = 0
SM_DT = IDX_DIM
SM_WI = IDX_DIM + SSD_HEADS
SM_PAD = LANES - (IDX_DIM + SSD_HEADS + IDX_HEADS)

NEG_BIG = -1e30
INT_MIN = -2 ** 31
INT_MAX = 2 ** 31 - 1
KEY_NEG_INF = -0x7F800000


def _cparams(n_axes):
    return pltpu.CompilerParams(dimension_semantics=("arbitrary",) * n_axes,
                                vmem_limit_bytes=VMEM_LIMIT_BYTES)


def _resident(shape, index_map):
    return pl.BlockSpec(shape, index_map, pipeline_mode=pl.Buffered(1))


def _per_batch(shape, index_map, steps_per_batch):
    if steps_per_batch > 1:
        return _resident(shape, index_map)
    return pl.BlockSpec(shape, index_map)


def _dot(a, b):
    return jnp.dot(a.astype(MXU_DTYPE), b.astype(MXU_DTYPE), preferred_element_type=F32)


def _dot_exact(a, b):
    return jnp.dot(a, b, preferred_element_type=F32, precision=lax.Precision.HIGHEST)


def _sigmoid(x):
    return 0.5 * jnp.tanh(0.5 * x) + 0.5


def _softplus(x):
    return jnp.maximum(x, 0.0) + jnp.log1p(jnp.exp(-jnp.abs(x)))


_PROJ_WIDTHS = (SSD_INNER, CONV_DIM, N_HEADS * HEAD_DIM, KV_DIM, KV_DIM, IDX_HEADS * IDX_DIM,
                D_MODEL, D_MODEL, LANES)
_PROJ_WIDTH = sum(_PROJ_WIDTHS)
_PROJ_CHUNK = 512


def _prep_w_in(w_in):
    offs = [0]
    for s in IN_SIZES:
        offs.append(offs[-1] + s)
    z, xbc, dt, q, k, v, qi, ki, wi, gs, ga = [w_in[:, offs[i]:offs[i + 1]] for i in range(len(IN_SIZES))]
    small = jnp.concatenate([ki, dt, wi, jnp.zeros((D_MODEL, SM_PAD), w_in.dtype)], axis=1)
    w = jnp.concatenate([z, xbc, q * HEAD_DIM ** -0.5, k, v, qi * IDX_DIM ** -0.5, gs, ga, small], axis=1)
    return w.astype(MXU_DTYPE)


def _inproj_kernel(x_ref, w_ref, z_ref, xbc_ref, q_ref, k_ref, v_ref, qi_ref, gs_ref, ga_ref, sm_ref):
    xb = x_ref[...].astype(MXU_DTYPE)
    outs = (z_ref, xbc_ref, q_ref, k_ref, v_ref, qi_ref, gs_ref, ga_ref, sm_ref)
    c0 = 0
    for ref, width in zip(outs, _PROJ_WIDTHS):
        for o in range(0, width, _PROJ_CHUNK):
            n = min(_PROJ_CHUNK, width - o)
            r = jnp.dot(xb, w_ref[:, c0 + o:c0 + o + n], preferred_element_type=F32)
            ref[:, o:o + n] = r.astype(ref.dtype)
        c0 += width


def _inproj(x2d, w_prep, tm):
    m = x2d.shape[0]
    dtypes = (F32, F32, MXU_DTYPE, F32, F32, MXU_DTYPE, F32, F32, F32)
    out_shape = tuple(jax.ShapeDtypeStruct((m, w), d) for w, d in zip(_PROJ_WIDTHS, dtypes))
    out_specs = tuple(pl.BlockSpec((tm, w), lambda i: (i, 0)) for w in _PROJ_WIDTHS)
    return pl.pallas_call(
        _inproj_kernel,
        grid=(m // tm,),
        in_specs=[pl.BlockSpec((tm, D_MODEL), lambda i: (i, 0)),
                  _resident((D_MODEL, _PROJ_WIDTH), lambda i: (0, 0))],
        out_specs=out_specs,
        out_shape=out_shape,
        compiler_params=_cparams(1),
        name="inproj",
    )(x2d, w_prep)


_HALO = CONV_W - 1
_XP_BASE = 8


def _ssd_kernel(xbc_ref, z_ref, sm_ref, dtT_ref, conv0_ref, ssm0_ref, convw_ref, convb_ref,
                hp_ref, hpT_ref, dfull_ref, nw_ref,
                yn_ref, ssm_out_ref, xp_ref, st_ref, y_ref, *, lc, t_valid):
    c = pl.program_id(1)
    nc = pl.num_programs(1)

    @pl.when(c == 0)
    def _():
        xp_ref[_XP_BASE - _HALO:_XP_BASE, :] = conv0_ref[0]
        st_ref[...] = ssm0_ref[0]

    xp_ref[_XP_BASE:_XP_BASE + lc, :] = xbc_ref[0]
    staged = xp_ref[...]
    conv = convb_ref[...]
    for i in range(CONV_W):
        back = _HALO - i
        shifted = pltpu.roll(staged, back, axis=0) if back else staged
        conv = conv + shifted[_XP_BASE:, :] * convw_ref[i:i + 1, :]
    xp_ref[_XP_BASE - _HALO:_XP_BASE, :] = xp_ref[_XP_BASE + lc - _HALO:_XP_BASE + lc, :]
    act = conv * _sigmoid(conv)
    xs = act[:, :SSD_INNER]
    bm = act[:, SSD_INNER:SSD_INNER + SSD_GROUPS * D_STATE]
    cm = act[:, SSD_INNER + SSD_GROUPS * D_STATE:]

    dt = _softplus(sm_ref[0][:, SM_DT:SM_DT + SSD_HEADS] + hp_ref[0:1, :])
    dt_t = _softplus(dtT_ref[0] + hpT_ref[:, 0:1])
    if t_valid < lc:
        dt = jnp.where(lax.broadcasted_iota(I32, (lc, SSD_HEADS), 0) < t_valid, dt, 0.0)
        dt_t = jnp.where(lax.broadcasted_iota(I32, (SSD_HEADS, lc), 1) < t_valid, dt_t, 0.0)
    a = dt * (-jnp.exp(hp_ref[1:2, :]))
    a_t = dt_t * (-jnp.exp(hpT_ref[:, 1:2]))
    ri = lax.broadcasted_iota(I32, (lc, lc), 0)
    ci = lax.broadcasted_iota(I32, (lc, lc), 1)
    causal = ci <= ri
    acum = _dot_exact(causal.astype(F32), a)
    acum_t = _dot_exact(a_t, (ri <= ci).astype(F32))
    e_a = jnp.exp(acum)
    w_end = jnp.exp(acum[lc - 1:lc, :] - acum) * dt
    lane = lax.broadcasted_iota(I32, (lc, 2 * SSD_HEAD_DIM), 1)
    first_head = lane < SSD_HEAD_DIM
    gw = HEADS_PER_GROUP * SSD_HEAD_DIM

    def per_head_lanes(v, h0):
        return jnp.where(first_head, v[:, h0:h0 + 1], v[:, h0 + 1:h0 + 2])

    for g in range(SSD_GROUPS):
        b_g = bm[:, g * D_STATE:(g + 1) * D_STATE]
        c_g = cm[:, g * D_STATE:(g + 1) * D_STATE].astype(MXU_DTYPE)
        b_gt = b_g.T.astype(MXU_DTYPE)
        cb = jnp.dot(c_g, b_gt, preferred_element_type=F32)
        state = st_ref[g]
        y_off = _dot(c_g, state)
        xw, e_last = [], []
        for jp in range(HEADS_PER_GROUP // 2):
            h0 = g * HEADS_PER_GROUP + 2 * jp
            ms = []
            for h in (h0, h0 + 1):
                seg = acum[:, h:h + 1] - acum_t[h:h + 1, :]
                decay = jnp.exp(jnp.where(causal, seg, -jnp.inf))
                ms.append((cb * decay * dt_t[h:h + 1, :]).astype(MXU_DTYPE))
            c0 = h0 * SSD_HEAD_DIM
            x_pair = xs[:, c0:c0 + 2 * SSD_HEAD_DIM]
            r = jnp.dot(jnp.concatenate(ms, axis=0), x_pair.astype(MXU_DTYPE), preferred_element_type=F32)
            y_diag = jnp.where(first_head, r[:lc], r[lc:])
            o0 = 2 * jp * SSD_HEAD_DIM
            e_pair = per_head_lanes(e_a, h0)
            y_ref[:, c0:c0 + 2 * SSD_HEAD_DIM] = (y_diag + y_off[:, o0:o0 + 2 * SSD_HEAD_DIM] * e_pair
                                                  + dfull_ref[:, c0:c0 + 2 * SSD_HEAD_DIM] * x_pair)
            xw.append((x_pair * per_head_lanes(w_end, h0)).astype(MXU_DTYPE))
            e_last.append(e_pair[lc - 1:lc, :])
        st_ref[g] = (state * jnp.concatenate(e_last, axis=1)
                     + jnp.dot(b_gt, jnp.concatenate(xw, axis=1), preferred_element_type=F32))

    zv = z_ref[0]
    hg = y_ref[...] * (zv * _sigmoid(zv))
    for g in range(SSD_GROUPS):
        sl = hg[:, g * gw:(g + 1) * gw]
        ms = jnp.mean(sl * sl, axis=-1, keepdims=True)
        yn_ref[0, :, g * gw:(g + 1) * gw] = (sl * lax.rsqrt(ms + RMS_EPS) * nw_ref[:, g * gw:(g + 1) * gw]).astype(yn_ref.dtype)

    @pl.when(c == nc - 1)
    def _():
        ssm_out_ref[0] = st_ref[...]


def _ssd(xbc, z, sm, dt_t, conv0, ssm0_t, conv_w, conv_b, dt_bias, a_log, d_skip, norm_w, lc, t_valid):
    b, t, _ = xbc.shape
    nc = t // lc
    hp = jnp.stack([dt_bias, a_log]).astype(F32)
    d_full = jnp.repeat(d_skip.astype(F32), SSD_HEAD_DIM)[None, :]
    gw = HEADS_PER_GROUP * SSD_HEAD_DIM
    const2 = lambda bi, ci: (0, 0)
    kern = functools.partial(_ssd_kernel, lc=lc, t_valid=t_valid)
    return pl.pallas_call(
        kern,
        grid=(b, nc),
        in_specs=[pl.BlockSpec((1, lc, CONV_DIM), lambda bi, ci: (bi, ci, 0)),
                  pl.BlockSpec((1, lc, SSD_INNER), lambda bi, ci: (bi, ci, 0)),
                  pl.BlockSpec((1, lc, LANES), lambda bi, ci: (bi, ci, 0)),
                  pl.BlockSpec((1, SSD_HEADS, lc), lambda bi, ci: (bi, 0, ci)),
                  pl.BlockSpec((1, _HALO, CONV_DIM), lambda bi, ci: (bi, 0, 0)),
                  pl.BlockSpec((1, SSD_GROUPS, D_STATE, gw), lambda bi, ci: (bi, 0, 0, 0)),
                  pl.BlockSpec((CONV_W, CONV_DIM), const2),
                  pl.BlockSpec((1, CONV_DIM), const2),
                  pl.BlockSpec((2, SSD_HEADS), const2),
                  pl.BlockSpec((SSD_HEADS, 2), const2),
                  pl.BlockSpec((1, SSD_INNER), const2),
                  pl.BlockSpec((1, SSD_INNER), const2)],
        out_specs=(pl.BlockSpec((1, lc, SSD_INNER), lambda bi, ci: (bi, ci, 0)),
                   pl.BlockSpec((1, SSD_GROUPS, D_STATE, gw), lambda bi, ci: (bi, 0, 0, 0))),
        out_shape=(jax.ShapeDtypeStruct((b, t, SSD_INNER), MXU_DTYPE),
                   jax.ShapeDtypeStruct((b, SSD_GROUPS, D_STATE, gw), F32)),
        scratch_shapes=[pltpu.VMEM((_XP_BASE + lc, CONV_DIM), F32),
                        pltpu.VMEM((SSD_GROUPS, D_STATE, gw), F32),
                        pltpu.VMEM((lc, SSD_INNER), F32)],
        compiler_params=_cparams(2),
        name="ssd",
    )(xbc, z, sm, dt_t, conv0, ssm0_t, conv_w.astype(F32), conv_b.astype(F32)[None, :], hp, hp.T,
      d_full, norm_w.astype(F32)[None, :])


def _state_to_kernel_layout(s):
    b = s.shape[0]
    s = s.reshape(b, SSD_GROUPS, HEADS_PER_GROUP, SSD_HEAD_DIM, D_STATE)
    return s.transpose(0, 1, 4, 2, 3).reshape(b, SSD_GROUPS, D_STATE, HEADS_PER_GROUP * SSD_HEAD_DIM)


def _state_from_kernel_layout(s):
    b = s.shape[0]
    s = s.reshape(b, SSD_GROUPS, D_STATE, HEADS_PER_GROUP, SSD_HEAD_DIM)
    return s.transpose(0, 1, 3, 4, 2).reshape(b, SSD_HEADS, SSD_HEAD_DIM, D_STATE)


_T5_LOG_STARTS = (12, 16, 23, 32, 46, 64, 91)
_T5_FAR_BUCKET = REL_BUCKETS // 2 - 1


def _bias_kernel(rb_ref, out_ref, *, tk):
    half = REL_BUCKETS // 2
    max_exact = half // 2
    ri = lax.broadcasted_iota(I32, (tk, tk), 0)
    ci = lax.broadcasted_iota(I32, (tk, tk), 1)
    for t in range(2):
        rel = ci - ri - (1 - t) * tk
        n = jnp.abs(rel)
        large = jnp.full((tk, tk), max_exact, I32)
        for start in _T5_LOG_STARTS:
            large = large + (n >= start).astype(I32)
        bucket = jnp.where(rel > 0, half, 0) + jnp.where(n < max_exact, n, large)
        for h in range(N_HEADS):
            acc = jnp.zeros((tk, tk), F32)
            for bkt in range(REL_BUCKETS):
                acc = jnp.where(bucket == bkt, rb_ref[bkt, h], acc)
            out_ref[t, h] = acc - rb_ref[_T5_FAR_BUCKET, h]


def _bias_tiles(rel_bias, tk):
    return pl.pallas_call(
        functools.partial(_bias_kernel, tk=tk),
        in_specs=[pl.BlockSpec(memory_space=pltpu.SMEM)],
        out_specs=pl.BlockSpec(memory_space=pltpu.VMEM),
        out_shape=jax.ShapeDtypeStruct((2, N_HEADS, tk, tk), F32),
        name="bias_tiles",
    )(rel_bias.astype(F32))


_MAX_BISECTIONS = 40
_SCORE_COLS = 256


def _select_kernel(qi_ref, sm_ref, kit_ref, mask_ref, key_ref, w_ref, *, tq, wk, s_len, q_start, n_sel):
    qb = pl.program_id(1)
    n_blocks = mask_ref.shape[2] // wk
    nt = wk // LANES
    p0 = q_start + qb * tq
    pos = p0 + lax.broadcasted_iota(I32, (tq, 1), 0)
    vis_end = (pos // CHUNK + 1) * CHUNK
    vis_max = jnp.minimum(((p0 + tq - 1) // CHUNK + 1) * CHUNK, s_len)
    nkb = (vis_max + wk - 1) // wk

    wi = sm_ref[0][:, SM_WI:SM_WI + IDX_HEADS] * IDX_HEADS ** -0.5
    for hh in range(IDX_HEADS):
        w_ref[hh] = jnp.broadcast_to(wi[:, hh:hh + 1], (tq, LANES))
    lane_k = lax.broadcasted_iota(I32, (1, LANES), 1)

    def to_key(score):
        bits = lax.bitcast_convert_type(score, I32)
        return jnp.where(bits < 0, INT_MIN - bits, bits)

    def score_block(j, top, masked):
        m1, m2 = top
        k0 = pl.multiple_of(j * wk, wk)
        for c in range(wk // _SCORE_COLS):
            c0 = k0 + c * _SCORE_COLS
            kt = kit_ref[0, :, pl.ds(c0, _SCORE_COLS)]
            d = jnp.maximum(jnp.dot(qi_ref[0, 0], kt, preferred_element_type=F32), 0.0)
            for u in range(_SCORE_COLS // LANES):
                sc = jnp.zeros((tq, LANES), F32)
                for hh in range(IDX_HEADS):
                    sc = sc + w_ref[hh] * d[hh * tq:(hh + 1) * tq, u * LANES:(u + 1) * LANES]
                if masked:
                    key_pos = c0 + u * LANES + lane_k
                    sc = jnp.where((key_pos < vis_end) & (key_pos < s_len), sc, -jnp.inf)
                key_ref[:, pl.ds(c0 + u * LANES, LANES)] = to_key(sc)
                m2 = jnp.maximum(m2, jnp.minimum(m1, sc))
                m1 = jnp.maximum(m1, sc)
        return m1, m2

    n_open = jnp.minimum(p0, s_len) // wk
    top = (jnp.full((tq, LANES), -jnp.inf, F32),) * 2
    top = lax.fori_loop(0, n_open, functools.partial(score_block, masked=False), top)
    m1, m2 = lax.fori_loop(n_open, nkb, functools.partial(score_block, masked=True), top)

    def count_ge(thr):
        thr_b = jnp.broadcast_to(thr, (tq, LANES))

        def body(j, acc):
            k0 = pl.multiple_of(j * wk, wk)
            for u in range(nt):
                acc = acc + jnp.where(key_ref[:, pl.ds(k0 + u * LANES, LANES)] >= thr_b, 1.0, 0.0)
            return acc

        acc = lax.fori_loop(0, nkb, body, jnp.zeros((tq, LANES), F32))
        return jnp.sum(acc, axis=1, keepdims=True).astype(I32)

    def unfinished(lo, hi, cnt_lo):
        return (cnt_lo > n_sel) & (hi - 1 > lo)

    def bisect(lo, hi, cnt_lo):
        def cond(st):
            return (st[0] < _MAX_BISECTIONS) & (st[1] > 0)

        def body(st):
            it, _, lo, hi, cnt_lo = st
            live = unfinished(lo, hi, cnt_lo)
            mid = (lo >> 1) + (hi >> 1) + (lo & hi & 1)
            cnt = count_ge(jnp.where(live, mid, lo))
            up = live & (cnt >= n_sel)
            lo = jnp.where(up, mid, lo)
            cnt_lo = jnp.where(up, cnt, cnt_lo)
            hi = jnp.where(live & (cnt < n_sel), mid, hi)
            return (it + 1, jnp.max(unfinished(lo, hi, cnt_lo).astype(I32)), lo, hi, cnt_lo)

        st0 = (jnp.int32(0), jnp.max(unfinished(lo, hi, cnt_lo).astype(I32)), lo, hi, cnt_lo)
        return lax.while_loop(cond, body, st0)[2:]

    lo = jnp.maximum(to_key(jnp.min(m2, axis=1, keepdims=True)), KEY_NEG_INF + 1)
    hi = to_key(jnp.max(m1, axis=1, keepdims=True)) + 1
    tau, _, cnt_tau = bisect(lo, hi, count_ge(lo))

    tie_rows = cnt_tau > n_sel
    has_ties = jnp.max(tie_rows.astype(I32)) > 0

    @pl.when(jnp.logical_not(has_ties))
    def _():
        tau_b = jnp.broadcast_to(tau, (tq, LANES))

        def body(j, carry):
            k0 = pl.multiple_of(j * wk, wk)
            for u in range(nt):
                sl = pl.ds(k0 + u * LANES, LANES)
                mask_ref[0, :, sl] = (key_ref[:, sl] >= tau_b).astype(I32).astype(mask_ref.dtype)
            return carry
        lax.fori_loop(0, nkb, body, 0)

    @pl.when(has_ties)
    def _():
        upper = (lax.broadcasted_iota(I32, (wk, wk), 0) <= lax.broadcasted_iota(I32, (wk, wk), 1)).astype(MXU_DTYPE)
        need = jnp.where(tie_rows, n_sel - count_ge(tau + 1), wk * n_blocks).astype(F32)

        def body(j, seen):
            k0 = pl.multiple_of(j * wk, wk)
            blk = key_ref[:, pl.ds(k0, wk)]
            eq = blk == tau
            rank = seen + jnp.dot(eq.astype(F32).astype(MXU_DTYPE), upper, preferred_element_type=F32)
            sel = (blk > tau) | (eq & (rank <= need))
            mask_ref[0, :, pl.ds(k0, wk)] = sel.astype(I32).astype(mask_ref.dtype)
            return seen + jnp.sum(eq.astype(F32), axis=1, keepdims=True)
        lax.fori_loop(0, nkb, body, jnp.zeros((tq, 1), F32))

    def zero_body(j, carry):
        k0 = pl.multiple_of(j * wk, wk)
        mask_ref[0, :, pl.ds(k0, wk)] = jnp.zeros((tq, wk), mask_ref.dtype)
        return carry

    lax.fori_loop(nkb, n_blocks, zero_body, 0)


def _select(qi, sm, kit, tq, wk, s_len, q_start, n_sel):
    b, t, _ = qi.shape
    s_pad = kit.shape[2]
    assert n_sel <= 2 * LANES, "the starting bracket takes two candidates per lane column"
    kern = functools.partial(_select_kernel, tq=tq, wk=wk, s_len=s_len, q_start=q_start, n_sel=n_sel)
    qi = qi.reshape(b, t // tq, tq, IDX_HEADS, IDX_DIM).transpose(0, 1, 3, 2, 4).reshape(b, t // tq, IDX_HEADS * tq, IDX_DIM)
    return pl.pallas_call(
        kern,
        grid=(b, t // tq),
        in_specs=[pl.BlockSpec((1, 1, IDX_HEADS * tq, IDX_DIM), lambda bi, qi_: (bi, qi_, 0, 0)),
                  pl.BlockSpec((1, tq, LANES), lambda bi, qi_: (bi, qi_, 0)),
                  _per_batch((1, IDX_DIM, s_pad), lambda bi, qi_: (bi, 0, 0), t // tq)],
        out_specs=pl.BlockSpec((1, tq, s_pad), lambda bi, qi_: (bi, qi_, 0)),
        out_shape=jax.ShapeDtypeStruct((b, t, s_pad), jnp.int8),
        scratch_shapes=[pltpu.VMEM((tq, s_pad), I32),
                        pltpu.VMEM((IDX_HEADS, tq, LANES), F32)],
        compiler_params=_cparams(2),
        name="select",
    )(qi, sm, kit)


_V_AUG = 2 * HEAD_DIM
_V_ROWS = 512
_MID_TILE = 512


def _attn_kernel(q_ref, kt_ref, v_ref, mask_ref, bias_ref, o_ref, m_ref, acc_ref, qg_ref, vaug_ref,
                 *, tq, tk, far_widths, q_start):
    qb = pl.program_id(1)
    jd = (q_start + qb * tq) // tk

    @pl.when(qb == 0)
    def _():
        one_hot = (lax.broadcasted_iota(I32, (_V_ROWS, _V_AUG - HEAD_DIM), 1) == 0).astype(vaug_ref.dtype)

        def body(i, carry):
            r0 = pl.multiple_of(i * _V_ROWS, _V_ROWS)
            blk = v_ref[0, pl.ds(r0, _V_ROWS), :]
            for g in range(N_KV_HEADS):
                vaug_ref[pl.ds(r0, _V_ROWS), g * _V_AUG:g * _V_AUG + HEAD_DIM] = blk[:, g * HEAD_DIM:(g + 1) * HEAD_DIM]
                vaug_ref[pl.ds(r0, _V_ROWS), g * _V_AUG + HEAD_DIM:(g + 1) * _V_AUG] = one_hot
            return carry

        lax.fori_loop(0, vaug_ref.shape[0] // _V_ROWS, body, 0)

    for g in range(N_KV_HEADS):
        for r in range(Q_PER_KV):
            c0 = (g * Q_PER_KV + r) * HEAD_DIM
            qg_ref[g, r * tq:(r + 1) * tq, :] = q_ref[0, :, c0:c0 + HEAD_DIM]
    m_ref[...] = jnp.full(m_ref.shape, NEG_BIG, F32)
    acc_ref[...] = jnp.zeros(acc_ref.shape, F32)

    def process(k0, width, near):
        nt = width // LANES
        mb = jnp.where(mask_ref[0, :, pl.ds(k0, width)].astype(I32) != 0, 0.0, NEG_BIG)
        mb = jnp.concatenate([mb] * Q_PER_KV, axis=0)
        for g in range(N_KV_HEADS):
            kt = kt_ref[0, g * HEAD_DIM:(g + 1) * HEAD_DIM, pl.ds(k0, width)]
            s = jnp.dot(qg_ref[g], kt, preferred_element_type=F32) + mb
            if near is not None:
                s = s + bias_ref[near, g]
            tiles = [s[:, u * LANES:(u + 1) * LANES] for u in range(nt)]
            mx = tiles[0]
            for u in range(1, nt):
                mx = jnp.maximum(mx, tiles[u])
            m_old = m_ref[g]
            m_new = jnp.maximum(m_old, jnp.max(mx, axis=1, keepdims=True))
            alpha = jnp.exp(m_old - m_new)
            p = jnp.concatenate([jnp.exp(tl - m_new).astype(MXU_DTYPE) for tl in tiles], axis=1)
            vt = vaug_ref[pl.ds(k0, width), g * _V_AUG:(g + 1) * _V_AUG]
            acc_ref[g] = alpha * acc_ref[g] + jnp.dot(p, vt, preferred_element_type=F32)
            m_ref[g] = m_new

    far_keys = jnp.maximum(jd - 1, 0) * tk
    done = 0
    for width in far_widths:
        steps = (far_keys - done) // width

        def far_body(i, carry, width=width, done=done):
            process(pl.multiple_of(done + i * width, width), width, None)
            return carry

        lax.fori_loop(0, steps, far_body, 0)
        done = done + steps * width

    @pl.when(jd >= 1)
    def _():
        process(pl.multiple_of((jd - 1) * tk, tk), tk, 0)

    process(pl.multiple_of(jd * tk, tk), tk, 1)

    for g in range(N_KV_HEADS):
        acc = acc_ref[g]
        o = acc[:, :HEAD_DIM] / acc[:, HEAD_DIM:HEAD_DIM + 1]
        for r in range(Q_PER_KV):
            c0 = (g * Q_PER_KV + r) * HEAD_DIM
            o_ref[0, :, c0:c0 + HEAD_DIM] = o[r * tq:(r + 1) * tq].astype(o_ref.dtype)


def _attention(q, kt, v, mask, bias, tq, tk, wide, q_start):
    b, t, _ = q.shape
    s_pad = kt.shape[2]
    assert s_pad % _V_ROWS == 0
    rows = Q_PER_KV * tq
    far_widths = tuple(sorted({wide, min(wide, _MID_TILE), tk}, reverse=True))
    assert all(a % b == 0 for a, b in zip(far_widths, far_widths[1:]))
    kern = functools.partial(_attn_kernel, tq=tq, tk=tk, far_widths=far_widths, q_start=q_start)
    return pl.pallas_call(
        kern,
        grid=(b, t // tq),
        in_specs=[pl.BlockSpec((1, tq, N_HEADS * HEAD_DIM), lambda bi, qi_: (bi, qi_, 0)),
                  _per_batch((1, KV_DIM, s_pad), lambda bi, qi_: (bi, 0, 0), t // tq),
                  _per_batch((1, s_pad, KV_DIM), lambda bi, qi_: (bi, 0, 0), t // tq),
                  pl.BlockSpec((1, tq, s_pad), lambda bi, qi_: (bi, qi_, 0)),
                  _resident((2, N_KV_HEADS, rows, tk), lambda bi, qi_: (0, 0, 0, 0))],
        out_specs=pl.BlockSpec((1, tq, N_HEADS * HEAD_DIM), lambda bi, qi_: (bi, qi_, 0)),
        out_shape=jax.ShapeDtypeStruct((b, t, N_HEADS * HEAD_DIM), MXU_DTYPE),
        scratch_shapes=[pltpu.VMEM((N_KV_HEADS, rows, LANES), F32),
                        pltpu.VMEM((N_KV_HEADS, rows, _V_AUG), F32),
                        pltpu.VMEM((N_KV_HEADS, rows, HEAD_DIM), MXU_DTYPE),
                        pltpu.VMEM((s_pad, N_KV_HEADS * _V_AUG), MXU_DTYPE)],
        compiler_params=_cparams(2),
        name="attention",
    )(q, kt, v, mask, bias)


def _layer_norm(x, g, b):
    mu = jnp.mean(x, axis=-1, keepdims=True)
    xc = x - mu
    var = jnp.mean(xc * xc, axis=-1, keepdims=True)
    return xc * lax.rsqrt(var + LN_EPS) * g + b


def _post_kernel(o_ref, yn_ref, gs_ref, ga_ref, x_ref, wso_ref, wao_ref, wout_ref, wup_ref, wdown_ref,
                 ln_ref, out_ref):
    y1 = jnp.dot(yn_ref[...], wso_ref[...], preferred_element_type=F32)
    y2 = jnp.dot(o_ref[...], wao_ref[...], preferred_element_type=F32)
    mix = _sigmoid(gs_ref[...]) * y1 + _sigmoid(ga_ref[...]) * y2
    mixed = _dot(mix, wout_ref[...])
    h = _layer_norm(ALPHA * x_ref[...] + mixed, ln_ref[0:1, :], ln_ref[1:2, :])
    u = jnp.maximum(_dot(h, wup_ref[...]), 0.0)
    f = _dot(u * u, wdown_ref[...])
    out_ref[...] = _layer_norm(ALPHA * h + f, ln_ref[2:3, :], ln_ref[3:4, :])


def _post(o, yn, gs, ga, x2d, w_ssd_o, w_attn_o, w_out, w_up, w_down, ln, tm):
    m = x2d.shape[0]
    row = lambda w: pl.BlockSpec((tm, w), lambda i: (i, 0))
    const = lambda i: (0, 0)
    return pl.pallas_call(
        _post_kernel,
        grid=(m // tm,),
        in_specs=[row(N_HEADS * HEAD_DIM), row(SSD_INNER), row(D_MODEL), row(D_MODEL), row(D_MODEL),
                  _resident((SSD_INNER, D_MODEL), const),
                  _resident((N_HEADS * HEAD_DIM, D_MODEL), const),
                  _resident((D_MODEL, D_MODEL), const),
                  _resident((D_MODEL, D_FF), const),
                  _resident((D_FF, D_MODEL), const),
                  _resident((4, D_MODEL), const)],
        out_specs=row(D_MODEL),
        out_shape=jax.ShapeDtypeStruct((m, D_MODEL), F32),
        compiler_params=_cparams(1),
        name="post",
    )(o, yn, gs, ga, x2d, w_ssd_o, w_attn_o, w_out, w_up, w_down, ln)


def _round_up(n, m):
    return (n + m - 1) // m * m


def _layer(x, cache_k, cache_v, cache_kidx, state_ssm, state_conv, bias_tiles, w_prep, conv_w, conv_b,
           dt_bias, a_log, d_skip, ssd_norm_w, w_ssd_o, w_attn_o, w_out, w_up, w_down, ln,
           *, tq, tk, wide, tqs, wk, lc, tm):
    b, t, _ = x.shape
    past = cache_k.shape[1]
    s_len = past + t
    m = b * t
    assert m % tm == 0 and t % tq == 0 and past % tk == 0 and tk % LANES == 0 and wide % tk == 0
    assert t % tqs == 0 and wk % LANES == 0
    assert tq == tk or t == tq, "every query tile must start on a key-tile boundary"
    assert t % lc == 0 or t < lc
    x2d = x.reshape(m, D_MODEL)

    z, xbc, q, k, v, qi, gs, ga, sm = _inproj(x2d, w_prep, tm)

    t_pad = _round_up(t, lc)
    pad_t = lambda a: jnp.pad(a.reshape(b, t, a.shape[-1]), ((0, 0), (0, t_pad - t), (0, 0)))
    sm3 = pad_t(sm)
    dt_t = sm3[:, :, SM_DT:SM_DT + SSD_HEADS].transpose(0, 2, 1)
    yn, ssm_new = _ssd(pad_t(xbc), pad_t(z), sm3, dt_t, state_conv.astype(F32),
                       _state_to_kernel_layout(state_ssm.astype(F32)), conv_w, conv_b, dt_bias, a_log, d_skip,
                       ssd_norm_w, lc, min(t, lc))
    yn = yn[:, :t].reshape(m, SSD_INNER)
    new_ssm = _state_from_kernel_layout(ssm_new)
    xbc3 = xbc.reshape(b, t, CONV_DIM)
    new_conv = jnp.concatenate([state_conv.astype(F32), xbc3], axis=1)[:, -_HALO:]

    k3 = k.reshape(b, t, KV_DIM)
    v3 = v.reshape(b, t, KV_DIM)
    ki3 = sm.reshape(b, t, LANES)[:, :, SM_KI:SM_KI + IDX_DIM]
    s_pad = _round_up(s_len, max(wide, wk))
    assert s_pad % wide == 0 and s_pad % wk == 0
    pad_s = lambda a: jnp.pad(a, ((0, 0), (0, s_pad - s_len)) + ((0, 0),) * (a.ndim - 2))
    k_all = pad_s(jnp.concatenate([cache_k.reshape(b, past, KV_DIM).astype(MXU_DTYPE), k3.astype(MXU_DTYPE)], axis=1))
    v_all = pad_s(jnp.concatenate([cache_v.reshape(b, past, KV_DIM).astype(MXU_DTYPE), v3.astype(MXU_DTYPE)], axis=1))
    ki_all = pad_s(jnp.concatenate([cache_kidx.astype(MXU_DTYPE), ki3.astype(MXU_DTYPE)], axis=1))
    n_sel = min(TOPK_MAX, s_len // 4)
    mask = _select(qi.reshape(b, t, -1), sm.reshape(b, t, LANES), ki_all.transpose(0, 2, 1),
                   tqs, wk, s_len, past, n_sel)
    bias = bias_tiles[:, :, :tq, :].reshape(2, N_KV_HEADS, Q_PER_KV * tq, tk)
    o = _attention(q.reshape(b, t, -1), k_all.transpose(0, 2, 1), v_all, mask, bias, tq, tk, wide, past)

    out = _post(o.reshape(m, -1), yn, gs, ga, x2d, w_ssd_o, w_attn_o, w_out, w_up, w_down, ln, tm)
    return out.reshape(b, t, D_MODEL), (k3.reshape(b, t, N_KV_HEADS, HEAD_DIM), v3.reshape(b, t, N_KV_HEADS, HEAD_DIM),
                                        ki3, new_ssm, new_conv)


ATTN_TILE = 128
WIDE_TILE = 1024
WIDE_TILE_SHORT = 512
SELECT_ROWS = 128
SELECT_WIDTH = 1024
SSD_CHUNK = 128
ROW_TILE = 256


def kernel(x_prompt, x_sample, cache_k, cache_v, cache_kidx, state_ssm, state_conv, rel_bias, w_in, conv_w, conv_b,
           dt_bias, a_log, d_skip, ssd_norm_w, w_ssd_o, w_attn_o, w_out, ln1_g, ln1_b, w_up, w_down, ln2_g, ln2_b):
    bp = x_prompt.shape[0]
    dtype = x_prompt.dtype
    bias_tiles = _bias_tiles(rel_bias, ATTN_TILE)
    yp, ys = x_prompt, x_sample
    st_p, st_s = [], []
    tiles = dict(tk=ATTN_TILE, wk=SELECT_WIDTH, lc=SSD_CHUNK, tm=ROW_TILE)
    for l in range(DEPTH):
        w_prep = _prep_w_in(w_in[l])
        ln = jnp.stack([ln1_g[l], ln1_b[l], ln2_g[l], ln2_b[l]]).astype(F32)
        weights = (bias_tiles, w_prep, conv_w[l], conv_b[l], dt_bias[l], a_log[l], d_skip[l], ssd_norm_w[l],
                   w_ssd_o[l].astype(MXU_DTYPE), w_attn_o[l].astype(MXU_DTYPE), w_out[l].astype(MXU_DTYPE),
                   w_up[l].astype(MXU_DTYPE), w_down[l].astype(MXU_DTYPE), ln)
        yp, sp = _layer(yp, jnp.zeros((bp, 0, N_KV_HEADS, HEAD_DIM), dtype), jnp.zeros((bp, 0, N_KV_HEADS, HEAD_DIM), dtype),
                        jnp.zeros((bp, 0, IDX_DIM), dtype), jnp.zeros((bp, SSD_HEADS, SSD_HEAD_DIM, D_STATE), dtype),
                        jnp.zeros((bp, _HALO, CONV_DIM), dtype), *weights,
                        tq=min(ATTN_TILE, yp.shape[1]), tqs=min(SELECT_ROWS, yp.shape[1]), wide=WIDE_TILE, **tiles)
        ys, ss = _layer(ys, cache_k[l], cache_v[l], cache_kidx[l], state_ssm[l], state_conv[l], *weights,
                        tq=min(ATTN_TILE, ys.shape[1]), tqs=min(SELECT_ROWS, ys.shape[1]), wide=WIDE_TILE_SHORT, **tiles)
        st_p.append(sp)
        st_s.append(ss)
    stack = lambda sts, i: jnp.stack([s[i] for s in sts])
    return (yp, ys) + tuple(stack(st_p, i) for i in range(5)) + tuple(stack(st_s, i) for i in range(5))
```

```python
import functools

import jax
import jax.numpy as jnp
from jax import lax
from jax.experimental import pallas as pl
from jax.experimental.pallas import tpu as pltpu

F32 = jnp.float32
I32 = jnp.int32
MXU_DTYPE = jnp.bfloat16

D_MODEL = 1024
CHUNK = 64
SSD_INNER = 2048
SSD_HEAD_DIM = 64
SSD_HEADS = 32
SSD_GROUPS = 4
HEADS_PER_GROUP = SSD_HEADS // SSD_GROUPS
D_STATE = 128
CONV_W = 4
CONV_DIM = SSD_INNER + 2 * SSD_GROUPS * D_STATE
N_HEADS = 16
N_KV_HEADS = 4
Q_PER_KV = N_HEADS // N_KV_HEADS
HEAD_DIM = 64
KV_DIM = N_KV_HEADS * HEAD_DIM
IDX_HEADS = 8
IDX_DIM = 64
TOPK_MAX = 256
REL_BUCKETS = 32
REL_MAX_DIST = 128
D_FF = 4 * D_MODEL
DEPTH = 1
ALPHA = (2.0 * DEPTH) ** 0.25
LN_EPS = 1e-5
RMS_EPS = 1e-5
IN_SIZES = (SSD_INNER, CONV_DIM, SSD_HEADS, N_HEADS * HEAD_DIM, KV_DIM, KV_DIM,
            IDX_HEADS * IDX_DIM, IDX_DIM, IDX_HEADS, D_MODEL, D_MODEL)

LANES = 128
VMEM_LIMIT_BYTES = 56 * 1024 * 1024

SM_KI = 0
SM_DT = IDX_DIM
SM_WI = IDX_DIM + SSD_HEADS
SM_PAD = LANES - (IDX_DIM + SSD_HEADS + IDX_HEADS)

NEG_BIG = -1e30
INT_MIN = -2 ** 31
INT_MAX = 2 ** 31 - 1
KEY_NEG_INF = -0x7F800000


def _cparams(n_axes):
    return pltpu.CompilerParams(dimension_semantics=("arbitrary",) * n_axes,
                                vmem_limit_bytes=VMEM_LIMIT_BYTES)


def _resident(shape, index_map):
    return pl.BlockSpec(shape, index_map, pipeline_mode=pl.Buffered(1))


def _per_batch(shape, index_map, steps_per_batch):
    if steps_per_batch > 1:
        return _resident(shape, index_map)
    return pl.BlockSpec(shape, index_map)


def _dot(a, b):
    return jnp.dot(a.astype(MXU_DTYPE), b.astype(MXU_DTYPE), preferred_element_type=F32)


def _dot_exact(a, b):
    return jnp.dot(a, b, preferred_element_type=F32, precision=lax.Precision.HIGHEST)


def _sigmoid(x):
    return 0.5 * jnp.tanh(0.5 * x) + 0.5


def _softplus(x):
    return jnp.maximum(x, 0.0) + jnp.log1p(jnp.exp(-jnp.abs(x)))


_PROJ_WIDTHS = (SSD_INNER, CONV_DIM, N_HEADS * HEAD_DIM, KV_DIM, KV_DIM, IDX_HEADS * IDX_DIM,
                D_MODEL, D_MODEL, LANES)
_PROJ_WIDTH = sum(_PROJ_WIDTHS)
_PROJ_CHUNK = 512


def _prep_w_in(w_in):
    offs = [0]
    for s in IN_SIZES:
        offs.append(offs[-1] + s)
    z, xbc, dt, q, k, v, qi, ki, wi, gs, ga = [w_in[:, offs[i]:offs[i + 1]] for i in range(len(IN_SIZES))]
    small = jnp.concatenate([ki, dt, wi, jnp.zeros((D_MODEL, SM_PAD), w_in.dtype)], axis=1)
    w = jnp.concatenate([z, xbc, q * HEAD_DIM ** -0.5, k, v, qi * IDX_DIM ** -0.5, gs, ga, small], axis=1)
    return w.astype(MXU_DTYPE)


def _inproj_kernel(x_ref, w_ref, z_ref, xbc_ref, q_ref, k_ref, v_ref, qi_ref, gs_ref, ga_ref, sm_ref):
    xb = x_ref[...].astype(MXU_DTYPE)
    outs = (z_ref, xbc_ref, q_ref, k_ref, v_ref, qi_ref, gs_ref, ga_ref, sm_ref)
    c0 = 0
    for ref, width in zip(outs, _PROJ_WIDTHS):
        for o in range(0, width, _PROJ_CHUNK):
            n = min(_PROJ_CHUNK, width - o)
            r = jnp.dot(xb, w_ref[:, c0 + o:c0 + o + n], preferred_element_type=F32)
            ref[:, o:o + n] = r.astype(ref.dtype)
        c0 += width


def _inproj(x2d, w_prep, tm):
    m = x2d.shape[0]
    dtypes = (F32, F32, MXU_DTYPE, F32, F32, MXU_DTYPE, F32, F32, F32)
    out_shape = tuple(jax.ShapeDtypeStruct((m, w), d) for w, d in zip(_PROJ_WIDTHS, dtypes))
    out_specs = tuple(pl.BlockSpec((tm, w), lambda i: (i, 0)) for w in _PROJ_WIDTHS)
    return pl.pallas_call(
        _inproj_kernel,
        grid=(m // tm,),
        in_specs=[pl.BlockSpec((tm, D_MODEL), lambda i: (i, 0)),
                  _resident((D_MODEL, _PROJ_WIDTH), lambda i: (0, 0))],
        out_specs=out_specs,
        out_shape=out_shape,
        compiler_params=_cparams(1),
        name="inproj",
    )(x2d, w_prep)


_HALO = CONV_W - 1
_XP_BASE = 8


def _ssd_kernel(xbc_ref, z_ref, sm_ref, dtT_ref, conv0_ref, ssm0_ref, convw_ref, convb_ref,
                hp_ref, hpT_ref, dfull_ref, nw_ref,
                yn_ref, ssm_out_ref, xp_ref, st_ref, y_ref, *, lc, t_valid):
    c = pl.program_id(1)
    nc = pl.num_programs(1)

    @pl.when(c == 0)
    def _():
        xp_ref[_XP_BASE - _HALO:_XP_BASE, :] = conv0_ref[0]
        st_ref[...] = ssm0_ref[0]

    xp_ref[_XP_BASE:_XP_BASE + lc, :] = xbc_ref[0]
    staged = xp_ref[...]
    conv = convb_ref[...]
    for i in range(CONV_W):
        back = _HALO - i
        shifted = pltpu.roll(staged, back, axis=0) if back else staged
        conv = conv + shifted[_XP_BASE:, :] * convw_ref[i:i + 1, :]
    xp_ref[_XP_BASE - _HALO:_XP_BASE, :] = xp_ref[_XP_BASE + lc - _HALO:_XP_BASE + lc, :]
    act = conv * _sigmoid(conv)
    xs = act[:, :SSD_INNER]
    bm = act[:, SSD_INNER:SSD_INNER + SSD_GROUPS * D_STATE]
    cm = act[:, SSD_INNER + SSD_GROUPS * D_STATE:]

    dt = _softplus(sm_ref[0][:, SM_DT:SM_DT + SSD_HEADS] + hp_ref[0:1, :])
    dt_t = _softplus(dtT_ref[0] + hpT_ref[:, 0:1])
    if t_valid < lc:
        dt = jnp.where(lax.broadcasted_iota(I32, (lc, SSD_HEADS), 0) < t_valid, dt, 0.0)
        dt_t = jnp.where(lax.broadcasted_iota(I32, (SSD_HEADS, lc), 1) < t_valid, dt_t, 0.0)
    a = dt * (-jnp.exp(hp_ref[1:2, :]))
    a_t = dt_t * (-jnp.exp(hpT_ref[:, 1:2]))
    ri = lax.broadcasted_iota(I32, (lc, lc), 0)
    ci = lax.broadcasted_iota(I32, (lc, lc), 1)
    causal = ci <= ri
    acum = _dot_exact(causal.astype(F32), a)
    acum_t = _dot_exact(a_t, (ri <= ci).astype(F32))
    e_a = jnp.exp(acum)
    w_end = jnp.exp(acum[lc - 1:lc, :] - acum) * dt
    lane = lax.broadcasted_iota(I32, (lc, 2 * SSD_HEAD_DIM), 1)
    first_head = lane < SSD_HEAD_DIM
    gw = HEADS_PER_GROUP * SSD_HEAD_DIM

    def per_head_lanes(v, h0):
        return jnp.where(first_head, v[:, h0:h0 + 1], v[:, h0 + 1:h0 + 2])

    for g in range(SSD_GROUPS):
        b_g = bm[:, g * D_STATE:(g + 1) * D_STATE]
        c_g = cm[:, g * D_STATE:(g + 1) * D_STATE].astype(MXU_DTYPE)
        b_gt = b_g.T.astype(MXU_DTYPE)
        cb = jnp.dot(c_g, b_gt, preferred_element_type=F32)
        state = st_ref[g]
        y_off = _dot(c_g, state)
        xw, e_last = [], []
        for jp in range(HEADS_PER_GROUP // 2):
            h0 = g * HEADS_PER_GROUP + 2 * jp
            ms = []
            for h in (h0, h0 + 1):
                seg = acum[:, h:h + 1] - acum_t[h:h + 1, :]
                decay = jnp.exp(jnp.where(causal, seg, -jnp.inf))
                ms.append((cb * decay * dt_t[h:h + 1, :]).astype(MXU_DTYPE))
            c0 = h0 * SSD_HEAD_DIM
            x_pair = xs[:, c0:c0 + 2 * SSD_HEAD_DIM]
            r = jnp.dot(jnp.concatenate(ms, axis=0), x_pair.astype(MXU_DTYPE), preferred_element_type=F32)
            y_diag = jnp.where(first_head, r[:lc], r[lc:])
            o0 = 2 * jp * SSD_HEAD_DIM
            e_pair = per_head_lanes(e_a, h0)
            y_ref[:, c0:c0 + 2 * SSD_HEAD_DIM] = (y_diag + y_off[:, o0:o0 + 2 * SSD_HEAD_DIM] * e_pair
                                                  + dfull_ref[:, c0:c0 + 2 * SSD_HEAD_DIM] * x_pair)
            xw.append((x_pair * per_head_lanes(w_end, h0)).astype(MXU_DTYPE))
            e_last.append(e_pair[lc - 1:lc, :])
        st_ref[g] = (state * jnp.concatenate(e_last, axis=1)
                     + jnp.dot(b_gt, jnp.concatenate(xw, axis=1), preferred_element_type=F32))

    zv = z_ref[0]
    hg = y_ref[...] * (zv * _sigmoid(zv))
    for g in range(SSD_GROUPS):
        sl = hg[:, g * gw:(g + 1) * gw]
        ms = jnp.mean(sl * sl, axis=-1, keepdims=True)
        yn_ref[0, :, g * gw:(g + 1) * gw] = (sl * lax.rsqrt(ms + RMS_EPS) * nw_ref[:, g * gw:(g + 1) * gw]).astype(yn_ref.dtype)

    @pl.when(c == nc - 1)
    def _():
        ssm_out_ref[0] = st_ref[...]


def _ssd(xbc, z, sm, dt_t, conv0, ssm0_t, conv_w, conv_b, dt_bias, a_log, d_skip, norm_w, lc, t_valid):
    b, t, _ = xbc.shape
    nc = t // lc
    hp = jnp.stack([dt_bias, a_log]).astype(F32)
    d_full = jnp.repeat(d_skip.astype(F32), SSD_HEAD_DIM)[None, :]
    gw = HEADS_PER_GROUP * SSD_HEAD_DIM
    const2 = lambda bi, ci: (0, 0)
    kern = functools.partial(_ssd_kernel, lc=lc, t_valid=t_valid)
    return pl.pallas_call(
        kern,
        grid=(b, nc),
        in_specs=[pl.BlockSpec((1, lc, CONV_DIM), lambda bi, ci: (bi, ci, 0)),
                  pl.BlockSpec((1, lc, SSD_INNER), lambda bi, ci: (bi, ci, 0)),
                  pl.BlockSpec((1, lc, LANES), lambda bi, ci: (bi, ci, 0)),
                  pl.BlockSpec((1, SSD_HEADS, lc), lambda bi, ci: (bi, 0, ci)),
                  pl.BlockSpec((1, _HALO, CONV_DIM), lambda bi, ci: (bi, 0, 0)),
                  pl.BlockSpec((1, SSD_GROUPS, D_STATE, gw), lambda bi, ci: (bi, 0, 0, 0)),
                  pl.BlockSpec((CONV_W, CONV_DIM), const2),
                  pl.BlockSpec((1, CONV_DIM), const2),
                  pl.BlockSpec((2, SSD_HEADS), const2),
                  pl.BlockSpec((SSD_HEADS, 2), const2),
                  pl.BlockSpec((1, SSD_INNER), const2),
                  pl.BlockSpec((1, SSD_INNER), const2)],
        out_specs=(pl.BlockSpec((1, lc, SSD_INNER), lambda bi, ci: (bi, ci, 0)),
                   pl.BlockSpec((1, SSD_GROUPS, D_STATE, gw), lambda bi, ci: (bi, 0, 0, 0))),
        out_shape=(jax.ShapeDtypeStruct((b, t, SSD_INNER), MXU_DTYPE),
                   jax.ShapeDtypeStruct((b, SSD_GROUPS, D_STATE, gw), F32)),
        scratch_shapes=[pltpu.VMEM((_XP_BASE + lc, CONV_DIM), F32),
                        pltpu.VMEM((SSD_GROUPS, D_STATE, gw), F32),
                        pltpu.VMEM((lc, SSD_INNER), F32)],
        compiler_params=_cparams(2),
        name="ssd",
    )(xbc, z, sm, dt_t, conv0, ssm0_t, conv_w.astype(F32), conv_b.astype(F32)[None, :], hp, hp.T,
      d_full, norm_w.astype(F32)[None, :])


def _state_to_kernel_layout(s):
    b = s.shape[0]
    s = s.reshape(b, SSD_GROUPS, HEADS_PER_GROUP, SSD_HEAD_DIM, D_STATE)
    return s.transpose(0, 1, 4, 2, 3).reshape(b, SSD_GROUPS, D_STATE, HEADS_PER_GROUP * SSD_HEAD_DIM)


def _state_from_kernel_layout(s):
    b = s.shape[0]
    s = s.reshape(b, SSD_GROUPS, D_STATE, HEADS_PER_GROUP, SSD_HEAD_DIM)
    return s.transpose(0, 1, 3, 4, 2).reshape(b, SSD_HEADS, SSD_HEAD_DIM, D_STATE)


_T5_LOG_STARTS = (12, 16, 23, 32, 46, 64, 91)
_T5_FAR_BUCKET = REL_BUCKETS // 2 - 1


def _bias_kernel(rb_ref, out_ref, *, tk):
    half = REL_BUCKETS // 2
    max_exact = half // 2
    ri = lax.broadcasted_iota(I32, (tk, tk), 0)
    ci = lax.broadcasted_iota(I32, (tk, tk), 1)
    for t in range(2):
        rel = ci - ri - (1 - t) * tk
        n = jnp.abs(rel)
        large = jnp.full((tk, tk), max_exact, I32)
        for start in _T5_LOG_STARTS:
            large = large + (n >= start).astype(I32)
        bucket = jnp.where(rel > 0, half, 0) + jnp.where(n < max_exact, n, large)
        for h in range(N_HEADS):
            acc = jnp.zeros((tk, tk), F32)
            for bkt in range(REL_BUCKETS):
                acc = jnp.where(bucket == bkt, rb_ref[bkt, h], acc)
            out_ref[t, h] = acc - rb_ref[_T5_FAR_BUCKET, h]


def _bias_tiles(rel_bias, tk):
    return pl.pallas_call(
        functools.partial(_bias_kernel, tk=tk),
        in_specs=[pl.BlockSpec(memory_space=pltpu.SMEM)],
        out_specs=pl.BlockSpec(memory_space=pltpu.VMEM),
        out_shape=jax.ShapeDtypeStruct((2, N_HEADS, tk, tk), F32),
        name="bias_tiles",
    )(rel_bias.astype(F32))


_MAX_BISECTIONS = 40
_SCORE_COLS = 256
_SUBLANES = 8
_SPARSE_NUM, _SPARSE_DEN = 3, 8


def _select_kernel(qi_ref, sm_ref, kit_ref, mask_ref, key_ref, w_ref, *, tq, wk, s_len, q_start, n_sel):
    qb = pl.program_id(1)
    n_blocks = mask_ref.shape[2] // wk
    nt = wk // LANES
    p0 = q_start + qb * tq
    pos = p0 + lax.broadcasted_iota(I32, (tq, 1), 0)
    vis_end = (pos // CHUNK + 1) * CHUNK
    vis_max = jnp.minimum(((p0 + tq - 1) // CHUNK + 1) * CHUNK, s_len)
    nkb = (vis_max + wk - 1) // wk

    wi = sm_ref[0][:, SM_WI:SM_WI + IDX_HEADS] * IDX_HEADS ** -0.5
    for hh in range(IDX_HEADS):
        w_ref[hh] = jnp.broadcast_to(wi[:, hh:hh + 1], (tq, LANES))
    lane_k = lax.broadcasted_iota(I32, (1, LANES), 1)

    def to_key(score):
        bits = lax.bitcast_convert_type(score, I32)
        return jnp.where(bits < 0, INT_MIN - bits, bits)

    def score_block(j, top, masked):
        m1, m2 = top
        k0 = pl.multiple_of(j * wk, wk)
        for c in range(wk // _SCORE_COLS):
            c0 = k0 + c * _SCORE_COLS
            kt = kit_ref[0, :, pl.ds(c0, _SCORE_COLS)]
            d = jnp.maximum(jnp.dot(qi_ref[0, 0], kt, preferred_element_type=F32), 0.0)
            for u in range(_SCORE_COLS // LANES):
                sc = jnp.zeros((tq, LANES), F32)
                for hh in range(IDX_HEADS):
                    sc = sc + w_ref[hh] * d[hh * tq:(hh + 1) * tq, u * LANES:(u + 1) * LANES]
                if masked:
                    key_pos = c0 + u * LANES + lane_k
                    sc = jnp.where((key_pos < vis_end) & (key_pos < s_len), sc, -jnp.inf)
                key_ref[:, pl.ds(c0 + u * LANES, LANES)] = to_key(sc)
                m2 = jnp.maximum(m2, jnp.minimum(m1, sc))
                m1 = jnp.maximum(m1, sc)
        return m1, m2

    n_open = jnp.minimum(p0, s_len) // wk
    top = (jnp.full((tq, LANES), -jnp.inf, F32),) * 2
    top = lax.fori_loop(0, n_open, functools.partial(score_block, masked=False), top)
    m1, m2 = lax.fori_loop(n_open, nkb, functools.partial(score_block, masked=True), top)

    def count_ge(thr):
        thr_b = jnp.broadcast_to(thr, (tq, LANES))

        def body(j, acc):
            k0 = pl.multiple_of(j * wk, wk)
            for u in range(nt):
                acc = acc + jnp.where(key_ref[:, pl.ds(k0 + u * LANES, LANES)] >= thr_b, 1.0, 0.0)
            return acc

        acc = lax.fori_loop(0, nkb, body, jnp.zeros((tq, LANES), F32))
        return jnp.sum(acc, axis=1, keepdims=True).astype(I32)

    n_groups = tq // _SUBLANES

    def count_ge_groups(thr, group_bits):
        thr_b = jnp.broadcast_to(thr, (tq, LANES))
        zero = jnp.zeros((_SUBLANES, LANES), F32)

        def count_group(rg):
            rows = slice(rg * _SUBLANES, (rg + 1) * _SUBLANES)
            thr_g = thr_b[rows]

            def body(j, accs):
                blk = key_ref[rows, pl.ds(pl.multiple_of(j * wk, wk), wk)]
                accs = list(accs)
                for u in range(nt):
                    hit = jnp.where(blk[:, u * LANES:(u + 1) * LANES] >= thr_g, 1.0, 0.0)
                    accs[u % len(accs)] = accs[u % len(accs)] + hit
                return tuple(accs)

            return sum(lax.fori_loop(0, nkb, body, (zero,) * min(4, nt)))

        parts = [lax.cond(((group_bits >> rg) & 1) == 1, functools.partial(count_group, rg), lambda: zero)
                 for rg in range(n_groups)]
        return jnp.sum(jnp.concatenate(parts, axis=0), axis=1, keepdims=True).astype(I32)

    def unfinished(lo, hi, cnt_lo):
        return (cnt_lo > n_sel) & (hi - 1 > lo)

    def live_groups(live):
        any_live = jnp.max(live.astype(I32).reshape(n_groups, _SUBLANES, 1), axis=1)
        weight = lax.shift_left(jnp.int32(1), lax.broadcasted_iota(I32, (n_groups, 1), 0))
        return jnp.sum(any_live * weight), jnp.sum(any_live)

    def bisect(lo, hi, cnt_lo):
        def cond(st):
            return (st[0] < _MAX_BISECTIONS) & (st[1] != 0)

        def body(st):
            it, group_bits, n_live, lo, hi, cnt_lo = st
            live = unfinished(lo, hi, cnt_lo)
            mid = (lo >> 1) + (hi >> 1) + (lo & hi & 1)
            thr = jnp.where(live, mid, lo)
            cnt = lax.cond(n_live * _SPARSE_DEN <= n_groups * _SPARSE_NUM,
                           lambda: count_ge_groups(thr, group_bits), lambda: count_ge(thr))
            up = live & (cnt >= n_sel)
            lo = jnp.where(up, mid, lo)
            cnt_lo = jnp.where(up, cnt, cnt_lo)
            hi = jnp.where(live & (cnt < n_sel), mid, hi)
            return (it + 1,) + live_groups(unfinished(lo, hi, cnt_lo)) + (lo, hi, cnt_lo)

        st0 = (jnp.int32(0),) + live_groups(unfinished(lo, hi, cnt_lo)) + (lo, hi, cnt_lo)
        return lax.while_loop(cond, body, st0)[3:]

    lo = jnp.maximum(to_key(jnp.min(m2, axis=1, keepdims=True)), KEY_NEG_INF + 1)
    hi = to_key(jnp.max(m1, axis=1, keepdims=True)) + 1
    tau, _, cnt_tau = bisect(lo, hi, count_ge(lo))

    tie_rows = cnt_tau > n_sel
    has_ties = jnp.max(tie_rows.astype(I32)) > 0

    @pl.when(jnp.logical_not(has_ties))
    def _():
        tau_b = jnp.broadcast_to(tau, (tq, LANES))

        def body(j, carry):
            k0 = pl.multiple_of(j * wk, wk)
            for u in range(nt):
                sl = pl.ds(k0 + u * LANES, LANES)
                mask_ref[0, :, sl] = (key_ref[:, sl] >= tau_b).astype(I32).astype(mask_ref.dtype)
            return carry
        lax.fori_loop(0, nkb, body, 0)

    @pl.when(has_ties)
    def _():
        upper = (lax.broadcasted_iota(I32, (wk, wk), 0) <= lax.broadcasted_iota(I32, (wk, wk), 1)).astype(MXU_DTYPE)
        need = jnp.where(tie_rows, n_sel - count_ge(tau + 1), wk * n_blocks).astype(F32)

        def body(j, seen):
            k0 = pl.multiple_of(j * wk, wk)
            blk = key_ref[:, pl.ds(k0, wk)]
            eq = blk == tau
            rank = seen + jnp.dot(eq.astype(F32).astype(MXU_DTYPE), upper, preferred_element_type=F32)
            sel = (blk > tau) | (eq & (rank <= need))
            mask_ref[0, :, pl.ds(k0, wk)] = sel.astype(I32).astype(mask_ref.dtype)
            return seen + jnp.sum(eq.astype(F32), axis=1, keepdims=True)
        lax.fori_loop(0, nkb, body, jnp.zeros((tq, 1), F32))

    def zero_body(j, carry):
        k0 = pl.multiple_of(j * wk, wk)
        mask_ref[0, :, pl.ds(k0, wk)] = jnp.zeros((tq, wk), mask_ref.dtype)
        return carry

    lax.fori_loop(nkb, n_blocks, zero_body, 0)


def _select(qi, sm, kit, tq, wk, s_len, q_start, n_sel):
    b, t, _ = qi.shape
    s_pad = kit.shape[2]
    assert n_sel <= 2 * LANES, "the starting bracket takes two candidates per lane column"
    kern = functools.partial(_select_kernel, tq=tq, wk=wk, s_len=s_len, q_start=q_start, n_sel=n_sel)
    qi = qi.reshape(b, t // tq, tq, IDX_HEADS, IDX_DIM).transpose(0, 1, 3, 2, 4).reshape(b, t // tq, IDX_HEADS * tq, IDX_DIM)
    return pl.pallas_call(
        kern,
        grid=(b, t // tq),
        in_specs=[pl.BlockSpec((1, 1, IDX_HEADS * tq, IDX_DIM), lambda bi, qi_: (bi, qi_, 0, 0)),
                  pl.BlockSpec((1, tq, LANES), lambda bi, qi_: (bi, qi_, 0)),
                  _per_batch((1, IDX_DIM, s_pad), lambda bi, qi_: (bi, 0, 0), t // tq)],
        out_specs=pl.BlockSpec((1, tq, s_pad), lambda bi, qi_: (bi, qi_, 0)),
        out_shape=jax.ShapeDtypeStruct((b, t, s_pad), jnp.int8),
        scratch_shapes=[pltpu.VMEM((tq, s_pad), I32),
                        pltpu.VMEM((IDX_HEADS, tq, LANES), F32)],
        compiler_params=_cparams(2),
        name="select",
    )(qi, sm, kit)


_V_AUG = 2 * HEAD_DIM
_V_ROWS = 512
_MID_TILES = (1024, 512)


def _attn_kernel(q_ref, kt_ref, v_ref, mask_ref, bias_ref, o_ref, m_ref, acc_ref, qg_ref, vaug_ref,
                 *, tq, tk, far_widths, q_start):
    qb = pl.program_id(1)
    jd = (q_start + qb * tq) // tk

    @pl.when(qb == 0)
    def _():
        one_hot = (lax.broadcasted_iota(I32, (_V_ROWS, _V_AUG - HEAD_DIM), 1) == 0).astype(vaug_ref.dtype)

        def body(i, carry):
            r0 = pl.multiple_of(i * _V_ROWS, _V_ROWS)
            blk = v_ref[0, pl.ds(r0, _V_ROWS), :]
            for g in range(N_KV_HEADS):
                vaug_ref[pl.ds(r0, _V_ROWS), g * _V_AUG:g * _V_AUG + HEAD_DIM] = blk[:, g * HEAD_DIM:(g + 1) * HEAD_DIM]
                vaug_ref[pl.ds(r0, _V_ROWS), g * _V_AUG + HEAD_DIM:(g + 1) * _V_AUG] = one_hot
            return carry

        lax.fori_loop(0, vaug_ref.shape[0] // _V_ROWS, body, 0)

    for g in range(N_KV_HEADS):
        for r in range(Q_PER_KV):
            c0 = (g * Q_PER_KV + r) * HEAD_DIM
            qg_ref[g, r * tq:(r + 1) * tq, :] = q_ref[0, :, c0:c0 + HEAD_DIM]
    m_ref[...] = jnp.full(m_ref.shape, NEG_BIG, F32)
    acc_ref[...] = jnp.zeros(acc_ref.shape, F32)

    def process(k0, width, near):
        nt = width // LANES
        mb = jnp.where(mask_ref[0, :, pl.ds(k0, width)].astype(I32) != 0, 0.0, NEG_BIG)
        mb = jnp.concatenate([mb] * Q_PER_KV, axis=0)
        for g in range(N_KV_HEADS):
            kt = kt_ref[0, g * HEAD_DIM:(g + 1) * HEAD_DIM, pl.ds(k0, width)]
            s = jnp.dot(qg_ref[g], kt, preferred_element_type=F32) + mb
            if near is not None:
                s = s + bias_ref[near, g]
            tiles = [s[:, u * LANES:(u + 1) * LANES] for u in range(nt)]
            mx = tiles[0]
            for u in range(1, nt):
                mx = jnp.maximum(mx, tiles[u])
            m_old = m_ref[g]
            m_new = jnp.maximum(m_old, jnp.max(mx, axis=1, keepdims=True))
            alpha = jnp.exp(m_old - m_new)
            p = jnp.concatenate([jnp.exp(tl - m_new).astype(MXU_DTYPE) for tl in tiles], axis=1)
            vt = vaug_ref[pl.ds(k0, width), g * _V_AUG:(g + 1) * _V_AUG]
            acc_ref[g] = alpha * acc_ref[g] + jnp.dot(p, vt, preferred_element_type=F32)
            m_ref[g] = m_new

    far_keys = jnp.maximum(jd - 1, 0) * tk
    done = 0
    for width in far_widths:
        steps = (far_keys - done) // width

        def far_body(i, carry, width=width, done=done):
            process(pl.multiple_of(done + i * width, width), width, None)
            return carry

        lax.fori_loop(0, steps, far_body, 0)
        done = done + steps * width

    @pl.when(jd >= 1)
    def _():
        process(pl.multiple_of((jd - 1) * tk, tk), tk, 0)

    process(pl.multiple_of(jd * tk, tk), tk, 1)

    for g in range(N_KV_HEADS):
        acc = acc_ref[g]
        o = acc[:, :HEAD_DIM] / acc[:, HEAD_DIM:HEAD_DIM + 1]
        for r in range(Q_PER_KV):
            c0 = (g * Q_PER_KV + r) * HEAD_DIM
            o_ref[0, :, c0:c0 + HEAD_DIM] = o[r * tq:(r + 1) * tq].astype(o_ref.dtype)


def _attention(q, kt, v, mask, bias, tq, tk, wide, q_start):
    b, t, _ = q.shape
    s_pad = kt.shape[2]
    assert s_pad % _V_ROWS == 0
    rows = Q_PER_KV * tq
    far_widths = tuple(sorted({wide, tk} | {w for w in _MID_TILES if tk < w < wide}, reverse=True))
    assert all(a % b == 0 for a, b in zip(far_widths, far_widths[1:]))
    kern = functools.partial(_attn_kernel, tq=tq, tk=tk, far_widths=far_widths, q_start=q_start)
    return pl.pallas_call(
        kern,
        grid=(b, t // tq),
        in_specs=[pl.BlockSpec((1, tq, N_HEADS * HEAD_DIM), lambda bi, qi_: (bi, qi_, 0)),
                  _per_batch((1, KV_DIM, s_pad), lambda bi, qi_: (bi, 0, 0), t // tq),
                  _per_batch((1, s_pad, KV_DIM), lambda bi, qi_: (bi, 0, 0), t // tq),
                  pl.BlockSpec((1, tq, s_pad), lambda bi, qi_: (bi, qi_, 0)),
                  _resident((2, N_KV_HEADS, rows, tk), lambda bi, qi_: (0, 0, 0, 0))],
        out_specs=pl.BlockSpec((1, tq, N_HEADS * HEAD_DIM), lambda bi, qi_: (bi, qi_, 0)),
        out_shape=jax.ShapeDtypeStruct((b, t, N_HEADS * HEAD_DIM), MXU_DTYPE),
        scratch_shapes=[pltpu.VMEM((N_KV_HEADS, rows, LANES), F32),
                        pltpu.VMEM((N_KV_HEADS, rows, _V_AUG), F32),
                        pltpu.VMEM((N_KV_HEADS, rows, HEAD_DIM), MXU_DTYPE),
                        pltpu.VMEM((s_pad, N_KV_HEADS * _V_AUG), MXU_DTYPE)],
        compiler_params=_cparams(2),
        name="attention",
    )(q, kt, v, mask, bias)


def _layer_norm(x, g, b):
    mu = jnp.mean(x, axis=-1, keepdims=True)
    xc = x - mu
    var = jnp.mean(xc * xc, axis=-1, keepdims=True)
    return xc * lax.rsqrt(var + LN_EPS) * g + b


def _post_kernel(o_ref, yn_ref, gs_ref, ga_ref, x_ref, wso_ref, wao_ref, wout_ref, wup_ref, wdown_ref,
                 ln_ref, out_ref):
    y1 = jnp.dot(yn_ref[...], wso_ref[...], preferred_element_type=F32)
    y2 = jnp.dot(o_ref[...], wao_ref[...], preferred_element_type=F32)
    mix = _sigmoid(gs_ref[...]) * y1 + _sigmoid(ga_ref[...]) * y2
    mixed = _dot(mix, wout_ref[...])
    h = _layer_norm(ALPHA * x_ref[...] + mixed, ln_ref[0:1, :], ln_ref[1:2, :])
    u = jnp.maximum(_dot(h, wup_ref[...]), 0.0)
    f = _dot(u * u, wdown_ref[...])
    out_ref[...] = _layer_norm(ALPHA * h + f, ln_ref[2:3, :], ln_ref[3:4, :])


def _post(o, yn, gs, ga, x2d, w_ssd_o, w_attn_o, w_out, w_up, w_down, ln, tm):
    m = x2d.shape[0]
    row = lambda w: pl.BlockSpec((tm, w), lambda i: (i, 0))
    const = lambda i: (0, 0)
    return pl.pallas_call(
        _post_kernel,
        grid=(m // tm,),
        in_specs=[row(N_HEADS * HEAD_DIM), row(SSD_INNER), row(D_MODEL), row(D_MODEL), row(D_MODEL),
                  _resident((SSD_INNER, D_MODEL), const),
                  _resident((N_HEADS * HEAD_DIM, D_MODEL), const),
                  _resident((D_MODEL, D_MODEL), const),
                  _resident((D_MODEL, D_FF), const),
                  _resident((D_FF, D_MODEL), const),
                  _resident((4, D_MODEL), const)],
        out_specs=row(D_MODEL),
        out_shape=jax.ShapeDtypeStruct((m, D_MODEL), F32),
        compiler_params=_cparams(1),
        name="post",
    )(o, yn, gs, ga, x2d, w_ssd_o, w_attn_o, w_out, w_up, w_down, ln)


def _round_up(n, m):
    return (n + m - 1) // m * m


def _layer(x, cache_k, cache_v, cache_kidx, state_ssm, state_conv, bias_tiles, w_prep, conv_w, conv_b,
           dt_bias, a_log, d_skip, ssd_norm_w, w_ssd_o, w_attn_o, w_out, w_up, w_down, ln,
           *, tq, tk, wide, tqs, wk, lc, tm):
    b, t, _ = x.shape
    past = cache_k.shape[1]
    s_len = past + t
    m = b * t
    assert m % tm == 0 and t % tq == 0 and past % tk == 0 and tk % LANES == 0 and wide % tk == 0
    assert t % tqs == 0 and wk % LANES == 0
    assert tq == tk or t == tq, "every query tile must start on a key-tile boundary"
    assert t % lc == 0 or t < lc
    x2d = x.reshape(m, D_MODEL)

    z, xbc, q, k, v, qi, gs, ga, sm = _inproj(x2d, w_prep, tm)

    t_pad = _round_up(t, lc)
    pad_t = lambda a: jnp.pad(a.reshape(b, t, a.shape[-1]), ((0, 0), (0, t_pad - t), (0, 0)))
    sm3 = pad_t(sm)
    dt_t = sm3[:, :, SM_DT:SM_DT + SSD_HEADS].transpose(0, 2, 1)
    yn, ssm_new = _ssd(pad_t(xbc), pad_t(z), sm3, dt_t, state_conv.astype(F32),
                       _state_to_kernel_layout(state_ssm.astype(F32)), conv_w, conv_b, dt_bias, a_log, d_skip,
                       ssd_norm_w, lc, min(t, lc))
    yn = yn[:, :t].reshape(m, SSD_INNER)
    new_ssm = _state_from_kernel_layout(ssm_new)
    xbc3 = xbc.reshape(b, t, CONV_DIM)
    new_conv = jnp.concatenate([state_conv.astype(F32), xbc3], axis=1)[:, -_HALO:]

    k3 = k.reshape(b, t, KV_DIM)
    v3 = v.reshape(b, t, KV_DIM)
    ki3 = sm.reshape(b, t, LANES)[:, :, SM_KI:SM_KI + IDX_DIM]
    s_pad = _round_up(s_len, max(wide, wk))
    assert s_pad % wide == 0 and s_pad % wk == 0
    pad_s = lambda a: jnp.pad(a, ((0, 0), (0, s_pad - s_len)) + ((0, 0),) * (a.ndim - 2))
    k_all = pad_s(jnp.concatenate([cache_k.reshape(b, past, KV_DIM).astype(MXU_DTYPE), k3.astype(MXU_DTYPE)], axis=1))
    v_all = pad_s(jnp.concatenate([cache_v.reshape(b, past, KV_DIM).astype(MXU_DTYPE), v3.astype(MXU_DTYPE)], axis=1))
    ki_all = pad_s(jnp.concatenate([cache_kidx.astype(MXU_DTYPE), ki3.astype(MXU_DTYPE)], axis=1))
    n_sel = min(TOPK_MAX, s_len // 4)
    mask = _select(qi.reshape(b, t, -1), sm.reshape(b, t, LANES), ki_all.transpose(0, 2, 1),
                   tqs, wk, s_len, past, n_sel)
    bias = bias_tiles[:, :, :tq, :].reshape(2, N_KV_HEADS, Q_PER_KV * tq, tk)
    o = _attention(q.reshape(b, t, -1), k_all.transpose(0, 2, 1), v_all, mask, bias, tq, tk, wide, past)

    out = _post(o.reshape(m, -1), yn, gs, ga, x2d, w_ssd_o, w_attn_o, w_out, w_up, w_down, ln, tm)
    return out.reshape(b, t, D_MODEL), (k3.reshape(b, t, N_KV_HEADS, HEAD_DIM), v3.reshape(b, t, N_KV_HEADS, HEAD_DIM),
                                        ki3, new_ssm, new_conv)


ATTN_TILE = 128
WIDE_TILE = 2048
WIDE_TILE_SHORT = 512
SELECT_ROWS = 128
SELECT_WIDTH = 1024
SSD_CHUNK = 128
ROW_TILE = 256


def kernel(x_prompt, x_sample, cache_k, cache_v, cache_kidx, state_ssm, state_conv, rel_bias, w_in, conv_w, conv_b,
           dt_bias, a_log, d_skip, ssd_norm_w, w_ssd_o, w_attn_o, w_out, ln1_g, ln1_b, w_up, w_down, ln2_g, ln2_b):
    bp = x_prompt.shape[0]
    dtype = x_prompt.dtype
    bias_tiles = _bias_tiles(rel_bias, ATTN_TILE)
    yp, ys = x_prompt, x_sample
    st_p, st_s = [], []
    tiles = dict(tk=ATTN_TILE, wk=SELECT_WIDTH, lc=SSD_CHUNK, tm=ROW_TILE)
    for l in range(DEPTH):
        w_prep = _prep_w_in(w_in[l])
        ln = jnp.stack([ln1_g[l], ln1_b[l], ln2_g[l], ln2_b[l]]).astype(F32)
        weights = (bias_tiles, w_prep, conv_w[l], conv_b[l], dt_bias[l], a_log[l], d_skip[l], ssd_norm_w[l],
                   w_ssd_o[l].astype(MXU_DTYPE), w_attn_o[l].astype(MXU_DTYPE), w_out[l].astype(MXU_DTYPE),
                   w_up[l].astype(MXU_DTYPE), w_down[l].astype(MXU_DTYPE), ln)
        yp, sp = _layer(yp, jnp.zeros((bp, 0, N_KV_HEADS, HEAD_DIM), dtype), jnp.zeros((bp, 0, N_KV_HEADS, HEAD_DIM), dtype),
                        jnp.zeros((bp, 0, IDX_DIM), dtype), jnp.zeros((bp, SSD_HEADS, SSD_HEAD_DIM, D_STATE), dtype),
                        jnp.zeros((bp, _HALO, CONV_DIM), dtype), *weights,
                        tq=min(ATTN_TILE, yp.shape[1]), tqs=min(SELECT_ROWS, yp.shape[1]), wide=WIDE_TILE, **tiles)
        ys, ss = _layer(ys, cache_k[l], cache_v[l], cache_kidx[l], state_ssm[l], state_conv[l], *weights,
                        tq=min(ATTN_TILE, ys.shape[1]), tqs=min(SELECT_ROWS, ys.shape[1]), wide=WIDE_TILE_SHORT, **tiles)
        st_p.append(sp)
        st_s.append(ss)
    stack = lambda sts, i: jnp.stack([s[i] for s in sts])
    return (yp, ys) + tuple(stack(st_p, i) for i in range(5)) + tuple(stack(st_s, i) for i in range(5))
```

```python
import functools

import jax
import jax.numpy as jnp
from jax import lax
from jax.experimental import pallas as pl
from jax.experimental.pallas import tpu as pltpu

F32 = jnp.float32
I32 = jnp.int32
MXU_DTYPE = jnp.bfloat16

D_MODEL = 1024
CHUNK = 64
SSD_INNER = 2048
SSD_HEAD_DIM = 64
SSD_HEADS = 32
SSD_GROUPS = 4
HEADS_PER_GROUP = SSD_HEADS // SSD_GROUPS
D_STATE = 128
CONV_W = 4
CONV_DIM = SSD_INNER + 2 * SSD_GROUPS * D_STATE
N_HEADS = 16
N_KV_HEADS = 4
Q_PER_KV = N_HEADS // N_KV_HEADS
HEAD_DIM = 64
KV_DIM = N_KV_HEADS * HEAD_DIM
IDX_HEADS = 8
IDX_DIM = 64
TOPK_MAX = 256
REL_BUCKETS = 32
REL_MAX_DIST = 128
D_FF = 4 * D_MODEL
DEPTH = 1
ALPHA = (2.0 * DEPTH) ** 0.25
LN_EPS = 1e-5
RMS_EPS = 1e-5
IN_SIZES = (SSD_INNER, CONV_DIM, SSD_HEADS, N_HEADS * HEAD_DIM, KV_DIM, KV_DIM,
            IDX_HEADS * IDX_DIM, IDX_DIM, IDX_HEADS, D_MODEL, D_MODEL)

LANES = 128
VMEM_LIMIT_BYTES = 56 * 1024 * 1024

SM_KI = 0
SM_DT = IDX_DIM
SM_WI = IDX_DIM + SSD_HEADS
SM_PAD = LANES - (IDX_DIM + SSD_HEADS + IDX_HEADS)

NEG_BIG = -1e30
INT_MIN = -2 ** 31
INT_MAX = 2 ** 31 - 1
KEY_NEG_INF = -0x7F800000


def _cparams(n_axes):
    return pltpu.CompilerParams(dimension_semantics=("arbitrary",) * n_axes,
                                vmem_limit_bytes=VMEM_LIMIT_BYTES)


def _resident(shape, index_map):
    return pl.BlockSpec(shape, index_map, pipeline_mode=pl.Buffered(1))


def _per_batch(shape, index_map, steps_per_batch):
    if steps_per_batch > 1:
        return _resident(shape, index_map)
    return pl.BlockSpec(shape, index_map)


def _dot(a, b):
    return jnp.dot(a.astype(MXU_DTYPE), b.astype(MXU_DTYPE), preferred_element_type=F32)


def _dot_exact(a, b):
    return jnp.dot(a, b, preferred_element_type=F32, precision=lax.Precision.HIGHEST)


def _sigmoid(x):
    return 0.5 * jnp.tanh(0.5 * x) + 0.5


def _softplus(x):
    return jnp.maximum(x, 0.0) + jnp.log1p(jnp.exp(-jnp.abs(x)))


_PROJ_WIDTHS = (SSD_INNER, CONV_DIM, N_HEADS * HEAD_DIM, KV_DIM, KV_DIM, IDX_HEADS * IDX_DIM,
                D_MODEL, D_MODEL, LANES)
_PROJ_WIDTH = sum(_PROJ_WIDTHS)
_PROJ_CHUNK = 512


def _prep_w_in(w_in):
    offs = [0]
    for s in IN_SIZES:
        offs.append(offs[-1] + s)
    z, xbc, dt, q, k, v, qi, ki, wi, gs, ga = [w_in[:, offs[i]:offs[i + 1]] for i in range(len(IN_SIZES))]
    small = jnp.concatenate([ki, dt, wi, jnp.zeros((D_MODEL, SM_PAD), w_in.dtype)], axis=1)
    w = jnp.concatenate([z, xbc, q * HEAD_DIM ** -0.5, k, v, qi * IDX_DIM ** -0.5, gs, ga, small], axis=1)
    return w.astype(MXU_DTYPE)


def _inproj_kernel(x_ref, w_ref, z_ref, xbc_ref, q_ref, k_ref, v_ref, qi_ref, gs_ref, ga_ref, sm_ref):
    xb = x_ref[...].astype(MXU_DTYPE)
    outs = (z_ref, xbc_ref, q_ref, k_ref, v_ref, qi_ref, gs_ref, ga_ref, sm_ref)
    c0 = 0
    for ref, width in zip(outs, _PROJ_WIDTHS):
        for o in range(0, width, _PROJ_CHUNK):
            n = min(_PROJ_CHUNK, width - o)
            r = jnp.dot(xb, w_ref[:, c0 + o:c0 + o + n], preferred_element_type=F32)
            ref[:, o:o + n] = r.astype(ref.dtype)
        c0 += width


def _inproj(x2d, w_prep, tm):
    m = x2d.shape[0]
    dtypes = (F32, F32, MXU_DTYPE, F32, F32, MXU_DTYPE, F32, F32, F32)
    out_shape = tuple(jax.ShapeDtypeStruct((m, w), d) for w, d in zip(_PROJ_WIDTHS, dtypes))
    out_specs = tuple(pl.BlockSpec((tm, w), lambda i: (i, 0)) for w in _PROJ_WIDTHS)
    return pl.pallas_call(
        _inproj_kernel,
        grid=(m // tm,),
        in_specs=[pl.BlockSpec((tm, D_MODEL), lambda i: (i, 0)),
                  _resident((D_MODEL, _PROJ_WIDTH), lambda i: (0, 0))],
        out_specs=out_specs,
        out_shape=out_shape,
        compiler_params=_cparams(1),
        name="inproj",
    )(x2d, w_prep)


_HALO = CONV_W - 1
_XP_BASE = 8


def _ssd_kernel(xbc_ref, z_ref, sm_ref, dtT_ref, conv0_ref, ssm0_ref, convw_ref, convb_ref,
                hp_ref, hpT_ref, dfull_ref, nw_ref,
                yn_ref, ssm_out_ref, xp_ref, st_ref, y_ref, *, lc, t_valid):
    c = pl.program_id(1)
    nc = pl.num_programs(1)

    @pl.when(c == 0)
    def _():
        xp_ref[_XP_BASE - _HALO:_XP_BASE, :] = conv0_ref[0]
        st_ref[...] = ssm0_ref[0]

    xp_ref[_XP_BASE:_XP_BASE + lc, :] = xbc_ref[0]
    staged = xp_ref[...]
    conv = convb_ref[...]
    for i in range(CONV_W):
        back = _HALO - i
        shifted = pltpu.roll(staged, back, axis=0) if back else staged
        conv = conv + shifted[_XP_BASE:, :] * convw_ref[i:i + 1, :]
    xp_ref[_XP_BASE - _HALO:_XP_BASE, :] = xp_ref[_XP_BASE + lc - _HALO:_XP_BASE + lc, :]
    act = conv * _sigmoid(conv)
    xs = act[:, :SSD_INNER]
    bm = act[:, SSD_INNER:SSD_INNER + SSD_GROUPS * D_STATE]
    cm = act[:, SSD_INNER + SSD_GROUPS * D_STATE:]

    dt = _softplus(sm_ref[0][:, SM_DT:SM_DT + SSD_HEADS] + hp_ref[0:1, :])
    dt_t = _softplus(dtT_ref[0] + hpT_ref[:, 0:1])
    if t_valid < lc:
        dt = jnp.where(lax.broadcasted_iota(I32, (lc, SSD_HEADS), 0) < t_valid, dt, 0.0)
        dt_t = jnp.where(lax.broadcasted_iota(I32, (SSD_HEADS, lc), 1) < t_valid, dt_t, 0.0)
    a = dt * (-jnp.exp(hp_ref[1:2, :]))
    a_t = dt_t * (-jnp.exp(hpT_ref[:, 1:2]))
    ri = lax.broadcasted_iota(I32, (lc, lc), 0)
    ci = lax.broadcasted_iota(I32, (lc, lc), 1)
    causal = ci <= ri
    acum = _dot_exact(causal.astype(F32), a)
    acum_t = _dot_exact(a_t, (ri <= ci).astype(F32))
    e_a = jnp.exp(acum)
    w_end = jnp.exp(acum[lc - 1:lc, :] - acum) * dt
    lane = lax.broadcasted_iota(I32, (lc, 2 * SSD_HEAD_DIM), 1)
    first_head = lane < SSD_HEAD_DIM
    gw = HEADS_PER_GROUP * SSD_HEAD_DIM

    def per_head_lanes(v, h0):
        return jnp.where(first_head, v[:, h0:h0 + 1], v[:, h0 + 1:h0 + 2])

    for g in range(SSD_GROUPS):
        b_g = bm[:, g * D_STATE:(g + 1) * D_STATE]
        c_g = cm[:, g * D_STATE:(g + 1) * D_STATE].astype(MXU_DTYPE)
        b_gt = b_g.T.astype(MXU_DTYPE)
        cb = jnp.dot(c_g, b_gt, preferred_element_type=F32)
        state = st_ref[g]
        y_off = _dot(c_g, state)
        xw, e_last = [], []
        for jp in range(HEADS_PER_GROUP // 2):
            h0 = g * HEADS_PER_GROUP + 2 * jp
            ms = []
            for h in (h0, h0 + 1):
                seg = acum[:, h:h + 1] - acum_t[h:h + 1, :]
                decay = jnp.exp(jnp.where(causal, seg, -jnp.inf))
                ms.append((cb * decay * dt_t[h:h + 1, :]).astype(MXU_DTYPE))
            c0 = h0 * SSD_HEAD_DIM
            x_pair = xs[:, c0:c0 + 2 * SSD_HEAD_DIM]
            r = jnp.dot(jnp.concatenate(ms, axis=0), x_pair.astype(MXU_DTYPE), preferred_element_type=F32)
            y_diag = jnp.where(first_head, r[:lc], r[lc:])
            o0 = 2 * jp * SSD_HEAD_DIM
            e_pair = per_head_lanes(e_a, h0)
            y_ref[:, c0:c0 + 2 * SSD_HEAD_DIM] = (y_diag + y_off[:, o0:o0 + 2 * SSD_HEAD_DIM] * e_pair
                                                  + dfull_ref[:, c0:c0 + 2 * SSD_HEAD_DIM] * x_pair)
            xw.append((x_pair * per_head_lanes(w_end, h0)).astype(MXU_DTYPE))
            e_last.append(e_pair[lc - 1:lc, :])
        st_ref[g] = (state * jnp.concatenate(e_last, axis=1)
                     + jnp.dot(b_gt, jnp.concatenate(xw, axis=1), preferred_element_type=F32))

    zv = z_ref[0]
    hg = y_ref[...] * (zv * _sigmoid(zv))
    for g in range(SSD_GROUPS):
        sl = hg[:, g * gw:(g + 1) * gw]
        ms = jnp.mean(sl * sl, axis=-1, keepdims=True)
        yn_ref[0, :, g * gw:(g + 1) * gw] = (sl * lax.rsqrt(ms + RMS_EPS) * nw_ref[:, g * gw:(g + 1) * gw]).astype(yn_ref.dtype)

    @pl.when(c == nc - 1)
    def _():
        ssm_out_ref[0] = st_ref[...]


def _ssd(xbc, z, sm, dt_t, conv0, ssm0_t, conv_w, conv_b, dt_bias, a_log, d_skip, norm_w, lc, t_valid):
    b, t, _ = xbc.shape
    nc = t // lc
    hp = jnp.stack([dt_bias, a_log]).astype(F32)
    d_full = jnp.repeat(d_skip.astype(F32), SSD_HEAD_DIM)[None, :]
    gw = HEADS_PER_GROUP * SSD_HEAD_DIM
    const2 = lambda bi, ci: (0, 0)
    kern = functools.partial(_ssd_kernel, lc=lc, t_valid=t_valid)
    return pl.pallas_call(
        kern,
        grid=(b, nc),
        in_specs=[pl.BlockSpec((1, lc, CONV_DIM), lambda bi, ci: (bi, ci, 0)),
                  pl.BlockSpec((1, lc, SSD_INNER), lambda bi, ci: (bi, ci, 0)),
                  pl.BlockSpec((1, lc, LANES), lambda bi, ci: (bi, ci, 0)),
                  pl.BlockSpec((1, SSD_HEADS, lc), lambda bi, ci: (bi, 0, ci)),
                  pl.BlockSpec((1, _HALO, CONV_DIM), lambda bi, ci: (bi, 0, 0)),
                  pl.BlockSpec((1, SSD_GROUPS, D_STATE, gw), lambda bi, ci: (bi, 0, 0, 0)),
                  pl.BlockSpec((CONV_W, CONV_DIM), const2),
                  pl.BlockSpec((1, CONV_DIM), const2),
                  pl.BlockSpec((2, SSD_HEADS), const2),
                  pl.BlockSpec((SSD_HEADS, 2), const2),
                  pl.BlockSpec((1, SSD_INNER), const2),
                  pl.BlockSpec((1, SSD_INNER), const2)],
        out_specs=(pl.BlockSpec((1, lc, SSD_INNER), lambda bi, ci: (bi, ci, 0)),
                   pl.BlockSpec((1, SSD_GROUPS, D_STATE, gw), lambda bi, ci: (bi, 0, 0, 0))),
        out_shape=(jax.ShapeDtypeStruct((b, t, SSD_INNER), MXU_DTYPE),
                   jax.ShapeDtypeStruct((b, SSD_GROUPS, D_STATE, gw), F32)),
        scratch_shapes=[pltpu.VMEM((_XP_BASE + lc, CONV_DIM), F32),
                        pltpu.VMEM((SSD_GROUPS, D_STATE, gw), F32),
                        pltpu.VMEM((lc, SSD_INNER), F32)],
        compiler_params=_cparams(2),
        name="ssd",
    )(xbc, z, sm, dt_t, conv0, ssm0_t, conv_w.astype(F32), conv_b.astype(F32)[None, :], hp, hp.T,
      d_full, norm_w.astype(F32)[None, :])


def _state_to_kernel_layout(s):
    b = s.shape[0]
    s = s.reshape(b, SSD_GROUPS, HEADS_PER_GROUP, SSD_HEAD_DIM, D_STATE)
    return s.transpose(0, 1, 4, 2, 3).reshape(b, SSD_GROUPS, D_STATE, HEADS_PER_GROUP * SSD_HEAD_DIM)


def _state_from_kernel_layout(s):
    b = s.shape[0]
    s = s.reshape(b, SSD_GROUPS, D_STATE, HEADS_PER_GROUP, SSD_HEAD_DIM)
    return s.transpose(0, 1, 3, 4, 2).reshape(b, SSD_HEADS, SSD_HEAD_DIM, D_STATE)


_T5_LOG_STARTS = (12, 16, 23, 32, 46, 64, 91)
_T5_FAR_BUCKET = REL_BUCKETS // 2 - 1


def _bias_kernel(rb_ref, out_ref, *, tk):
    half = REL_BUCKETS // 2
    max_exact = half // 2
    ri = lax.broadcasted_iota(I32, (tk, tk), 0)
    ci = lax.broadcasted_iota(I32, (tk, tk), 1)
    for t in range(2):
        rel = ci - ri - (1 - t) * tk
        n = jnp.abs(rel)
        large = jnp.full((tk, tk), max_exact, I32)
        for start in _T5_LOG_STARTS:
            large = large + (n >= start).astype(I32)
        bucket = jnp.where(rel > 0, half, 0) + jnp.where(n < max_exact, n, large)
        for h in range(N_HEADS):
            acc = jnp.zeros((tk, tk), F32)
            for bkt in range(REL_BUCKETS):
                acc = jnp.where(bucket == bkt, rb_ref[bkt, h], acc)
            out_ref[t, h] = acc - rb_ref[_T5_FAR_BUCKET, h]


def _bias_tiles(rel_bias, tk):
    return pl.pallas_call(
        functools.partial(_bias_kernel, tk=tk),
        in_specs=[pl.BlockSpec(memory_space=pltpu.SMEM)],
        out_specs=pl.BlockSpec(memory_space=pltpu.VMEM),
        out_shape=jax.ShapeDtypeStruct((2, N_HEADS, tk, tk), F32),
        name="bias_tiles",
    )(rel_bias.astype(F32))


_MAX_BISECTIONS = 40
_SCORE_COLS = 256


def _select_kernel(qi_ref, sm_ref, kit_ref, mask_ref, key_ref, w_ref, *, tq, wk, s_len, q_start, n_sel):
    qb = pl.program_id(1)
    n_blocks = mask_ref.shape[2] // wk
    nt = wk // LANES
    p0 = q_start + qb * tq
    pos = p0 + lax.broadcasted_iota(I32, (tq, 1), 0)
    vis_end = (pos // CHUNK + 1) * CHUNK
    vis_max = jnp.minimum(((p0 + tq - 1) // CHUNK + 1) * CHUNK, s_len)
    nkb = (vis_max + wk - 1) // wk

    wi = sm_ref[0][:, SM_WI:SM_WI + IDX_HEADS] * IDX_HEADS ** -0.5
    for hh in range(IDX_HEADS):
        w_ref[hh] = jnp.broadcast_to(wi[:, hh:hh + 1], (tq, LANES))
    lane_k = lax.broadcasted_iota(I32, (1, LANES), 1)

    def to_key(score):
        bits = lax.bitcast_convert_type(score, I32)
        return jnp.where(bits < 0, INT_MIN - bits, bits)

    def score_block(j, top, masked):
        m1, m2 = top
        k0 = pl.multiple_of(j * wk, wk)
        for c in range(wk // _SCORE_COLS):
            c0 = k0 + c * _SCORE_COLS
            kt = kit_ref[0, :, pl.ds(c0, _SCORE_COLS)]
            d = jnp.maximum(jnp.dot(qi_ref[0, 0], kt, preferred_element_type=F32), 0.0)
            for u in range(_SCORE_COLS // LANES):
                sc = jnp.zeros((tq, LANES), F32)
                for hh in range(IDX_HEADS):
                    sc = sc + w_ref[hh] * d[hh * tq:(hh + 1) * tq, u * LANES:(u + 1) * LANES]
                if masked:
                    key_pos = c0 + u * LANES + lane_k
                    sc = jnp.where((key_pos < vis_end) & (key_pos < s_len), sc, -jnp.inf)
                key_ref[:, pl.ds(c0 + u * LANES, LANES)] = to_key(sc)
                m2 = jnp.maximum(m2, jnp.minimum(m1, sc))
                m1 = jnp.maximum(m1, sc)
        return m1, m2

    n_open = jnp.minimum(p0, s_len) // wk
    top = (jnp.full((tq, LANES), -jnp.inf, F32),) * 2
    top = lax.fori_loop(0, n_open, functools.partial(score_block, masked=False), top)
    m1, m2 = lax.fori_loop(n_open, nkb, functools.partial(score_block, masked=True), top)

    def count_ge(thr):
        thr_b = jnp.broadcast_to(thr, (tq, LANES))

        def body(j, acc):
            k0 = pl.multiple_of(j * wk, wk)
            for u in range(nt):
                acc = acc + jnp.where(key_ref[:, pl.ds(k0 + u * LANES, LANES)] >= thr_b, 1.0, 0.0)
            return acc

        acc = lax.fori_loop(0, nkb, body, jnp.zeros((tq, LANES), F32))
        return jnp.sum(acc, axis=1, keepdims=True).astype(I32)

    def unfinished(lo, hi, cnt_lo):
        return (cnt_lo > n_sel) & (hi - 1 > lo)

    def bisect(lo, hi, cnt_lo):
        def cond(st):
            return (st[0] < _MAX_BISECTIONS) & (st[1] > 0)

        def body(st):
            it, _, lo, hi, cnt_lo = st
            live = unfinished(lo, hi, cnt_lo)
            mid = (lo >> 1) + (hi >> 1) + (lo & hi & 1)
            cnt = count_ge(jnp.where(live, mid, lo))
            up = live & (cnt >= n_sel)
            lo = jnp.where(up, mid, lo)
            cnt_lo = jnp.where(up, cnt, cnt_lo)
            hi = jnp.where(live & (cnt < n_sel), mid, hi)
            return (it + 1, jnp.max(unfinished(lo, hi, cnt_lo).astype(I32)), lo, hi, cnt_lo)

        st0 = (jnp.int32(0), jnp.max(unfinished(lo, hi, cnt_lo).astype(I32)), lo, hi, cnt_lo)
        return lax.while_loop(cond, body, st0)[2:]

    lo = jnp.maximum(to_key(jnp.min(m2, axis=1, keepdims=True)), KEY_NEG_INF + 1)
    hi = to_key(jnp.max(m1, axis=1, keepdims=True)) + 1
    tau, _, cnt_tau = bisect(lo, hi, count_ge(lo))

    tie_rows = cnt_tau > n_sel
    has_ties = jnp.max(tie_rows.astype(I32)) > 0

    @pl.when(jnp.logical_not(has_ties))
    def _():
        tau_b = jnp.broadcast_to(tau, (tq, LANES))

        def body(j, carry):
            k0 = pl.multiple_of(j * wk, wk)
            for u in range(nt):
                sl = pl.ds(k0 + u * LANES, LANES)
                mask_ref[0, :, sl] = (key_ref[:, sl] >= tau_b).astype(I32).astype(mask_ref.dtype)
            return carry
        lax.fori_loop(0, nkb, body, 0)

    @pl.when(has_ties)
    def _():
        upper = (lax.broadcasted_iota(I32, (wk, wk), 0) <= lax.broadcasted_iota(I32, (wk, wk), 1)).astype(MXU_DTYPE)
        need = jnp.where(tie_rows, n_sel - count_ge(tau + 1), wk * n_blocks).astype(F32)

        def body(j, seen):
            k0 = pl.multiple_of(j * wk, wk)
            blk = key_ref[:, pl.ds(k0, wk)]
            eq = blk == tau
            rank = seen + jnp.dot(eq.astype(F32).astype(MXU_DTYPE), upper, preferred_element_type=F32)
            sel = (blk > tau) | (eq & (rank <= need))
            mask_ref[0, :, pl.ds(k0, wk)] = sel.astype(I32).astype(mask_ref.dtype)
            return seen + jnp.sum(eq.astype(F32), axis=1, keepdims=True)
        lax.fori_loop(0, nkb, body, jnp.zeros((tq, 1), F32))

    def zero_body(j, carry):
        k0 = pl.multiple_of(j * wk, wk)
        mask_ref[0, :, pl.ds(k0, wk)] = jnp.zeros((tq, wk), mask_ref.dtype)
        return carry

    lax.fori_loop(nkb, n_blocks, zero_body, 0)


def _select(qi, sm, kit, tq, wk, s_len, q_start, n_sel):
    b, t, _ = qi.shape
    s_pad = kit.shape[2]
    assert n_sel <= 2 * LANES, "the starting bracket takes two candidates per lane column"
    kern = functools.partial(_select_kernel, tq=tq, wk=wk, s_len=s_len, q_start=q_start, n_sel=n_sel)
    qi = qi.reshape(b, t // tq, tq, IDX_HEADS, IDX_DIM).transpose(0, 1, 3, 2, 4).reshape(b, t // tq, IDX_HEADS * tq, IDX_DIM)
    return pl.pallas_call(
        kern,
        grid=(b, t // tq),
        in_specs=[pl.BlockSpec((1, 1, IDX_HEADS * tq, IDX_DIM), lambda bi, qi_: (bi, qi_, 0, 0)),
                  pl.BlockSpec((1, tq, LANES), lambda bi, qi_: (bi, qi_, 0)),
                  _per_batch((1, IDX_DIM, s_pad), lambda bi, qi_: (bi, 0, 0), t // tq)],
        out_specs=pl.BlockSpec((1, tq, s_pad), lambda bi, qi_: (bi, qi_, 0)),
        out_shape=jax.ShapeDtypeStruct((b, t, s_pad), jnp.int8),
        scratch_shapes=[pltpu.VMEM((tq, s_pad), I32),
                        pltpu.VMEM((IDX_HEADS, tq, LANES), F32)],
        compiler_params=_cparams(2),
        name="select",
    )(qi, sm, kit)


_V_AUG = 2 * HEAD_DIM
_V_ROWS = 512
_MID_TILES = (1024, 512)


def _attn_kernel(q_ref, kt_ref, v_ref, mask_ref, bias_ref, o_ref, m_ref, acc_ref, qg_ref, vaug_ref,
                 *, tq, tk, far_widths, q_start):
    qb = pl.program_id(1)
    jd = (q_start + qb * tq) // tk

    @pl.when(qb == 0)
    def _():
        one_hot = (lax.broadcasted_iota(I32, (_V_ROWS, _V_AUG - HEAD_DIM), 1) == 0).astype(vaug_ref.dtype)

        def body(i, carry):
            r0 = pl.multiple_of(i * _V_ROWS, _V_ROWS)
            blk = v_ref[0, pl.ds(r0, _V_ROWS), :]
            for g in range(N_KV_HEADS):
                vaug_ref[pl.ds(r0, _V_ROWS), g * _V_AUG:g * _V_AUG + HEAD_DIM] = blk[:, g * HEAD_DIM:(g + 1) * HEAD_DIM]
                vaug_ref[pl.ds(r0, _V_ROWS), g * _V_AUG + HEAD_DIM:(g + 1) * _V_AUG] = one_hot
            return carry

        lax.fori_loop(0, vaug_ref.shape[0] // _V_ROWS, body, 0)

    for g in range(N_KV_HEADS):
        for r in range(Q_PER_KV):
            c0 = (g * Q_PER_KV + r) * HEAD_DIM
            qg_ref[g, r * tq:(r + 1) * tq, :] = q_ref[0, :, c0:c0 + HEAD_DIM]
    m_ref[...] = jnp.full(m_ref.shape, NEG_BIG, F32)
    acc_ref[...] = jnp.zeros(acc_ref.shape, F32)

    def process(k0, width, near):
        nt = width // LANES
        mb = jnp.where(mask_ref[0, :, pl.ds(k0, width)].astype(I32) != 0, 0.0, NEG_BIG)
        mb = jnp.concatenate([mb] * Q_PER_KV, axis=0)
        for g in range(N_KV_HEADS):
            kt = kt_ref[0, g * HEAD_DIM:(g + 1) * HEAD_DIM, pl.ds(k0, width)]
            s = jnp.dot(qg_ref[g], kt, preferred_element_type=F32) + mb
            if near is not None:
                s = s + bias_ref[near, g]
            tiles = [s[:, u * LANES:(u + 1) * LANES] for u in range(nt)]
            mx = tiles[0]
            for u in range(1, nt):
                mx = jnp.maximum(mx, tiles[u])
            m_old = m_ref[g]
            m_new = jnp.maximum(m_old, jnp.max(mx, axis=1, keepdims=True))
            alpha = jnp.exp(m_old - m_new)
            p = jnp.concatenate([jnp.exp(tl - m_new).astype(MXU_DTYPE) for tl in tiles], axis=1)
            vt = vaug_ref[pl.ds(k0, width), g * _V_AUG:(g + 1) * _V_AUG]
            acc_ref[g] = alpha * acc_ref[g] + jnp.dot(p, vt, preferred_element_type=F32)
            m_ref[g] = m_new

    far_keys = jnp.maximum(jd - 1, 0) * tk
    done = 0
    for width in far_widths:
        steps = (far_keys - done) // width

        def far_body(i, carry, width=width, done=done):
            process(pl.multiple_of(done + i * width, width), width, None)
            return carry

        lax.fori_loop(0, steps, far_body, 0)
        done = done + steps * width

    @pl.when(jd >= 1)
    def _():
        process(pl.multiple_of((jd - 1) * tk, tk), tk, 0)

    process(pl.multiple_of(jd * tk, tk), tk, 1)

    for g in range(N_KV_HEADS):
        acc = acc_ref[g]
        o = acc[:, :HEAD_DIM] / acc[:, HEAD_DIM:HEAD_DIM + 1]
        for r in range(Q_PER_KV):
            c0 = (g * Q_PER_KV + r) * HEAD_DIM
            o_ref[0, :, c0:c0 + HEAD_DIM] = o[r * tq:(r + 1) * tq].astype(o_ref.dtype)


def _attention(q, kt, v, mask, bias, tq, tk, wide, q_start):
    b, t, _ = q.shape
    s_pad = kt.shape[2]
    assert s_pad % _V_ROWS == 0
    rows = Q_PER_KV * tq
    far_widths = tuple(sorted({wide, tk} | {w for w in _MID_TILES if tk < w < wide}, reverse=True))
    assert all(a % b == 0 for a, b in zip(far_widths, far_widths[1:]))
    kern = functools.partial(_attn_kernel, tq=tq, tk=tk, far_widths=far_widths, q_start=q_start)
    return pl.pallas_call(
        kern,
        grid=(b, t // tq),
        in_specs=[pl.BlockSpec((1, tq, N_HEADS * HEAD_DIM), lambda bi, qi_: (bi, qi_, 0)),
                  _per_batch((1, KV_DIM, s_pad), lambda bi, qi_: (bi, 0, 0), t // tq),
                  _per_batch((1, s_pad, KV_DIM), lambda bi, qi_: (bi, 0, 0), t // tq),
                  pl.BlockSpec((1, tq, s_pad), lambda bi, qi_: (bi, qi_, 0)),
                  _resident((2, N_KV_HEADS, rows, tk), lambda bi, qi_: (0, 0, 0, 0))],
        out_specs=pl.BlockSpec((1, tq, N_HEADS * HEAD_DIM), lambda bi, qi_: (bi, qi_, 0)),
        out_shape=jax.ShapeDtypeStruct((b, t, N_HEADS * HEAD_DIM), MXU_DTYPE),
        scratch_shapes=[pltpu.VMEM((N_KV_HEADS, rows, LANES), F32),
                        pltpu.VMEM((N_KV_HEADS, rows, _V_AUG), F32),
                        pltpu.VMEM((N_KV_HEADS, rows, HEAD_DIM), MXU_DTYPE),
                        pltpu.VMEM((s_pad, N_KV_HEADS * _V_AUG), MXU_DTYPE)],
        compiler_params=_cparams(2),
        name="attention",
    )(q, kt, v, mask, bias)


def _layer_norm(x, g, b):
    mu = jnp.mean(x, axis=-1, keepdims=True)
    xc = x - mu
    var = jnp.mean(xc * xc, axis=-1, keepdims=True)
    return xc * lax.rsqrt(var + LN_EPS) * g + b


def _post_kernel(o_ref, yn_ref, gs_ref, ga_ref, x_ref, wso_ref, wao_ref, wout_ref, wup_ref, wdown_ref,
                 ln_ref, out_ref):
    y1 = jnp.dot(yn_ref[...], wso_ref[...], preferred_element_type=F32)
    y2 = jnp.dot(o_ref[...], wao_ref[...], preferred_element_type=F32)
    mix = _sigmoid(gs_ref[...]) * y1 + _sigmoid(ga_ref[...]) * y2
    mixed = _dot(mix, wout_ref[...])
    h = _layer_norm(ALPHA * x_ref[...] + mixed, ln_ref[0:1, :], ln_ref[1:2, :])
    u = jnp.maximum(_dot(h, wup_ref[...]), 0.0)
    f = _dot(u * u, wdown_ref[...])
    out_ref[...] = _layer_norm(ALPHA * h + f, ln_ref[2:3, :], ln_ref[3:4, :])


def _post(o, yn, gs, ga, x2d, w_ssd_o, w_attn_o, w_out, w_up, w_down, ln, tm):
    m = x2d.shape[0]
    row = lambda w: pl.BlockSpec((tm, w), lambda i: (i, 0))
    const = lambda i: (0, 0)
    return pl.pallas_call(
        _post_kernel,
        grid=(m // tm,),
        in_specs=[row(N_HEADS * HEAD_DIM), row(SSD_INNER), row(D_MODEL), row(D_MODEL), row(D_MODEL),
                  _resident((SSD_INNER, D_MODEL), const),
                  _resident((N_HEADS * HEAD_DIM, D_MODEL), const),
                  _resident((D_MODEL, D_MODEL), const),
                  _resident((D_MODEL, D_FF), const),
                  _resident((D_FF, D_MODEL), const),
                  _resident((4, D_MODEL), const)],
        out_specs=row(D_MODEL),
        out_shape=jax.ShapeDtypeStruct((m, D_MODEL), F32),
        compiler_params=_cparams(1),
        name="post",
    )(o, yn, gs, ga, x2d, w_ssd_o, w_attn_o, w_out, w_up, w_down, ln)


def _round_up(n, m):
    return (n + m - 1) // m * m


def _layer(x, cache_k, cache_v, cache_kidx, state_ssm, state_conv, bias_tiles, w_prep, conv_w, conv_b,
           dt_bias, a_log, d_skip, ssd_norm_w, w_ssd_o, w_attn_o, w_out, w_up, w_down, ln,
           *, tq, tk, wide, tqs, wk, lc, tm):
    b, t, _ = x.shape
    past = cache_k.shape[1]
    s_len = past + t
    m = b * t
    assert m % tm == 0 and t % tq == 0 and past % tk == 0 and tk % LANES == 0 and wide % tk == 0
    assert t % tqs == 0 and wk % LANES == 0
    assert tq == tk or t == tq, "every query tile must start on a key-tile boundary"
    assert t % lc == 0 or t < lc
    x2d = x.reshape(m, D_MODEL)

    z, xbc, q, k, v, qi, gs, ga, sm = _inproj(x2d, w_prep, tm)

    t_pad = _round_up(t, lc)
    pad_t = lambda a: jnp.pad(a.reshape(b, t, a.shape[-1]), ((0, 0), (0, t_pad - t), (0, 0)))
    sm3 = pad_t(sm)
    dt_t = sm3[:, :, SM_DT:SM_DT + SSD_HEADS].transpose(0, 2, 1)
    yn, ssm_new = _ssd(pad_t(xbc), pad_t(z), sm3, dt_t, state_conv.astype(F32),
                       _state_to_kernel_layout(state_ssm.astype(F32)), conv_w, conv_b, dt_bias, a_log, d_skip,
                       ssd_norm_w, lc, min(t, lc))
    yn = yn[:, :t].reshape(m, SSD_INNER)
    new_ssm = _state_from_kernel_layout(ssm_new)
    xbc3 = xbc.reshape(b, t, CONV_DIM)
    new_conv = jnp.concatenate([state_conv.astype(F32), xbc3], axis=1)[:, -_HALO:]

    k3 = k.reshape(b, t, KV_DIM)
    v3 = v.reshape(b, t, KV_DIM)
    ki3 = sm.reshape(b, t, LANES)[:, :, SM_KI:SM_KI + IDX_DIM]
    s_pad = _round_up(s_len, max(wide, wk))
    assert s_pad % wide == 0 and s_pad % wk == 0
    pad_s = lambda a: jnp.pad(a, ((0, 0), (0, s_pad - s_len)) + ((0, 0),) * (a.ndim - 2))
    k_all = pad_s(jnp.concatenate([cache_k.reshape(b, past, KV_DIM).astype(MXU_DTYPE), k3.astype(MXU_DTYPE)], axis=1))
    v_all = pad_s(jnp.concatenate([cache_v.reshape(b, past, KV_DIM).astype(MXU_DTYPE), v3.astype(MXU_DTYPE)], axis=1))
    ki_all = pad_s(jnp.concatenate([cache_kidx.astype(MXU_DTYPE), ki3.astype(MXU_DTYPE)], axis=1))
    n_sel = min(TOPK_MAX, s_len // 4)
    mask = _select(qi.reshape(b, t, -1), sm.reshape(b, t, LANES), ki_all.transpose(0, 2, 1),
                   tqs, wk, s_len, past, n_sel)
    bias = bias_tiles[:, :, :tq, :].reshape(2, N_KV_HEADS, Q_PER_KV * tq, tk)
    o = _attention(q.reshape(b, t, -1), k_all.transpose(0, 2, 1), v_all, mask, bias, tq, tk, wide, past)

    out = _post(o.reshape(m, -1), yn, gs, ga, x2d, w_ssd_o, w_attn_o, w_out, w_up, w_down, ln, tm)
    return out.reshape(b, t, D_MODEL), (k3.reshape(b, t, N_KV_HEADS, HEAD_DIM), v3.reshape(b, t, N_KV_HEADS, HEAD_DIM),
                                        ki3, new_ssm, new_conv)


ATTN_TILE = 128
WIDE_TILE = 2048
WIDE_TILE_SHORT = 512
SELECT_ROWS = 128
SELECT_WIDTH = 1024
SSD_CHUNK = 128
ROW_TILE = 256


def kernel(x_prompt, x_sample, cache_k, cache_v, cache_kidx, state_ssm, state_conv, rel_bias, w_in, conv_w, conv_b,
           dt_bias, a_log, d_skip, ssd_norm_w, w_ssd_o, w_attn_o, w_out, ln1_g, ln1_b, w_up, w_down, ln2_g, ln2_b):
    bp = x_prompt.shape[0]
    dtype = x_prompt.dtype
    bias_tiles = _bias_tiles(rel_bias, ATTN_TILE)
    yp, ys = x_prompt, x_sample
    st_p, st_s = [], []
    tiles = dict(tk=ATTN_TILE, wk=SELECT_WIDTH, lc=SSD_CHUNK, tm=ROW_TILE)
    for l in range(DEPTH):
        w_prep = _prep_w_in(w_in[l])
        ln = jnp.stack([ln1_g[l], ln1_b[l], ln2_g[l], ln2_b[l]]).astype(F32)
        weights = (bias_tiles, w_prep, conv_w[l], conv_b[l], dt_bias[l], a_log[l], d_skip[l], ssd_norm_w[l],
                   w_ssd_o[l].astype(MXU_DTYPE), w_attn_o[l].astype(MXU_DTYPE), w_out[l].astype(MXU_DTYPE),
                   w_up[l].astype(MXU_DTYPE), w_down[l].astype(MXU_DTYPE), ln)
        yp, sp = _layer(yp, jnp.zeros((bp, 0, N_KV_HEADS, HEAD_DIM), dtype), jnp.zeros((bp, 0, N_KV_HEADS, HEAD_DIM), dtype),
                        jnp.zeros((bp, 0, IDX_DIM), dtype), jnp.zeros((bp, SSD_HEADS, SSD_HEAD_DIM, D_STATE), dtype),
                        jnp.zeros((bp, _HALO, CONV_DIM), dtype), *weights,
                        tq=min(ATTN_TILE, yp.shape[1]), tqs=min(SELECT_ROWS, yp.shape[1]), wide=WIDE_TILE, **tiles)
        ys, ss = _layer(ys, cache_k[l], cache_v[l], cache_kidx[l], state_ssm[l], state_conv[l], *weights,
                        tq=min(ATTN_TILE, ys.shape[1]), tqs=min(SELECT_ROWS, ys.shape[1]), wide=WIDE_TILE_SHORT, **tiles)
        st_p.append(sp)
        st_s.append(ss)
    stack = lambda sts, i: jnp.stack([s[i] for s in sts])
    return (yp, ys) + tuple(stack(st_p, i) for i in range(5)) + tuple(stack(st_s, i) for i in range(5))
```

```python
import functools

import jax
import jax.numpy as jnp
from jax import lax
from jax.experimental import pallas as pl
from jax.experimental.pallas import tpu as pltpu

F32 = jnp.float32
I32 = jnp.int32
MXU_DTYPE = jnp.bfloat16

D_MODEL = 1024
CHUNK = 64
SSD_INNER = 2048
SSD_HEAD_DIM = 64
SSD_HEADS = 32
SSD_GROUPS = 4
HEADS_PER_GROUP = SSD_HEADS // SSD_GROUPS
D_STATE = 128
CONV_W = 4
CONV_DIM = SSD_INNER + 2 * SSD_GROUPS * D_STATE
N_HEADS = 16
N_KV_HEADS = 4
Q_PER_KV = N_HEADS // N_KV_HEADS
HEAD_DIM = 64
KV_DIM = N_KV_HEADS * HEAD_DIM
IDX_HEADS = 8
IDX_DIM = 64
TOPK_MAX = 256
REL_BUCKETS = 32
REL_MAX_DIST = 128
D_FF = 4 * D_MODEL
DEPTH = 1
ALPHA = (2.0 * DEPTH) ** 0.25
LN_EPS = 1e-5
RMS_EPS = 1e-5
IN_SIZES = (SSD_INNER, CONV_DIM, SSD_HEADS, N_HEADS * HEAD_DIM, KV_DIM, KV_DIM,
            IDX_HEADS * IDX_DIM, IDX_DIM, IDX_HEADS, D_MODEL, D_MODEL)

LANES = 128
VMEM_LIMIT_BYTES = 56 * 1024 * 1024

SM_KI = 0
SM_DT = IDX_DIM
SM_WI = IDX_DIM + SSD_HEADS
SM_PAD = LANES - (IDX_DIM + SSD_HEADS + IDX_HEADS)

NEG_BIG = -1e30
INT_MIN = -2 ** 31
INT_MAX = 2 ** 31 - 1
KEY_NEG_INF = -0x7F800000


def _cparams(n_axes):
    return pltpu.CompilerParams(dimension_semantics=("arbitrary",) * n_axes,
                                vmem_limit_bytes=VMEM_LIMIT_BYTES)


def _resident(shape, index_map):
    return pl.BlockSpec(shape, index_map, pipeline_mode=pl.Buffered(1))


def _per_batch(shape, index_map, steps_per_batch):
    if steps_per_batch > 1:
        return _resident(shape, index_map)
    return pl.BlockSpec(shape, index_map)


def _dot(a, b):
    return jnp.dot(a.astype(MXU_DTYPE), b.astype(MXU_DTYPE), preferred_element_type=F32)


def _dot_exact(a, b):
    return jnp.dot(a, b, preferred_element_type=F32, precision=lax.Precision.HIGHEST)


def _sigmoid(x):
    return 0.5 * jnp.tanh(0.5 * x) + 0.5


def _softplus(x):
    return jnp.maximum(x, 0.0) + jnp.log1p(jnp.exp(-jnp.abs(x)))


_PROJ_WIDTHS = (SSD_INNER, CONV_DIM, N_HEADS * HEAD_DIM, KV_DIM, KV_DIM, IDX_HEADS * IDX_DIM,
                D_MODEL, D_MODEL, LANES)
_PROJ_WIDTH = sum(_PROJ_WIDTHS)
_PROJ_CHUNK = 512


def _prep_w_in(w_in):
    offs = [0]
    for s in IN_SIZES:
        offs.append(offs[-1] + s)
    z, xbc, dt, q, k, v, qi, ki, wi, gs, ga = [w_in[:, offs[i]:offs[i + 1]] for i in range(len(IN_SIZES))]
    small = jnp.concatenate([ki, dt, wi, jnp.zeros((D_MODEL, SM_PAD), w_in.dtype)], axis=1)
    w = jnp.concatenate([z, xbc, q * HEAD_DIM ** -0.5, k, v, qi * IDX_DIM ** -0.5, gs, ga, small], axis=1)
    return w.astype(MXU_DTYPE)


def _inproj_kernel(x_ref, w_ref, z_ref, xbc_ref, q_ref, k_ref, v_ref, qi_ref, gs_ref, ga_ref, sm_ref):
    xb = x_ref[...].astype(MXU_DTYPE)
    outs = (z_ref, xbc_ref, q_ref, k_ref, v_ref, qi_ref, gs_ref, ga_ref, sm_ref)
    c0 = 0
    for ref, width in zip(outs, _PROJ_WIDTHS):
        for o in range(0, width, _PROJ_CHUNK):
            n = min(_PROJ_CHUNK, width - o)
            r = jnp.dot(xb, w_ref[:, c0 + o:c0 + o + n], preferred_element_type=F32)
            ref[:, o:o + n] = r.astype(ref.dtype)
        c0 += width


def _inproj(x2d, w_prep, tm):
    m = x2d.shape[0]
    dtypes = (F32, F32, MXU_DTYPE, F32, F32, MXU_DTYPE, F32, F32, F32)
    out_shape = tuple(jax.ShapeDtypeStruct((m, w), d) for w, d in zip(_PROJ_WIDTHS, dtypes))
    out_specs = tuple(pl.BlockSpec((tm, w), lambda i: (i, 0)) for w in _PROJ_WIDTHS)
    return pl.pallas_call(
        _inproj_kernel,
        grid=(m // tm,),
        in_specs=[pl.BlockSpec((tm, D_MODEL), lambda i: (i, 0)),
                  _resident((D_MODEL, _PROJ_WIDTH), lambda i: (0, 0))],
        out_specs=out_specs,
        out_shape=out_shape,
        compiler_params=_cparams(1),
        name="inproj",
    )(x2d, w_prep)


_HALO = CONV_W - 1
_XP_BASE = 8


def _ssd_kernel(xbc_ref, z_ref, sm_ref, dtT_ref, conv0_ref, ssm0_ref, convw_ref, convb_ref,
                hp_ref, hpT_ref, dfull_ref, nw_ref,
                yn_ref, ssm_out_ref, xp_ref, st_ref, y_ref, *, lc, t_valid):
    c = pl.program_id(1)
    nc = pl.num_programs(1)

    @pl.when(c == 0)
    def _():
        xp_ref[_XP_BASE - _HALO:_XP_BASE, :] = conv0_ref[0]
        st_ref[...] = ssm0_ref[0]

    xp_ref[_XP_BASE:_XP_BASE + lc, :] = xbc_ref[0]
    staged = xp_ref[...]
    conv = convb_ref[...]
    for i in range(CONV_W):
        back = _HALO - i
        shifted = pltpu.roll(staged, back, axis=0) if back else staged
        conv = conv + shifted[_XP_BASE:, :] * convw_ref[i:i + 1, :]
    xp_ref[_XP_BASE - _HALO:_XP_BASE, :] = xp_ref[_XP_BASE + lc - _HALO:_XP_BASE + lc, :]
    act = conv * _sigmoid(conv)
    xs = act[:, :SSD_INNER]
    bm = act[:, SSD_INNER:SSD_INNER + SSD_GROUPS * D_STATE]
    cm = act[:, SSD_INNER + SSD_GROUPS * D_STATE:]

    dt = _softplus(sm_ref[0][:, SM_DT:SM_DT + SSD_HEADS] + hp_ref[0:1, :])
    dt_t = _softplus(dtT_ref[0] + hpT_ref[:, 0:1])
    if t_valid < lc:
        dt = jnp.where(lax.broadcasted_iota(I32, (lc, SSD_HEADS), 0) < t_valid, dt, 0.0)
        dt_t = jnp.where(lax.broadcasted_iota(I32, (SSD_HEADS, lc), 1) < t_valid, dt_t, 0.0)
    a = dt * (-jnp.exp(hp_ref[1:2, :]))
    a_t = dt_t * (-jnp.exp(hpT_ref[:, 1:2]))
    ri = lax.broadcasted_iota(I32, (lc, lc), 0)
    ci = lax.broadcasted_iota(I32, (lc, lc), 1)
    causal = ci <= ri
    acum = _dot_exact(causal.astype(F32), a)
    acum_t = _dot_exact(a_t, (ri <= ci).astype(F32))
    e_a = jnp.exp(acum)
    w_end = jnp.exp(acum[lc - 1:lc, :] - acum) * dt
    lane = lax.broadcasted_iota(I32, (lc, 2 * SSD_HEAD_DIM), 1)
    first_head = lane < SSD_HEAD_DIM
    gw = HEADS_PER_GROUP * SSD_HEAD_DIM

    def per_head_lanes(v, h0):
        return jnp.where(first_head, v[:, h0:h0 + 1], v[:, h0 + 1:h0 + 2])

    for g in range(SSD_GROUPS):
        b_g = bm[:, g * D_STATE:(g + 1) * D_STATE]
        c_g = cm[:, g * D_STATE:(g + 1) * D_STATE].astype(MXU_DTYPE)
        b_gt = b_g.T.astype(MXU_DTYPE)
        cb = jnp.dot(c_g, b_gt, preferred_element_type=F32)
        state = st_ref[g]
        y_off = _dot(c_g, state)
        xw, e_last = [], []
        for jp in range(HEADS_PER_GROUP // 2):
            h0 = g * HEADS_PER_GROUP + 2 * jp
            ms = []
            for h in (h0, h0 + 1):
                seg = acum[:, h:h + 1] - acum_t[h:h + 1, :]
                decay = jnp.exp(jnp.where(causal, seg, -jnp.inf))
                ms.append((cb * decay * dt_t[h:h + 1, :]).astype(MXU_DTYPE))
            c0 = h0 * SSD_HEAD_DIM
            x_pair = xs[:, c0:c0 + 2 * SSD_HEAD_DIM]
            r = jnp.dot(jnp.concatenate(ms, axis=0), x_pair.astype(MXU_DTYPE), preferred_element_type=F32)
            y_diag = jnp.where(first_head, r[:lc], r[lc:])
            o0 = 2 * jp * SSD_HEAD_DIM
            e_pair = per_head_lanes(e_a, h0)
            y_ref[:, c0:c0 + 2 * SSD_HEAD_DIM] = (y_diag + y_off[:, o0:o0 + 2 * SSD_HEAD_DIM] * e_pair
                                                  + dfull_ref[:, c0:c0 + 2 * SSD_HEAD_DIM] * x_pair)
            xw.append((x_pair * per_head_lanes(w_end, h0)).astype(MXU_DTYPE))
            e_last.append(e_pair[lc - 1:lc, :])
        st_ref[g] = (state * jnp.concatenate(e_last, axis=1)
                     + jnp.dot(b_gt, jnp.concatenate(xw, axis=1), preferred_element_type=F32))

    zv = z_ref[0]
    hg = y_ref[...] * (zv * _sigmoid(zv))
    for g in range(SSD_GROUPS):
        sl = hg[:, g * gw:(g + 1) * gw]
        ms = jnp.mean(sl * sl, axis=-1, keepdims=True)
        yn_ref[0, :, g * gw:(g + 1) * gw] = (sl * lax.rsqrt(ms + RMS_EPS) * nw_ref[:, g * gw:(g + 1) * gw]).astype(yn_ref.dtype)

    @pl.when(c == nc - 1)
    def _():
        ssm_out_ref[0] = st_ref[...]


def _ssd(xbc, z, sm, dt_t, conv0, ssm0_t, conv_w, conv_b, dt_bias, a_log, d_skip, norm_w, lc, t_valid):
    b, t, _ = xbc.shape
    nc = t // lc
    hp = jnp.stack([dt_bias, a_log]).astype(F32)
    d_full = jnp.repeat(d_skip.astype(F32), SSD_HEAD_DIM)[None, :]
    gw = HEADS_PER_GROUP * SSD_HEAD_DIM
    const2 = lambda bi, ci: (0, 0)
    kern = functools.partial(_ssd_kernel, lc=lc, t_valid=t_valid)
    return pl.pallas_call(
        kern,
        grid=(b, nc),
        in_specs=[pl.BlockSpec((1, lc, CONV_DIM), lambda bi, ci: (bi, ci, 0)),
                  pl.BlockSpec((1, lc, SSD_INNER), lambda bi, ci: (bi, ci, 0)),
                  pl.BlockSpec((1, lc, LANES), lambda bi, ci: (bi, ci, 0)),
                  pl.BlockSpec((1, SSD_HEADS, lc), lambda bi, ci: (bi, 0, ci)),
                  pl.BlockSpec((1, _HALO, CONV_DIM), lambda bi, ci: (bi, 0, 0)),
                  pl.BlockSpec((1, SSD_GROUPS, D_STATE, gw), lambda bi, ci: (bi, 0, 0, 0)),
                  pl.BlockSpec((CONV_W, CONV_DIM), const2),
                  pl.BlockSpec((1, CONV_DIM), const2),
                  pl.BlockSpec((2, SSD_HEADS), const2),
                  pl.BlockSpec((SSD_HEADS, 2), const2),
                  pl.BlockSpec((1, SSD_INNER), const2),
                  pl.BlockSpec((1, SSD_INNER), const2)],
        out_specs=(pl.BlockSpec((1, lc, SSD_INNER), lambda bi, ci: (bi, ci, 0)),
                   pl.BlockSpec((1, SSD_GROUPS, D_STATE, gw), lambda bi, ci: (bi, 0, 0, 0))),
        out_shape=(jax.ShapeDtypeStruct((b, t, SSD_INNER), MXU_DTYPE),
                   jax.ShapeDtypeStruct((b, SSD_GROUPS, D_STATE, gw), F32)),
        scratch_shapes=[pltpu.VMEM((_XP_BASE + lc, CONV_DIM), F32),
                        pltpu.VMEM((SSD_GROUPS, D_STATE, gw), F32),
                        pltpu.VMEM((lc, SSD_INNER), F32)],
        compiler_params=_cparams(2),
        name="ssd",
    )(xbc, z, sm, dt_t, conv0, ssm0_t, conv_w.astype(F32), conv_b.astype(F32)[None, :], hp, hp.T,
      d_full, norm_w.astype(F32)[None, :])


def _state_to_kernel_layout(s):
    b = s.shape[0]
    s = s.reshape(b, SSD_GROUPS, HEADS_PER_GROUP, SSD_HEAD_DIM, D_STATE)
    return s.transpose(0, 1, 4, 2, 3).reshape(b, SSD_GROUPS, D_STATE, HEADS_PER_GROUP * SSD_HEAD_DIM)


def _state_from_kernel_layout(s):
    b = s.shape[0]
    s = s.reshape(b, SSD_GROUPS, D_STATE, HEADS_PER_GROUP, SSD_HEAD_DIM)
    return s.transpose(0, 1, 3, 4, 2).reshape(b, SSD_HEADS, SSD_HEAD_DIM, D_STATE)


_T5_LOG_STARTS = (12, 16, 23, 32, 46, 64, 91)
_T5_FAR_BUCKET = REL_BUCKETS // 2 - 1


def _bias_kernel(rb_ref, out_ref, *, tk):
    half = REL_BUCKETS // 2
    max_exact = half // 2
    ri = lax.broadcasted_iota(I32, (tk, tk), 0)
    ci = lax.broadcasted_iota(I32, (tk, tk), 1)
    for t in range(2):
        rel = ci - ri - (1 - t) * tk
        n = jnp.abs(rel)
        large = jnp.full((tk, tk), max_exact, I32)
        for start in _T5_LOG_STARTS:
            large = large + (n >= start).astype(I32)
        bucket = jnp.where(rel > 0, half, 0) + jnp.where(n < max_exact, n, large)
        for h in range(N_HEADS):
            acc = jnp.zeros((tk, tk), F32)
            for bkt in range(REL_BUCKETS):
                acc = jnp.where(bucket == bkt, rb_ref[bkt, h], acc)
            out_ref[t, h] = acc - rb_ref[_T5_FAR_BUCKET, h]


def _bias_tiles(rel_bias, tk):
    return pl.pallas_call(
        functools.partial(_bias_kernel, tk=tk),
        in_specs=[pl.BlockSpec(memory_space=pltpu.SMEM)],
        out_specs=pl.BlockSpec(memory_space=pltpu.VMEM),
        out_shape=jax.ShapeDtypeStruct((2, N_HEADS, tk, tk), F32),
        name="bias_tiles",
    )(rel_bias.astype(F32))


_MAX_BISECTIONS = 40
_SCORE_COLS = 256


def _select_kernel(qi_ref, sm_ref, kit_ref, mask_ref, key_ref, w_ref, *, tq, wk, s_len, q_start, n_sel):
    qb = pl.program_id(1)
    n_blocks = mask_ref.shape[2] // wk
    nt = wk // LANES
    p0 = q_start + qb * tq
    pos = p0 + lax.broadcasted_iota(I32, (tq, 1), 0)
    vis_end = (pos // CHUNK + 1) * CHUNK
    vis_max = jnp.minimum(((p0 + tq - 1) // CHUNK + 1) * CHUNK, s_len)
    nkb = (vis_max + wk - 1) // wk

    wi = sm_ref[0][:, SM_WI:SM_WI + IDX_HEADS] * IDX_HEADS ** -0.5
    for hh in range(IDX_HEADS):
        w_ref[hh] = jnp.broadcast_to(wi[:, hh:hh + 1], (tq, LANES))
    lane_k = lax.broadcasted_iota(I32, (1, LANES), 1)

    def to_key(score):
        bits = lax.bitcast_convert_type(score, I32)
        return jnp.where(bits < 0, INT_MIN - bits, bits)

    def score_block(j, top, masked):
        m1, m2 = top
        k0 = pl.multiple_of(j * wk, wk)
        for c in range(wk // _SCORE_COLS):
            c0 = k0 + c * _SCORE_COLS
            kt = kit_ref[0, :, pl.ds(c0, _SCORE_COLS)]
            d = jnp.maximum(jnp.dot(qi_ref[0, 0], kt, preferred_element_type=F32), 0.0)
            for u in range(_SCORE_COLS // LANES):
                sc = jnp.zeros((tq, LANES), F32)
                for hh in range(IDX_HEADS):
                    sc = sc + w_ref[hh] * d[hh * tq:(hh + 1) * tq, u * LANES:(u + 1) * LANES]
                if masked:
                    key_pos = c0 + u * LANES + lane_k
                    sc = jnp.where((key_pos < vis_end) & (key_pos < s_len), sc, -jnp.inf)
                key_ref[:, pl.ds(c0 + u * LANES, LANES)] = to_key(sc)
                m2 = jnp.maximum(m2, jnp.minimum(m1, sc))
                m1 = jnp.maximum(m1, sc)
        return m1, m2

    n_open = jnp.minimum(p0, s_len) // wk
    top = (jnp.full((tq, LANES), -jnp.inf, F32),) * 2
    top = lax.fori_loop(0, n_open, functools.partial(score_block, masked=False), top)
    m1, m2 = lax.fori_loop(n_open, nkb, functools.partial(score_block, masked=True), top)

    def count_ge(thr):
        thr_b = jnp.broadcast_to(thr, (tq, LANES))

        def body(j, acc):
            k0 = pl.multiple_of(j * wk, wk)
            for u in range(nt):
                acc = acc + jnp.where(key_ref[:, pl.ds(k0 + u * LANES, LANES)] >= thr_b, 1.0, 0.0)
            return acc

        acc = lax.fori_loop(0, nkb, body, jnp.zeros((tq, LANES), F32))
        return jnp.sum(acc, axis=1, keepdims=True).astype(I32)

    def unfinished(lo, hi, cnt_lo):
        return (cnt_lo > n_sel) & (hi - 1 > lo)

    def bisect(lo, hi, cnt_lo):
        def cond(st):
            return (st[0] < _MAX_BISECTIONS) & (st[1] > 0)

        def body(st):
            it, _, lo, hi, cnt_lo = st
            live = unfinished(lo, hi, cnt_lo)
            mid = (lo >> 1) + (hi >> 1) + (lo & hi & 1)
            cnt = count_ge(jnp.where(live, mid, lo))
            up = live & (cnt >= n_sel)
            lo = jnp.where(up, mid, lo)
            cnt_lo = jnp.where(up, cnt, cnt_lo)
            hi = jnp.where(live & (cnt < n_sel), mid, hi)
            return (it + 1, jnp.max(unfinished(lo, hi, cnt_lo).astype(I32)), lo, hi, cnt_lo)

        st0 = (jnp.int32(0), jnp.max(unfinished(lo, hi, cnt_lo).astype(I32)), lo, hi, cnt_lo)
        return lax.while_loop(cond, body, st0)[2:]

    lo = jnp.maximum(to_key(jnp.min(m2, axis=1, keepdims=True)), KEY_NEG_INF + 1)
    hi = to_key(jnp.max(m1, axis=1, keepdims=True)) + 1
    tau, _, cnt_tau = bisect(lo, hi, count_ge(lo))

    tie_rows = cnt_tau > n_sel
    has_ties = jnp.max(tie_rows.astype(I32)) > 0

    @pl.when(jnp.logical_not(has_ties))
    def _():
        tau_b = jnp.broadcast_to(tau, (tq, LANES))

        def body(j, carry):
            k0 = pl.multiple_of(j * wk, wk)
            for u in range(nt):
                sl = pl.ds(k0 + u * LANES, LANES)
                mask_ref[0, :, sl] = (key_ref[:, sl] >= tau_b).astype(I32).astype(mask_ref.dtype)
            return carry
        lax.fori_loop(0, nkb, body, 0)

    @pl.when(has_ties)
    def _():
        upper = (lax.broadcasted_iota(I32, (wk, wk), 0) <= lax.broadcasted_iota(I32, (wk, wk), 1)).astype(MXU_DTYPE)
        need = jnp.where(tie_rows, n_sel - count_ge(tau + 1), wk * n_blocks).astype(F32)

        def body(j, seen):
            k0 = pl.multiple_of(j * wk, wk)
            blk = key_ref[:, pl.ds(k0, wk)]
            eq = blk == tau
            rank = seen + jnp.dot(eq.astype(F32).astype(MXU_DTYPE), upper, preferred_element_type=F32)
            sel = (blk > tau) | (eq & (rank <= need))
            mask_ref[0, :, pl.ds(k0, wk)] = sel.astype(I32).astype(mask_ref.dtype)
            return seen + jnp.sum(eq.astype(F32), axis=1, keepdims=True)
        lax.fori_loop(0, nkb, body, jnp.zeros((tq, 1), F32))

    def zero_body(j, carry):
        k0 = pl.multiple_of(j * wk, wk)
        mask_ref[0, :, pl.ds(k0, wk)] = jnp.zeros((tq, wk), mask_ref.dtype)
        return carry

    lax.fori_loop(nkb, n_blocks, zero_body, 0)


def _select(qi, sm, kit, tq, wk, s_len, q_start, n_sel):
    b, t, _ = qi.shape
    s_pad = kit.shape[2]
    assert n_sel <= 2 * LANES, "the starting bracket takes two candidates per lane column"
    kern = functools.partial(_select_kernel, tq=tq, wk=wk, s_len=s_len, q_start=q_start, n_sel=n_sel)
    qi = qi.reshape(b, t // tq, tq, IDX_HEADS, IDX_DIM).transpose(0, 1, 3, 2, 4).reshape(b, t // tq, IDX_HEADS * tq, IDX_DIM)
    return pl.pallas_call(
        kern,
        grid=(b, t // tq),
        in_specs=[pl.BlockSpec((1, 1, IDX_HEADS * tq, IDX_DIM), lambda bi, qi_: (bi, qi_, 0, 0)),
                  pl.BlockSpec((1, tq, LANES), lambda bi, qi_: (bi, qi_, 0)),
                  _per_batch((1, IDX_DIM, s_pad), lambda bi, qi_: (bi, 0, 0), t // tq)],
        out_specs=pl.BlockSpec((1, tq, s_pad), lambda bi, qi_: (bi, qi_, 0)),
        out_shape=jax.ShapeDtypeStruct((b, t, s_pad), jnp.int8),
        scratch_shapes=[pltpu.VMEM((tq, s_pad), I32),
                        pltpu.VMEM((IDX_HEADS, tq, LANES), F32)],
        compiler_params=_cparams(2),
        name="select",
    )(qi, sm, kit)


_V_AUG = 2 * HEAD_DIM
_V_ROWS = 512
_MID_TILES = (1024, 512)


def _attn_kernel(q_ref, kt_ref, v_ref, mask_ref, bias_ref, o_ref, m_ref, acc_ref, qg_ref, vaug_ref,
                 *, tq, tk, far_widths, q_start):
    qb = pl.program_id(1)
    jd = (q_start + qb * tq) // tk

    @pl.when(qb == 0)
    def _():
        one_hot = (lax.broadcasted_iota(I32, (_V_ROWS, _V_AUG - HEAD_DIM), 1) == 0).astype(vaug_ref.dtype)

        def body(i, carry):
            r0 = pl.multiple_of(i * _V_ROWS, _V_ROWS)
            blk = v_ref[0, pl.ds(r0, _V_ROWS), :]
            for g in range(N_KV_HEADS):
                vaug_ref[pl.ds(r0, _V_ROWS), g * _V_AUG:g * _V_AUG + HEAD_DIM] = blk[:, g * HEAD_DIM:(g + 1) * HEAD_DIM]
                vaug_ref[pl.ds(r0, _V_ROWS), g * _V_AUG + HEAD_DIM:(g + 1) * _V_AUG] = one_hot
            return carry

        lax.fori_loop(0, vaug_ref.shape[0] // _V_ROWS, body, 0)

    for g in range(N_KV_HEADS):
        for r in range(Q_PER_KV):
            c0 = (g * Q_PER_KV + r) * HEAD_DIM
            qg_ref[g, r * tq:(r + 1) * tq, :] = q_ref[0, :, c0:c0 + HEAD_DIM]
    m_ref[...] = jnp.full(m_ref.shape, NEG_BIG, F32)
    acc_ref[...] = jnp.zeros(acc_ref.shape, F32)

    def process(k0, width, near):
        nt = width // LANES
        mb = jnp.where(mask_ref[0, :, pl.ds(k0, width)].astype(I32) != 0, 0.0, NEG_BIG)
        mb = jnp.concatenate([mb] * Q_PER_KV, axis=0)
        for g in range(N_KV_HEADS):
            kt = kt_ref[0, g * HEAD_DIM:(g + 1) * HEAD_DIM, pl.ds(k0, width)]
            s = jnp.dot(qg_ref[g], kt, preferred_element_type=F32) + mb
            if near is not None:
                s = s + bias_ref[near, g]
            tiles = [s[:, u * LANES:(u + 1) * LANES] for u in range(nt)]
            mx = tiles[0]
            for u in range(1, nt):
                mx = jnp.maximum(mx, tiles[u])
            m_old = m_ref[g]
            m_new = jnp.maximum(m_old, jnp.max(mx, axis=1, keepdims=True))
            alpha = jnp.exp(m_old - m_new)
            p = jnp.concatenate([jnp.exp(tl - m_new).astype(MXU_DTYPE) for tl in tiles], axis=1)
            vt = vaug_ref[pl.ds(k0, width), g * _V_AUG:(g + 1) * _V_AUG]
            acc_ref[g] = alpha * acc_ref[g] + jnp.dot(p, vt, preferred_element_type=F32)
            m_ref[g] = m_new

    far_keys = jnp.maximum(jd - 1, 0) * tk
    done = 0
    for width in far_widths:
        steps = (far_keys - done) // width

        def far_body(i, carry, width=width, done=done):
            process(pl.multiple_of(done + i * width, width), width, None)
            return carry

        lax.fori_loop(0, steps, far_body, 0)
        done = done + steps * width

    @pl.when(jd >= 1)
    def _():
        process(pl.multiple_of((jd - 1) * tk, tk), tk, 0)

    process(pl.multiple_of(jd * tk, tk), tk, 1)

    for g in range(N_KV_HEADS):
        acc = acc_ref[g]
        o = acc[:, :HEAD_DIM] / acc[:, HEAD_DIM:HEAD_DIM + 1]
        for r in range(Q_PER_KV):
            c0 = (g * Q_PER_KV + r) * HEAD_DIM
            o_ref[0, :, c0:c0 + HEAD_DIM] = o[r * tq:(r + 1) * tq].astype(o_ref.dtype)


def _attention(q, kt, v, mask, bias, tq, tk, wide, q_start):
    b, t, _ = q.shape
    s_pad = kt.shape[2]
    assert s_pad % _V_ROWS == 0
    rows = Q_PER_KV * tq
    far_widths = tuple(sorted({wide, tk} | {w for w in _MID_TILES if tk < w < wide}, reverse=True))
    assert all(a % b == 0 for a, b in zip(far_widths, far_widths[1:]))
    kern = functools.partial(_attn_kernel, tq=tq, tk=tk, far_widths=far_widths, q_start=q_start)
    return pl.pallas_call(
        kern,
        grid=(b, t // tq),
        in_specs=[pl.BlockSpec((1, tq, N_HEADS * HEAD_DIM), lambda bi, qi_: (bi, qi_, 0)),
                  _per_batch((1, KV_DIM, s_pad), lambda bi, qi_: (bi, 0, 0), t // tq),
                  _per_batch((1, s_pad, KV_DIM), lambda bi, qi_: (bi, 0, 0), t // tq),
                  pl.BlockSpec((1, tq, s_pad), lambda bi, qi_: (bi, qi_, 0)),
                  _resident((2, N_KV_HEADS, rows, tk), lambda bi, qi_: (0, 0, 0, 0))],
        out_specs=pl.BlockSpec((1, tq, N_HEADS * HEAD_DIM), lambda bi, qi_: (bi, qi_, 0)),
        out_shape=jax.ShapeDtypeStruct((b, t, N_HEADS * HEAD_DIM), MXU_DTYPE),
        scratch_shapes=[pltpu.VMEM((N_KV_HEADS, rows, LANES), F32),
                        pltpu.VMEM((N_KV_HEADS, rows, _V_AUG), F32),
                        pltpu.VMEM((N_KV_HEADS, rows, HEAD_DIM), MXU_DTYPE),
                        pltpu.VMEM((s_pad, N_KV_HEADS * _V_AUG), MXU_DTYPE)],
        compiler_params=_cparams(2),
        name="attention",
    )(q, kt, v, mask, bias)


def _layer_norm(x, g, b):
    mu = jnp.mean(x, axis=-1, keepdims=True)
    xc = x - mu
    var = jnp.mean(xc * xc, axis=-1, keepdims=True)
    return xc * lax.rsqrt(var + LN_EPS) * g + b


def _post_kernel(o_ref, yn_ref, gs_ref, ga_ref, x_ref, wso_ref, wao_ref, wout_ref, wup_ref, wdown_ref,
                 ln_ref, out_ref):
    y1 = jnp.dot(yn_ref[...], wso_ref[...], preferred_element_type=F32)
    y2 = jnp.dot(o_ref[...], wao_ref[...], preferred_element_type=F32)
    mix = _sigmoid(gs_ref[...]) * y1 + _sigmoid(ga_ref[...]) * y2
    mixed = _dot(mix, wout_ref[...])
    h = _layer_norm(ALPHA * x_ref[...] + mixed, ln_ref[0:1, :], ln_ref[1:2, :])
    u = jnp.maximum(_dot(h, wup_ref[...]), 0.0)
    f = _dot(u * u, wdown_ref[...])
    out_ref[...] = _layer_norm(ALPHA * h + f, ln_ref[2:3, :], ln_ref[3:4, :])


def _post(o, yn, gs, ga, x2d, w_ssd_o, w_attn_o, w_out, w_up, w_down, ln, tm):
    m = x2d.shape[0]
    row = lambda w: pl.BlockSpec((tm, w), lambda i: (i, 0))
    const = lambda i: (0, 0)
    return pl.pallas_call(
        _post_kernel,
        grid=(m // tm,),
        in_specs=[row(N_HEADS * HEAD_DIM), row(SSD_INNER), row(D_MODEL), row(D_MODEL), row(D_MODEL),
                  _resident((SSD_INNER, D_MODEL), const),
                  _resident((N_HEADS * HEAD_DIM, D_MODEL), const),
                  _resident((D_MODEL, D_MODEL), const),
                  _resident((D_MODEL, D_FF), const),
                  _resident((D_FF, D_MODEL), const),
                  _resident((4, D_MODEL), const)],
        out_specs=row(D_MODEL),
        out_shape=jax.ShapeDtypeStruct((m, D_MODEL), F32),
        compiler_params=_cparams(1),
        name="post",
    )(o, yn, gs, ga, x2d, w_ssd_o, w_attn_o, w_out, w_up, w_down, ln)


def _round_up(n, m):
    return (n + m - 1) // m * m


def _layer(x, cache_k, cache_v, cache_kidx, state_ssm, state_conv, bias_tiles, w_prep, conv_w, conv_b,
           dt_bias, a_log, d_skip, ssd_norm_w, w_ssd_o, w_attn_o, w_out, w_up, w_down, ln,
           *, tq, tk, wide, tqs, wk, lc, tm):
    b, t, _ = x.shape
    past = cache_k.shape[1]
    s_len = past + t
    m = b * t
    assert m % tm == 0 and t % tq == 0 and past % tk == 0 and tk % LANES == 0 and wide % tk == 0
    assert t % tqs == 0 and wk % LANES == 0
    assert tq == tk or t == tq, "every query tile must start on a key-tile boundary"
    assert t % lc == 0 or t < lc
    x2d = x.reshape(m, D_MODEL)

    z, xbc, q, k, v, qi, gs, ga, sm = _inproj(x2d, w_prep, tm)

    t_pad = _round_up(t, lc)
    pad_t = lambda a: jnp.pad(a.reshape(b, t, a.shape[-1]), ((0, 0), (0, t_pad - t), (0, 0)))
    sm3 = pad_t(sm)
    dt_t = sm3[:, :, SM_DT:SM_DT + SSD_HEADS].transpose(0, 2, 1)
    yn, ssm_new = _ssd(pad_t(xbc), pad_t(z), sm3, dt_t, state_conv.astype(F32),
                       _state_to_kernel_layout(state_ssm.astype(F32)), conv_w, conv_b, dt_bias, a_log, d_skip,
                       ssd_norm_w, lc, min(t, lc))
    yn = yn[:, :t].reshape(m, SSD_INNER)
    new_ssm = _state_from_kernel_layout(ssm_new)
    xbc3 = xbc.reshape(b, t, CONV_DIM)
    new_conv = jnp.concatenate([state_conv.astype(F32), xbc3], axis=1)[:, -_HALO:]

    k3 = k.reshape(b, t, KV_DIM)
    v3 = v.reshape(b, t, KV_DIM)
    ki3 = sm.reshape(b, t, LANES)[:, :, SM_KI:SM_KI + IDX_DIM]
    s_pad = _round_up(s_len, wk)
    assert s_pad % tk == 0
    pad_s = lambda a: jnp.pad(a, ((0, 0), (0, s_pad - s_len)) + ((0, 0),) * (a.ndim - 2))
    k_all = pad_s(jnp.concatenate([cache_k.reshape(b, past, KV_DIM).astype(MXU_DTYPE), k3.astype(MXU_DTYPE)], axis=1))
    v_all = pad_s(jnp.concatenate([cache_v.reshape(b, past, KV_DIM).astype(MXU_DTYPE), v3.astype(MXU_DTYPE)], axis=1))
    ki_all = pad_s(jnp.concatenate([cache_kidx.astype(MXU_DTYPE), ki3.astype(MXU_DTYPE)], axis=1))
    n_sel = min(TOPK_MAX, s_len // 4)
    mask = _select(qi.reshape(b, t, -1), sm.reshape(b, t, LANES), ki_all.transpose(0, 2, 1),
                   tqs, wk, s_len, past, n_sel)
    bias = bias_tiles[:, :, :tq, :].reshape(2, N_KV_HEADS, Q_PER_KV * tq, tk)
    o = _attention(q.reshape(b, t, -1), k_all.transpose(0, 2, 1), v_all, mask, bias, tq, tk, wide, past)

    out = _post(o.reshape(m, -1), yn, gs, ga, x2d, w_ssd_o, w_attn_o, w_out, w_up, w_down, ln, tm)
    return out.reshape(b, t, D_MODEL), (k3.reshape(b, t, N_KV_HEADS, HEAD_DIM), v3.reshape(b, t, N_KV_HEADS, HEAD_DIM),
                                        ki3, new_ssm, new_conv)


ATTN_TILE = 128
WIDE_TILE = 2048
SELECT_ROWS = 128
SELECT_WIDTH = 1024
SSD_CHUNK = 128
ROW_TILE = 256


def kernel(x_prompt, x_sample, cache_k, cache_v, cache_kidx, state_ssm, state_conv, rel_bias, w_in, conv_w, conv_b,
           dt_bias, a_log, d_skip, ssd_norm_w, w_ssd_o, w_attn_o, w_out, ln1_g, ln1_b, w_up, w_down, ln2_g, ln2_b):
    bp = x_prompt.shape[0]
    dtype = x_prompt.dtype
    bias_tiles = _bias_tiles(rel_bias, ATTN_TILE)
    yp, ys = x_prompt, x_sample
    st_p, st_s = [], []
    tiles = dict(tk=ATTN_TILE, wide=WIDE_TILE, wk=SELECT_WIDTH, lc=SSD_CHUNK, tm=ROW_TILE)
    for l in range(DEPTH):
        w_prep = _prep_w_in(w_in[l])
        ln = jnp.stack([ln1_g[l], ln1_b[l], ln2_g[l], ln2_b[l]]).astype(F32)
        weights = (bias_tiles, w_prep, conv_w[l], conv_b[l], dt_bias[l], a_log[l], d_skip[l], ssd_norm_w[l],
                   w_ssd_o[l].astype(MXU_DTYPE), w_attn_o[l].astype(MXU_DTYPE), w_out[l].astype(MXU_DTYPE),
                   w_up[l].astype(MXU_DTYPE), w_down[l].astype(MXU_DTYPE), ln)
        yp, sp = _layer(yp, jnp.zeros((bp, 0, N_KV_HEADS, HEAD_DIM), dtype), jnp.zeros((bp, 0, N_KV_HEADS, HEAD_DIM), dtype),
                        jnp.zeros((bp, 0, IDX_DIM), dtype), jnp.zeros((bp, SSD_HEADS, SSD_HEAD_DIM, D_STATE), dtype),
                        jnp.zeros((bp, _HALO, CONV_DIM), dtype), *weights,
                        tq=min(ATTN_TILE, yp.shape[1]), tqs=min(SELECT_ROWS, yp.shape[1]), **tiles)
        ys, ss = _layer(ys, cache_k[l], cache_v[l], cache_kidx[l], state_ssm[l], state_conv[l], *weights,
                        tq=min(ATTN_TILE, ys.shape[1]), tqs=min(SELECT_ROWS, ys.shape[1]), **tiles)
        st_p.append(sp)
        st_s.append(ss)
    stack = lambda sts, i: jnp.stack([s[i] for s in sts])
    return (yp, ys) + tuple(stack(st_p, i) for i in range(5)) + tuple(stack(st_s, i) for i in range(5))
```

```python
import functools

import jax
import jax.numpy as jnp
from jax import lax
from jax.experimental import pallas as pl
from jax.experimental.pallas import tpu as pltpu

F32 = jnp.float32
I32 = jnp.int32
MXU_DTYPE = jnp.bfloat16

D_MODEL = 1024
CHUNK = 64
SSD_INNER = 2048
SSD_HEAD_DIM = 64
SSD_HEADS = 32
SSD_GROUPS = 4
HEADS_PER_GROUP = SSD_HEADS // SSD_GROUPS
D_STATE = 128
CONV_W = 4
CONV_DIM = SSD_INNER + 2 * SSD_GROUPS * D_STATE
N_HEADS = 16
N_KV_HEADS = 4
Q_PER_KV = N_HEADS // N_KV_HEADS
HEAD_DIM = 64
KV_DIM = N_KV_HEADS * HEAD_DIM
IDX_HEADS = 8
IDX_DIM = 64
TOPK_MAX = 256
REL_BUCKETS = 32
REL_MAX_DIST = 128
D_FF = 4 * D_MODEL
DEPTH = 1
ALPHA = (2.0 * DEPTH) ** 0.25
LN_EPS = 1e-5
RMS_EPS = 1e-5
IN_SIZES = (SSD_INNER, CONV_DIM, SSD_HEADS, N_HEADS * HEAD_DIM, KV_DIM, KV_DIM,
            IDX_HEADS * IDX_DIM, IDX_DIM, IDX_HEADS, D_MODEL, D_MODEL)

LANES = 128
VMEM_LIMIT_BYTES = 56 * 1024 * 1024

SM_KI = 0
SM_DT = IDX_DIM
SM_WI = IDX_DIM + SSD_HEADS
SM_PAD = LANES - (IDX_DIM + SSD_HEADS + IDX_HEADS)

NEG_BIG = -1e30
INT_MIN = -2 ** 31
KEY_NEG_INF = -0x7F800000


def _cparams(n_axes):
    return pltpu.CompilerParams(dimension_semantics=("arbitrary",) * n_axes,
                                vmem_limit_bytes=VMEM_LIMIT_BYTES)


def _resident(shape, index_map):
    return pl.BlockSpec(shape, index_map, pipeline_mode=pl.Buffered(1))


def _per_batch(shape, index_map, steps_per_batch):
    if steps_per_batch > 1:
        return _resident(shape, index_map)
    return pl.BlockSpec(shape, index_map)


def _dot(a, b):
    return jnp.dot(a.astype(MXU_DTYPE), b.astype(MXU_DTYPE), preferred_element_type=F32)


def _dot_exact(a, b):
    return jnp.dot(a, b, preferred_element_type=F32, precision=lax.Precision.HIGHEST)


def _sigmoid(x):
    return 0.5 * jnp.tanh(0.5 * x) + 0.5


def _softplus(x):
    return jnp.maximum(x, 0.0) + jnp.log1p(jnp.exp(-jnp.abs(x)))


_PROJ_WIDTHS = (SSD_INNER, CONV_DIM, N_HEADS * HEAD_DIM, KV_DIM, KV_DIM, IDX_HEADS * IDX_DIM,
                D_MODEL, D_MODEL, LANES)
_PROJ_WIDTH = sum(_PROJ_WIDTHS)
_PROJ_CHUNK = 512


def _prep_w_in(w_in):
    offs = [0]
    for s in IN_SIZES:
        offs.append(offs[-1] + s)
    z, xbc, dt, q, k, v, qi, ki, wi, gs, ga = [w_in[:, offs[i]:offs[i + 1]] for i in range(len(IN_SIZES))]
    small = jnp.concatenate([ki, dt, wi, jnp.zeros((D_MODEL, SM_PAD), w_in.dtype)], axis=1)
    w = jnp.concatenate([z, xbc, q * HEAD_DIM ** -0.5, k, v, qi * IDX_DIM ** -0.5, gs, ga, small], axis=1)
    return w.astype(MXU_DTYPE)


def _inproj_kernel(x_ref, w_ref, z_ref, xbc_ref, q_ref, k_ref, v_ref, qi_ref, gs_ref, ga_ref, sm_ref,
                   kb_ref, vb_ref):
    xb = x_ref[...].astype(MXU_DTYPE)
    outs = (z_ref, xbc_ref, q_ref, k_ref, v_ref, qi_ref, gs_ref, ga_ref, sm_ref)
    c0 = 0
    for ref, width in zip(outs, _PROJ_WIDTHS):
        for o in range(0, width, _PROJ_CHUNK):
            n = min(_PROJ_CHUNK, width - o)
            r = jnp.dot(xb, w_ref[:, c0 + o:c0 + o + n], preferred_element_type=F32)
            ref[:, o:o + n] = r.astype(ref.dtype)
        c0 += width
    kb_ref[...] = k_ref[...].astype(kb_ref.dtype)
    vb_ref[...] = v_ref[...].astype(vb_ref.dtype)


def _inproj(x2d, w_prep, tm):
    m = x2d.shape[0]
    widths = _PROJ_WIDTHS + (KV_DIM, KV_DIM)
    dtypes = (F32, F32, MXU_DTYPE, F32, F32, MXU_DTYPE, F32, F32, F32, MXU_DTYPE, MXU_DTYPE)
    out_shape = tuple(jax.ShapeDtypeStruct((m, w), d) for w, d in zip(widths, dtypes))
    out_specs = tuple(pl.BlockSpec((tm, w), lambda i: (i, 0)) for w in widths)
    return pl.pallas_call(
        _inproj_kernel,
        grid=(m // tm,),
        in_specs=[pl.BlockSpec((tm, D_MODEL), lambda i: (i, 0)),
                  _resident((D_MODEL, _PROJ_WIDTH), lambda i: (0, 0))],
        out_specs=out_specs,
        out_shape=out_shape,
        compiler_params=_cparams(1),
        name="inproj",
    )(x2d, w_prep)


_HALO = CONV_W - 1
_XP_BASE = 8


def _ssd_kernel(xbc_ref, z_ref, sm_ref, dtT_ref, conv0_ref, ssm0_ref, convw_ref, convb_ref,
                hp_ref, hpT_ref, dfull_ref, nw_ref,
                yn_ref, ssm_out_ref, xp_ref, st_ref, y_ref, *, lc, t_valid):
    c = pl.program_id(1)
    nc = pl.num_programs(1)

    @pl.when(c == 0)
    def _():
        xp_ref[_XP_BASE - _HALO:_XP_BASE, :] = conv0_ref[0]
        st_ref[...] = ssm0_ref[0]

    xp_ref[_XP_BASE:_XP_BASE + lc, :] = xbc_ref[0]
    staged = xp_ref[...]
    conv = convb_ref[...]
    for i in range(CONV_W):
        back = _HALO - i
        shifted = pltpu.roll(staged, back, axis=0) if back else staged
        conv = conv + shifted[_XP_BASE:, :] * convw_ref[i:i + 1, :]
    xp_ref[_XP_BASE - _HALO:_XP_BASE, :] = xp_ref[_XP_BASE + lc - _HALO:_XP_BASE + lc, :]
    act = conv * _sigmoid(conv)
    xs = act[:, :SSD_INNER]
    bm = act[:, SSD_INNER:SSD_INNER + SSD_GROUPS * D_STATE]
    cm = act[:, SSD_INNER + SSD_GROUPS * D_STATE:]

    dt = _softplus(sm_ref[0][:, SM_DT:SM_DT + SSD_HEADS] + hp_ref[0:1, :])
    dt_t = _softplus(dtT_ref[0] + hpT_ref[:, 0:1])
    if t_valid < lc:
        dt = jnp.where(lax.broadcasted_iota(I32, (lc, SSD_HEADS), 0) < t_valid, dt, 0.0)
        dt_t = jnp.where(lax.broadcasted_iota(I32, (SSD_HEADS, lc), 1) < t_valid, dt_t, 0.0)
    a = dt * (-jnp.exp(hp_ref[1:2, :]))
    a_t = dt_t * (-jnp.exp(hpT_ref[:, 1:2]))
    ri = lax.broadcasted_iota(I32, (lc, lc), 0)
    ci = lax.broadcasted_iota(I32, (lc, lc), 1)
    causal = ci <= ri
    acum = _dot_exact(causal.astype(F32), a)
    acum_t = _dot_exact(a_t, (ri <= ci).astype(F32))
    e_a = jnp.exp(acum)
    w_end = jnp.exp(acum[lc - 1:lc, :] - acum) * dt
    lane = lax.broadcasted_iota(I32, (lc, 2 * SSD_HEAD_DIM), 1)
    first_head = lane < SSD_HEAD_DIM
    gw = HEADS_PER_GROUP * SSD_HEAD_DIM

    def per_head_lanes(v, h0):
        return jnp.where(first_head, v[:, h0:h0 + 1], v[:, h0 + 1:h0 + 2])

    for g in range(SSD_GROUPS):
        b_g = bm[:, g * D_STATE:(g + 1) * D_STATE]
        c_g = cm[:, g * D_STATE:(g + 1) * D_STATE].astype(MXU_DTYPE)
        b_gt = b_g.T.astype(MXU_DTYPE)
        cb = jnp.dot(c_g, b_gt, preferred_element_type=F32)
        state = st_ref[g]
        y_off = _dot(c_g, state)
        xw, e_last = [], []
        for jp in range(HEADS_PER_GROUP // 2):
            h0 = g * HEADS_PER_GROUP + 2 * jp
            ms = []
            for h in (h0, h0 + 1):
                seg = acum[:, h:h + 1] - acum_t[h:h + 1, :]
                decay = jnp.exp(jnp.where(causal, seg, -jnp.inf))
                ms.append((cb * decay * dt_t[h:h + 1, :]).astype(MXU_DTYPE))
            c0 = h0 * SSD_HEAD_DIM
            x_pair = xs[:, c0:c0 + 2 * SSD_HEAD_DIM]
            r = jnp.dot(jnp.concatenate(ms, axis=0), x_pair.astype(MXU_DTYPE), preferred_element_type=F32)
            y_diag = jnp.where(first_head, r[:lc], r[lc:])
            o0 = 2 * jp * SSD_HEAD_DIM
            e_pair = per_head_lanes(e_a, h0)
            y_ref[:, c0:c0 + 2 * SSD_HEAD_DIM] = (y_diag + y_off[:, o0:o0 + 2 * SSD_HEAD_DIM] * e_pair
                                                  + dfull_ref[:, c0:c0 + 2 * SSD_HEAD_DIM] * x_pair)
            xw.append((x_pair * per_head_lanes(w_end, h0)).astype(MXU_DTYPE))
            e_last.append(e_pair[lc - 1:lc, :])
        st_ref[g] = (state * jnp.concatenate(e_last, axis=1)
                     + jnp.dot(b_gt, jnp.concatenate(xw, axis=1), preferred_element_type=F32))

    zv = z_ref[0]
    hg = y_ref[...] * (zv * _sigmoid(zv))
    for g in range(SSD_GROUPS):
        sl = hg[:, g * gw:(g + 1) * gw]
        ms = jnp.mean(sl * sl, axis=-1, keepdims=True)
        yn_ref[0, :, g * gw:(g + 1) * gw] = (sl * lax.rsqrt(ms + RMS_EPS) * nw_ref[:, g * gw:(g + 1) * gw]).astype(yn_ref.dtype)

    @pl.when(c == nc - 1)
    def _():
        ssm_out_ref[0] = st_ref[...]


def _ssd(xbc, z, sm, dt_t, conv0, ssm0_t, conv_w, conv_b, dt_bias, a_log, d_skip, norm_w, lc, t_valid):
    b, t, _ = xbc.shape
    nc = t // lc
    hp = jnp.stack([dt_bias, a_log]).astype(F32)
    d_full = jnp.repeat(d_skip.astype(F32), SSD_HEAD_DIM)[None, :]
    gw = HEADS_PER_GROUP * SSD_HEAD_DIM
    const2 = lambda bi, ci: (0, 0)
    kern = functools.partial(_ssd_kernel, lc=lc, t_valid=t_valid)
    return pl.pallas_call(
        kern,
        grid=(b, nc),
        in_specs=[pl.BlockSpec((1, lc, CONV_DIM), lambda bi, ci: (bi, ci, 0)),
                  pl.BlockSpec((1, lc, SSD_INNER), lambda bi, ci: (bi, ci, 0)),
                  pl.BlockSpec((1, lc, LANES), lambda bi, ci: (bi, ci, 0)),
                  pl.BlockSpec((1, SSD_HEADS, lc), lambda bi, ci: (bi, 0, ci)),
                  pl.BlockSpec((1, _HALO, CONV_DIM), lambda bi, ci: (bi, 0, 0)),
                  pl.BlockSpec((1, SSD_GROUPS, D_STATE, gw), lambda bi, ci: (bi, 0, 0, 0)),
                  pl.BlockSpec((CONV_W, CONV_DIM), const2),
                  pl.BlockSpec((1, CONV_DIM), const2),
                  pl.BlockSpec((2, SSD_HEADS), const2),
                  pl.BlockSpec((SSD_HEADS, 2), const2),
                  pl.BlockSpec((1, SSD_INNER), const2),
                  pl.BlockSpec((1, SSD_INNER), const2)],
        out_specs=(pl.BlockSpec((1, lc, SSD_INNER), lambda bi, ci: (bi, ci, 0)),
                   pl.BlockSpec((1, SSD_GROUPS, D_STATE, gw), lambda bi, ci: (bi, 0, 0, 0))),
        out_shape=(jax.ShapeDtypeStruct((b, t, SSD_INNER), MXU_DTYPE),
                   jax.ShapeDtypeStruct((b, SSD_GROUPS, D_STATE, gw), F32)),
        scratch_shapes=[pltpu.VMEM((_XP_BASE + lc, CONV_DIM), F32),
                        pltpu.VMEM((SSD_GROUPS, D_STATE, gw), F32),
                        pltpu.VMEM((lc, SSD_INNER), F32)],
        compiler_params=_cparams(2),
        name="ssd",
    )(xbc, z, sm, dt_t, conv0, ssm0_t, conv_w.astype(F32), conv_b.astype(F32)[None, :], hp, hp.T,
      d_full, norm_w.astype(F32)[None, :])


def _state_to_kernel_layout(s):
    b = s.shape[0]
    s = s.reshape(b, SSD_GROUPS, HEADS_PER_GROUP, SSD_HEAD_DIM, D_STATE)
    return s.transpose(0, 1, 4, 2, 3).reshape(b, SSD_GROUPS, D_STATE, HEADS_PER_GROUP * SSD_HEAD_DIM)


def _state_from_kernel_layout(s):
    b = s.shape[0]
    s = s.reshape(b, SSD_GROUPS, D_STATE, HEADS_PER_GROUP, SSD_HEAD_DIM)
    return s.transpose(0, 1, 3, 4, 2).reshape(b, SSD_HEADS, SSD_HEAD_DIM, D_STATE)


_T5_LOG_STARTS = (12, 16, 23, 32, 46, 64, 91)
_T5_FAR_BUCKET = REL_BUCKETS // 2 - 1


def _bias_kernel(rb_ref, out_ref, *, tk):
    half = REL_BUCKETS // 2
    max_exact = half // 2
    ri = lax.broadcasted_iota(I32, (tk, tk), 0)
    ci = lax.broadcasted_iota(I32, (tk, tk), 1)
    for t in range(2):
        rel = ci - ri - (1 - t) * tk
        n = jnp.abs(rel)
        large = jnp.full((tk, tk), max_exact, I32)
        for start in _T5_LOG_STARTS:
            large = large + (n >= start).astype(I32)
        bucket = jnp.where(rel > 0, half, 0) + jnp.where(n < max_exact, n, large)
        for h in range(N_HEADS):
            acc = jnp.zeros((tk, tk), F32)
            for bkt in range(REL_BUCKETS):
                acc = jnp.where(bucket == bkt, rb_ref[bkt, h], acc)
            out_ref[t, h] = acc - rb_ref[_T5_FAR_BUCKET, h]


def _bias_tiles(rel_bias, tk):
    return pl.pallas_call(
        functools.partial(_bias_kernel, tk=tk),
        in_specs=[pl.BlockSpec(memory_space=pltpu.SMEM)],
        out_specs=pl.BlockSpec(memory_space=pltpu.VMEM),
        out_shape=jax.ShapeDtypeStruct((2, N_HEADS, tk, tk), F32),
        name="bias_tiles",
    )(rel_bias.astype(F32))


_MAX_BISECTIONS = 40
_SCORE_COLS = 256


def _select_kernel(qi_ref, sm_ref, kit_ref, mask_ref, key_ref, w_ref, *, tq, wk, s_len, q_start, n_sel):
    qb = pl.program_id(1)
    n_blocks = mask_ref.shape[2] // wk
    nt = wk // LANES
    p0 = q_start + qb * tq
    pos = p0 + lax.broadcasted_iota(I32, (tq, 1), 0)
    vis_end = (pos // CHUNK + 1) * CHUNK
    vis_max = jnp.minimum(((p0 + tq - 1) // CHUNK + 1) * CHUNK, s_len)
    nkb = (vis_max + wk - 1) // wk

    wi = sm_ref[0][:, SM_WI:SM_WI + IDX_HEADS] * IDX_HEADS ** -0.5
    for hh in range(IDX_HEADS):
        w_ref[hh] = jnp.broadcast_to(wi[:, hh:hh + 1], (tq, LANES))
    lane_k = lax.broadcasted_iota(I32, (1, LANES), 1)

    def to_key(score):
        bits = lax.bitcast_convert_type(score, I32)
        return jnp.where(bits < 0, INT_MIN - bits, bits)

    def score_block(j, top, masked):
        m1, m2 = top
        k0 = pl.multiple_of(j * wk, wk)
        for c in range(wk // _SCORE_COLS):
            c0 = k0 + c * _SCORE_COLS
            kt = kit_ref[0, :, pl.ds(c0, _SCORE_COLS)]
            d = jnp.maximum(jnp.dot(qi_ref[0, 0], kt, preferred_element_type=F32), 0.0)
            for u in range(_SCORE_COLS // LANES):
                sc = jnp.zeros((tq, LANES), F32)
                for hh in range(IDX_HEADS):
                    sc = sc + w_ref[hh] * d[hh * tq:(hh + 1) * tq, u * LANES:(u + 1) * LANES]
                if masked:
                    key_pos = c0 + u * LANES + lane_k
                    sc = jnp.where((key_pos < vis_end) & (key_pos < s_len), sc, -jnp.inf)
                key_ref[:, pl.ds(c0 + u * LANES, LANES)] = to_key(sc)
                m2 = jnp.maximum(m2, jnp.minimum(m1, sc))
                m1 = jnp.maximum(m1, sc)
        return m1, m2

    n_open = jnp.minimum(p0, s_len) // wk
    top = (jnp.full((tq, LANES), -jnp.inf, F32),) * 2
    top = lax.fori_loop(0, n_open, functools.partial(score_block, masked=False), top)
    m1, m2 = lax.fori_loop(n_open, nkb, functools.partial(score_block, masked=True), top)

    def count_ge(thr):
        thr_b = jnp.broadcast_to(thr, (tq, LANES))

        def body(j, acc):
            k0 = pl.multiple_of(j * wk, wk)
            for u in range(nt):
                acc = acc + jnp.where(key_ref[:, pl.ds(k0 + u * LANES, LANES)] >= thr_b, 1.0, 0.0)
            return acc

        acc = lax.fori_loop(0, nkb, body, jnp.zeros((tq, LANES), F32))
        return jnp.sum(acc, axis=1, keepdims=True).astype(I32)

    def unfinished(lo, hi, cnt_lo):
        return (cnt_lo > n_sel) & (hi - 1 > lo)

    def bisect(lo, hi, cnt_lo):
        def cond(st):
            return (st[0] < _MAX_BISECTIONS) & (st[1] > 0)

        def body(st):
            it, _, lo, hi, cnt_lo = st
            live = unfinished(lo, hi, cnt_lo)
            mid = (lo >> 1) + (hi >> 1) + (lo & hi & 1)
            cnt = count_ge(jnp.where(live, mid, lo))
            up = live & (cnt >= n_sel)
            lo = jnp.where(up, mid, lo)
            cnt_lo = jnp.where(up, cnt, cnt_lo)
            hi = jnp.where(live & (cnt < n_sel), mid, hi)
            return (it + 1, jnp.max(unfinished(lo, hi, cnt_lo).astype(I32)), lo, hi, cnt_lo)

        st0 = (jnp.int32(0), jnp.max(unfinished(lo, hi, cnt_lo).astype(I32)), lo, hi, cnt_lo)
        return lax.while_loop(cond, body, st0)[2:]

    lo = jnp.maximum(to_key(jnp.min(m2, axis=1, keepdims=True)), KEY_NEG_INF + 1)
    hi = to_key(jnp.max(m1, axis=1, keepdims=True)) + 1
    tau, _, cnt_tau = bisect(lo, hi, count_ge(lo))

    tie_rows = cnt_tau > n_sel
    has_ties = jnp.max(tie_rows.astype(I32)) > 0

    @pl.when(jnp.logical_not(has_ties))
    def _():
        tau_b = jnp.broadcast_to(tau, (tq, LANES))

        def body(j, carry):
            k0 = pl.multiple_of(j * wk, wk)
            for u in range(nt):
                sl = pl.ds(k0 + u * LANES, LANES)
                mask_ref[0, :, sl] = (key_ref[:, sl] >= tau_b).astype(I32).astype(mask_ref.dtype)
            return carry
        lax.fori_loop(0, nkb, body, 0)

    @pl.when(has_ties)
    def _():
        upper = (lax.broadcasted_iota(I32, (wk, wk), 0) <= lax.broadcasted_iota(I32, (wk, wk), 1)).astype(MXU_DTYPE)
        need = jnp.where(tie_rows, n_sel - count_ge(tau + 1), wk * n_blocks).astype(F32)

        def body(j, seen):
            k0 = pl.multiple_of(j * wk, wk)
            blk = key_ref[:, pl.ds(k0, wk)]
            eq = blk == tau
            rank = seen + jnp.dot(eq.astype(F32).astype(MXU_DTYPE), upper, preferred_element_type=F32)
            sel = (blk > tau) | (eq & (rank <= need))
            mask_ref[0, :, pl.ds(k0, wk)] = sel.astype(I32).astype(mask_ref.dtype)
            return seen + jnp.sum(eq.astype(F32), axis=1, keepdims=True)
        lax.fori_loop(0, nkb, body, jnp.zeros((tq, 1), F32))

    def zero_body(j, carry):
        k0 = pl.multiple_of(j * wk, wk)
        mask_ref[0, :, pl.ds(k0, wk)] = jnp.zeros((tq, wk), mask_ref.dtype)
        return carry

    lax.fori_loop(nkb, n_blocks, zero_body, 0)


def _select(qi, sm, kit, tq, wk, s_len, q_start, n_sel):
    b, t, _ = qi.shape
    s_pad = kit.shape[2]
    assert n_sel <= 2 * LANES, "the starting bracket takes two candidates per lane column"
    kern = functools.partial(_select_kernel, tq=tq, wk=wk, s_len=s_len, q_start=q_start, n_sel=n_sel)
    qi = qi.reshape(b, t // tq, tq, IDX_HEADS, IDX_DIM).transpose(0, 1, 3, 2, 4).reshape(b, t // tq, IDX_HEADS * tq, IDX_DIM)
    return pl.pallas_call(
        kern,
        grid=(b, t // tq),
        in_specs=[pl.BlockSpec((1, 1, IDX_HEADS * tq, IDX_DIM), lambda bi, qi_: (bi, qi_, 0, 0)),
                  pl.BlockSpec((1, tq, LANES), lambda bi, qi_: (bi, qi_, 0)),
                  _per_batch((1, IDX_DIM, s_pad), lambda bi, qi_: (bi, 0, 0), t // tq)],
        out_specs=pl.BlockSpec((1, tq, s_pad), lambda bi, qi_: (bi, qi_, 0)),
        out_shape=jax.ShapeDtypeStruct((b, t, s_pad), jnp.int8),
        scratch_shapes=[pltpu.VMEM((tq, s_pad), I32),
                        pltpu.VMEM((IDX_HEADS, tq, LANES), F32)],
        compiler_params=_cparams(2),
        name="select",
    )(qi, sm, kit)


_V_AUG = 2 * HEAD_DIM
_V_ROWS = 512
_MID_TILES = (1024, 512)


def _attn_kernel(q_ref, kt_ref, v_ref, mask_ref, bias_ref, o_ref, m_ref, acc_ref, qg_ref, vaug_ref,
                 *, tq, tk, far_widths, q_start):
    qb = pl.program_id(1)
    jd = (q_start + qb * tq) // tk

    @pl.when(qb == 0)
    def _():
        one_hot = (lax.broadcasted_iota(I32, (_V_ROWS, _V_AUG - HEAD_DIM), 1) == 0).astype(vaug_ref.dtype)

        def body(i, carry):
            r0 = pl.multiple_of(i * _V_ROWS, _V_ROWS)
            blk = v_ref[0, pl.ds(r0, _V_ROWS), :]
            for g in range(N_KV_HEADS):
                vaug_ref[pl.ds(r0, _V_ROWS), g * _V_AUG:g * _V_AUG + HEAD_DIM] = blk[:, g * HEAD_DIM:(g + 1) * HEAD_DIM]
                vaug_ref[pl.ds(r0, _V_ROWS), g * _V_AUG + HEAD_DIM:(g + 1) * _V_AUG] = one_hot
            return carry

        lax.fori_loop(0, vaug_ref.shape[0] // _V_ROWS, body, 0)

    for g in range(N_KV_HEADS):
        for r in range(Q_PER_KV):
            c0 = (g * Q_PER_KV + r) * HEAD_DIM
            qg_ref[g, r * tq:(r + 1) * tq, :] = q_ref[0, :, c0:c0 + HEAD_DIM]
    m_ref[...] = jnp.full(m_ref.shape, NEG_BIG, F32)
    acc_ref[...] = jnp.zeros(acc_ref.shape, F32)

    def process(k0, width, near):
        nt = width // LANES
        mb = jnp.where(mask_ref[0, :, pl.ds(k0, width)].astype(I32) != 0, 0.0, NEG_BIG)
        mb = jnp.concatenate([mb] * Q_PER_KV, axis=0)
        for g in range(N_KV_HEADS):
            kt = kt_ref[0, g * HEAD_DIM:(g + 1) * HEAD_DIM, pl.ds(k0, width)]
            s = jnp.dot(qg_ref[g], kt, preferred_element_type=F32) + mb
            if near is not None:
                s = s + bias_ref[near, g]
            tiles = [s[:, u * LANES:(u + 1) * LANES] for u in range(nt)]
            mx = tiles[0]
            for u in range(1, nt):
                mx = jnp.maximum(mx, tiles[u])
            m_old = m_ref[g]
            m_new = jnp.maximum(m_old, jnp.max(mx, axis=1, keepdims=True))
            alpha = jnp.exp(m_old - m_new)
            p = jnp.concatenate([jnp.exp(tl - m_new).astype(MXU_DTYPE) for tl in tiles], axis=1)
            vt = vaug_ref[pl.ds(k0, width), g * _V_AUG:(g + 1) * _V_AUG]
            acc_ref[g] = alpha * acc_ref[g] + jnp.dot(p, vt, preferred_element_type=F32)
            m_ref[g] = m_new

    far_keys = jnp.maximum(jd - 1, 0) * tk
    done = 0
    for width in far_widths:
        steps = (far_keys - done) // width

        def far_body(i, carry, width=width, done=done):
            process(pl.multiple_of(done + i * width, width), width, None)
            return carry

        lax.fori_loop(0, steps, far_body, 0)
        done = done + steps * width

    @pl.when(jd >= 1)
    def _():
        process(pl.multiple_of((jd - 1) * tk, tk), tk, 0)

    process(pl.multiple_of(jd * tk, tk), tk, 1)

    for g in range(N_KV_HEADS):
        acc = acc_ref[g]
        o = acc[:, :HEAD_DIM] / acc[:, HEAD_DIM:HEAD_DIM + 1]
        for r in range(Q_PER_KV):
            c0 = (g * Q_PER_KV + r) * HEAD_DIM
            o_ref[0, :, c0:c0 + HEAD_DIM] = o[r * tq:(r + 1) * tq].astype(o_ref.dtype)


def _attention(q, kt, v, mask, bias, tq, tk, wide, q_start):
    b, t, _ = q.shape
    s_pad = kt.shape[2]
    assert s_pad % _V_ROWS == 0
    rows = Q_PER_KV * tq
    far_widths = tuple(sorted({wide, tk} | {w for w in _MID_TILES if tk < w < wide}, reverse=True))
    assert all(a % b == 0 for a, b in zip(far_widths, far_widths[1:]))
    kern = functools.partial(_attn_kernel, tq=tq, tk=tk, far_widths=far_widths, q_start=q_start)
    return pl.pallas_call(
        kern,
        grid=(b, t // tq),
        in_specs=[pl.BlockSpec((1, tq, N_HEADS * HEAD_DIM), lambda bi, qi_: (bi, qi_, 0)),
                  _per_batch((1, KV_DIM, s_pad), lambda bi, qi_: (bi, 0, 0), t // tq),
                  _per_batch((1, s_pad, KV_DIM), lambda bi, qi_: (bi, 0, 0), t // tq),
                  pl.BlockSpec((1, tq, s_pad), lambda bi, qi_: (bi, qi_, 0)),
                  _resident((2, N_KV_HEADS, rows, tk), lambda bi, qi_: (0, 0, 0, 0))],
        out_specs=pl.BlockSpec((1, tq, N_HEADS * HEAD_DIM), lambda bi, qi_: (bi, qi_, 0)),
        out_shape=jax.ShapeDtypeStruct((b, t, N_HEADS * HEAD_DIM), MXU_DTYPE),
        scratch_shapes=[pltpu.VMEM((N_KV_HEADS, rows, LANES), F32),
                        pltpu.VMEM((N_KV_HEADS, rows, _V_AUG), F32),
                        pltpu.VMEM((N_KV_HEADS, rows, HEAD_DIM), MXU_DTYPE),
                        pltpu.VMEM((s_pad, N_KV_HEADS * _V_AUG), MXU_DTYPE)],
        compiler_params=_cparams(2),
        name="attention",
    )(q, kt, v, mask, bias)


def _layer_norm(x, g, b):
    mu = jnp.mean(x, axis=-1, keepdims=True)
    xc = x - mu
    var = jnp.mean(xc * xc, axis=-1, keepdims=True)
    return xc * lax.rsqrt(var + LN_EPS) * g + b


def _post_kernel(o_ref, yn_ref, gs_ref, ga_ref, x_ref, wso_ref, wao_ref, wout_ref, wup_ref, wdown_ref,
                 ln_ref, out_ref):
    y1 = jnp.dot(yn_ref[...], wso_ref[...], preferred_element_type=F32)
    y2 = jnp.dot(o_ref[...], wao_ref[...], preferred_element_type=F32)
    mix = _sigmoid(gs_ref[...]) * y1 + _sigmoid(ga_ref[...]) * y2
    mixed = _dot(mix, wout_ref[...])
    h = _layer_norm(ALPHA * x_ref[...] + mixed, ln_ref[0:1, :], ln_ref[1:2, :])
    u = jnp.maximum(_dot(h, wup_ref[...]), 0.0)
    f = _dot(u * u, wdown_ref[...])
    out_ref[...] = _layer_norm(ALPHA * h + f, ln_ref[2:3, :], ln_ref[3:4, :])


def _post(o, yn, gs, ga, x2d, w_ssd_o, w_attn_o, w_out, w_up, w_down, ln, tm):
    m = x2d.shape[0]
    row = lambda w: pl.BlockSpec((tm, w), lambda i: (i, 0))
    const = lambda i: (0, 0)
    return pl.pallas_call(
        _post_kernel,
        grid=(m // tm,),
        in_specs=[row(N_HEADS * HEAD_DIM), row(SSD_INNER), row(D_MODEL), row(D_MODEL), row(D_MODEL),
                  _resident((SSD_INNER, D_MODEL), const),
                  _resident((N_HEADS * HEAD_DIM, D_MODEL), const),
                  _resident((D_MODEL, D_MODEL), const),
                  _resident((D_MODEL, D_FF), const),
                  _resident((D_FF, D_MODEL), const),
                  _resident((4, D_MODEL), const)],
        out_specs=row(D_MODEL),
        out_shape=jax.ShapeDtypeStruct((m, D_MODEL), F32),
        compiler_params=_cparams(1),
        name="post",
    )(o, yn, gs, ga, x2d, w_ssd_o, w_attn_o, w_out, w_up, w_down, ln)


def _round_up(n, m):
    return (n + m - 1) // m * m


def _layer(x, cache_k, cache_v, cache_kidx, state_ssm, state_conv, bias_tiles, w_prep, conv_w, conv_b,
           dt_bias, a_log, d_skip, ssd_norm_w, w_ssd_o, w_attn_o, w_out, w_up, w_down, ln,
           *, tq, tk, wide, tqs, wk, lc, tm):
    b, t, _ = x.shape
    past = cache_k.shape[1]
    s_len = past + t
    m = b * t
    assert m % tm == 0 and t % tq == 0 and past % tk == 0 and tk % LANES == 0 and wide % tk == 0
    assert t % tqs == 0 and wk % LANES == 0
    assert tq == tk or t == tq, "every query tile must start on a key-tile boundary"
    assert t % lc == 0 or t < lc
    x2d = x.reshape(m, D_MODEL)

    z, xbc, q, k, v, qi, gs, ga, sm, kb, vb = _inproj(x2d, w_prep, tm)

    t_pad = _round_up(t, lc)
    pad_t = lambda a: jnp.pad(a.reshape(b, t, a.shape[-1]), ((0, 0), (0, t_pad - t), (0, 0)))
    sm3 = pad_t(sm)
    dt_t = sm3[:, :, SM_DT:SM_DT + SSD_HEADS].transpose(0, 2, 1)
    yn, ssm_new = _ssd(pad_t(xbc), pad_t(z), sm3, dt_t, state_conv.astype(F32),
                       _state_to_kernel_layout(state_ssm.astype(F32)), conv_w, conv_b, dt_bias, a_log, d_skip,
                       ssd_norm_w, lc, min(t, lc))
    yn = yn[:, :t].reshape(m, SSD_INNER)
    new_ssm = _state_from_kernel_layout(ssm_new)
    xbc3 = xbc.reshape(b, t, CONV_DIM)
    new_conv = jnp.concatenate([state_conv.astype(F32), xbc3], axis=1)[:, -_HALO:]

    k3 = k.reshape(b, t, KV_DIM)
    v3 = v.reshape(b, t, KV_DIM)
    ki3 = sm.reshape(b, t, LANES)[:, :, SM_KI:SM_KI + IDX_DIM]
    s_pad = _round_up(s_len, wk)
    assert s_pad % tk == 0
    pad_s = lambda a: jnp.pad(a, ((0, 0), (0, s_pad - s_len)) + ((0, 0),) * (a.ndim - 2))
    k_all = pad_s(jnp.concatenate([cache_k.reshape(b, past, KV_DIM).astype(MXU_DTYPE), kb.reshape(b, t, KV_DIM)], axis=1))
    v_all = pad_s(jnp.concatenate([cache_v.reshape(b, past, KV_DIM).astype(MXU_DTYPE), vb.reshape(b, t, KV_DIM)], axis=1))
    ki_all = pad_s(jnp.concatenate([cache_kidx.astype(MXU_DTYPE), ki3.astype(MXU_DTYPE)], axis=1))
    n_sel = min(TOPK_MAX, s_len // 4)
    mask = _select(qi.reshape(b, t, -1), sm.reshape(b, t, LANES), ki_all.transpose(0, 2, 1),
                   tqs, wk, s_len, past, n_sel)
    bias = bias_tiles[:, :, :tq, :].reshape(2, N_KV_HEADS, Q_PER_KV * tq, tk)
    o = _attention(q.reshape(b, t, -1), k_all.transpose(0, 2, 1), v_all, mask, bias, tq, tk, wide, past)

    out = _post(o.reshape(m, -1), yn, gs, ga, x2d, w_ssd_o, w_attn_o, w_out, w_up, w_down, ln, tm)
    return out.reshape(b, t, D_MODEL), (k3.reshape(b, t, N_KV_HEADS, HEAD_DIM), v3.reshape(b, t, N_KV_HEADS, HEAD_DIM),
                                        ki3, new_ssm, new_conv)


ATTN_TILE = 128
WIDE_TILE = 2048
SELECT_ROWS = 128
SELECT_WIDTH = 1024
SSD_CHUNK = 128
ROW_TILE = 256


def kernel(x_prompt, x_sample, cache_k, cache_v, cache_kidx, state_ssm, state_conv, rel_bias, w_in, conv_w, conv_b,
           dt_bias, a_log, d_skip, ssd_norm_w, w_ssd_o, w_attn_o, w_out, ln1_g, ln1_b, w_up, w_down, ln2_g, ln2_b):
    bp = x_prompt.shape[0]
    dtype = x_prompt.dtype
    bias_tiles = _bias_tiles(rel_bias, ATTN_TILE)
    yp, ys = x_prompt, x_sample
    st_p, st_s = [], []
    tiles = dict(tk=ATTN_TILE, wide=WIDE_TILE, wk=SELECT_WIDTH, lc=SSD_CHUNK, tm=ROW_TILE)
    for l in range(DEPTH):
        w_prep = _prep_w_in(w_in[l])
        ln = jnp.stack([ln1_g[l], ln1_b[l], ln2_g[l], ln2_b[l]]).astype(F32)
        weights = (bias_tiles, w_prep, conv_w[l], conv_b[l], dt_bias[l], a_log[l], d_skip[l], ssd_norm_w[l],
                   w_ssd_o[l].astype(MXU_DTYPE), w_attn_o[l].astype(MXU_DTYPE), w_out[l].astype(MXU_DTYPE),
                   w_up[l].astype(MXU_DTYPE), w_down[l].astype(MXU_DTYPE), ln)
        yp, sp = _layer(yp, jnp.zeros((bp, 0, N_KV_HEADS, HEAD_DIM), dtype), jnp.zeros((bp, 0, N_KV_HEADS, HEAD_DIM), dtype),
                        jnp.zeros((bp, 0, IDX_DIM), dtype), jnp.zeros((bp, SSD_HEADS, SSD_HEAD_DIM, D_STATE), dtype),
                        jnp.zeros((bp, _HALO, CONV_DIM), dtype), *weights,
                        tq=min(ATTN_TILE, yp.shape[1]), tqs=min(SELECT_ROWS, yp.shape[1]), **tiles)
        ys, ss = _layer(ys, cache_k[l], cache_v[l], cache_kidx[l], state_ssm[l], state_conv[l], *weights,
                        tq=min(ATTN_TILE, ys.shape[1]), tqs=min(SELECT_ROWS, ys.shape[1]), **tiles)
        st_p.append(sp)
        st_s.append(ss)
    stack = lambda sts, i: jnp.stack([s[i] for s in sts])
    return (yp, ys) + tuple(stack(st_p, i) for i in range(5)) + tuple(stack(st_s, i) for i in range(5))
```

```python
import functools

import jax
import jax.numpy as jnp
from jax import lax
from jax.experimental import pallas as pl
from jax.experimental.pallas import tpu as pltpu

F32 = jnp.float32
I32 = jnp.int32
MXU_DTYPE = jnp.bfloat16

D_MODEL = 1024
CHUNK = 64
SSD_INNER = 2048
SSD_HEAD_DIM = 64
SSD_HEADS = 32
SSD_GROUPS = 4
HEADS_PER_GROUP = SSD_HEADS // SSD_GROUPS
D_STATE = 128
CONV_W = 4
CONV_DIM = SSD_INNER + 2 * SSD_GROUPS * D_STATE
N_HEADS = 16
N_KV_HEADS = 4
Q_PER_KV = N_HEADS // N_KV_HEADS
HEAD_DIM = 64
KV_DIM = N_KV_HEADS * HEAD_DIM
IDX_HEADS = 8
IDX_DIM = 64
TOPK_MAX = 256
REL_BUCKETS = 32
REL_MAX_DIST = 128
D_FF = 4 * D_MODEL
DEPTH = 1
ALPHA = (2.0 * DEPTH) ** 0.25
LN_EPS = 1e-5
RMS_EPS = 1e-5
IN_SIZES = (SSD_INNER, CONV_DIM, SSD_HEADS, N_HEADS * HEAD_DIM, KV_DIM, KV_DIM,
            IDX_HEADS * IDX_DIM, IDX_DIM, IDX_HEADS, D_MODEL, D_MODEL)

LANES = 128
VMEM_LIMIT_BYTES = 56 * 1024 * 1024

SM_KI = 0
SM_DT = IDX_DIM
SM_WI = IDX_DIM + SSD_HEADS
SM_PAD = LANES - (IDX_DIM + SSD_HEADS + IDX_HEADS)

NEG_BIG = -1e30
INT_MIN = -2 ** 31
KEY_NEG_INF = -0x7F800000


def _cparams(n_axes):
    return pltpu.CompilerParams(dimension_semantics=("arbitrary",) * n_axes,
                                vmem_limit_bytes=VMEM_LIMIT_BYTES)


def _resident(shape, index_map):
    return pl.BlockSpec(shape, index_map, pipeline_mode=pl.Buffered(1))


def _per_batch(shape, index_map, steps_per_batch):
    if steps_per_batch > 1:
        return _resident(shape, index_map)
    return pl.BlockSpec(shape, index_map)


def _dot(a, b):
    return jnp.dot(a.astype(MXU_DTYPE), b.astype(MXU_DTYPE), preferred_element_type=F32)


def _dot_exact(a, b):
    return jnp.dot(a, b, preferred_element_type=F32, precision=lax.Precision.HIGHEST)


def _sigmoid(x):
    return 0.5 * jnp.tanh(0.5 * x) + 0.5


def _softplus(x):
    return jnp.maximum(x, 0.0) + jnp.log1p(jnp.exp(-jnp.abs(x)))


_PROJ_WIDTHS = (SSD_INNER, CONV_DIM, N_HEADS * HEAD_DIM, KV_DIM, KV_DIM, IDX_HEADS * IDX_DIM,
                D_MODEL, D_MODEL, LANES)
_PROJ_WIDTH = sum(_PROJ_WIDTHS)
_PROJ_CHUNK = 512


def _prep_w_in(w_in):
    offs = [0]
    for s in IN_SIZES:
        offs.append(offs[-1] + s)
    z, xbc, dt, q, k, v, qi, ki, wi, gs, ga = [w_in[:, offs[i]:offs[i + 1]] for i in range(len(IN_SIZES))]
    small = jnp.concatenate([ki, dt, wi, jnp.zeros((D_MODEL, SM_PAD), w_in.dtype)], axis=1)
    w = jnp.concatenate([z, xbc, q * HEAD_DIM ** -0.5, k, v, qi * IDX_DIM ** -0.5, gs, ga, small], axis=1)
    return w.astype(MXU_DTYPE)


def _inproj_kernel(x_ref, w_ref, z_ref, xbc_ref, q_ref, k_ref, v_ref, qi_ref, gs_ref, ga_ref, sm_ref,
                   kb_ref, vb_ref):
    xb = x_ref[...].astype(MXU_DTYPE)
    outs = (z_ref, xbc_ref, q_ref, k_ref, v_ref, qi_ref, gs_ref, ga_ref, sm_ref)
    c0 = 0
    for ref, width in zip(outs, _PROJ_WIDTHS):
        for o in range(0, width, _PROJ_CHUNK):
            n = min(_PROJ_CHUNK, width - o)
            r = jnp.dot(xb, w_ref[:, c0 + o:c0 + o + n], preferred_element_type=F32)
            ref[:, o:o + n] = r.astype(ref.dtype)
        c0 += width
    kb_ref[...] = k_ref[...].astype(kb_ref.dtype)
    vb_ref[...] = v_ref[...].astype(vb_ref.dtype)


def _inproj(x2d, w_prep, tm):
    m = x2d.shape[0]
    widths = _PROJ_WIDTHS + (KV_DIM, KV_DIM)
    dtypes = (F32, F32, MXU_DTYPE, F32, F32, MXU_DTYPE, F32, F32, F32, MXU_DTYPE, MXU_DTYPE)
    out_shape = tuple(jax.ShapeDtypeStruct((m, w), d) for w, d in zip(widths, dtypes))
    out_specs = tuple(pl.BlockSpec((tm, w), lambda i: (i, 0)) for w in widths)
    return pl.pallas_call(
        _inproj_kernel,
        grid=(m // tm,),
        in_specs=[pl.BlockSpec((tm, D_MODEL), lambda i: (i, 0)),
                  _resident((D_MODEL, _PROJ_WIDTH), lambda i: (0, 0))],
        out_specs=out_specs,
        out_shape=out_shape,
        compiler_params=_cparams(1),
        name="inproj",
    )(x2d, w_prep)


_HALO = CONV_W - 1
_XP_BASE = 8


def _ssd_kernel(xbc_ref, z_ref, sm_ref, dtT_ref, conv0_ref, ssm0_ref, convw_ref, convb_ref,
                hp_ref, hpT_ref, dfull_ref, nw_ref,
                yn_ref, ssm_out_ref, xp_ref, st_ref, y_ref, *, lc, t_valid):
    c = pl.program_id(1)
    nc = pl.num_programs(1)

    @pl.when(c == 0)
    def _():
        xp_ref[_XP_BASE - _HALO:_XP_BASE, :] = conv0_ref[0]
        st_ref[...] = ssm0_ref[0]

    xp_ref[_XP_BASE:_XP_BASE + lc, :] = xbc_ref[0]
    staged = xp_ref[...]
    conv = convb_ref[...]
    for i in range(CONV_W):
        back = _HALO - i
        shifted = pltpu.roll(staged, back, axis=0) if back else staged
        conv = conv + shifted[_XP_BASE:, :] * convw_ref[i:i + 1, :]
    xp_ref[_XP_BASE - _HALO:_XP_BASE, :] = xp_ref[_XP_BASE + lc - _HALO:_XP_BASE + lc, :]
    act = conv * _sigmoid(conv)
    xs = act[:, :SSD_INNER]
    bm = act[:, SSD_INNER:SSD_INNER + SSD_GROUPS * D_STATE]
    cm = act[:, SSD_INNER + SSD_GROUPS * D_STATE:]

    dt = _softplus(sm_ref[0][:, SM_DT:SM_DT + SSD_HEADS] + hp_ref[0:1, :])
    dt_t = _softplus(dtT_ref[0] + hpT_ref[:, 0:1])
    if t_valid < lc:
        dt = jnp.where(lax.broadcasted_iota(I32, (lc, SSD_HEADS), 0) < t_valid, dt, 0.0)
        dt_t = jnp.where(lax.broadcasted_iota(I32, (SSD_HEADS, lc), 1) < t_valid, dt_t, 0.0)
    a = dt * (-jnp.exp(hp_ref[1:2, :]))
    a_t = dt_t * (-jnp.exp(hpT_ref[:, 1:2]))
    ri = lax.broadcasted_iota(I32, (lc, lc), 0)
    ci = lax.broadcasted_iota(I32, (lc, lc), 1)
    causal = ci <= ri
    acum = _dot_exact(causal.astype(F32), a)
    acum_t = _dot_exact(a_t, (ri <= ci).astype(F32))
    e_a = jnp.exp(acum)
    w_end = jnp.exp(acum[lc - 1:lc, :] - acum) * dt
    lane = lax.broadcasted_iota(I32, (lc, 2 * SSD_HEAD_DIM), 1)
    first_head = lane < SSD_HEAD_DIM
    gw = HEADS_PER_GROUP * SSD_HEAD_DIM

    def per_head_lanes(v, h0):
        return jnp.where(first_head, v[:, h0:h0 + 1], v[:, h0 + 1:h0 + 2])

    for g in range(SSD_GROUPS):
        b_g = bm[:, g * D_STATE:(g + 1) * D_STATE]
        c_g = cm[:, g * D_STATE:(g + 1) * D_STATE].astype(MXU_DTYPE)
        b_gt = b_g.T.astype(MXU_DTYPE)
        cb = jnp.dot(c_g, b_gt, preferred_element_type=F32)
        state = st_ref[g]
        y_off = _dot(c_g, state)
        xw, e_last = [], []
        for jp in range(HEADS_PER_GROUP // 2):
            h0 = g * HEADS_PER_GROUP + 2 * jp
            ms = []
            for h in (h0, h0 + 1):
                seg = acum[:, h:h + 1] - acum_t[h:h + 1, :]
                decay = jnp.exp(jnp.where(causal, seg, -jnp.inf))
                ms.append((cb * decay * dt_t[h:h + 1, :]).astype(MXU_DTYPE))
            c0 = h0 * SSD_HEAD_DIM
            x_pair = xs[:, c0:c0 + 2 * SSD_HEAD_DIM]
            r = jnp.dot(jnp.concatenate(ms, axis=0), x_pair.astype(MXU_DTYPE), preferred_element_type=F32)
            y_diag = jnp.where(first_head, r[:lc], r[lc:])
            o0 = 2 * jp * SSD_HEAD_DIM
            e_pair = per_head_lanes(e_a, h0)
            y_ref[:, c0:c0 + 2 * SSD_HEAD_DIM] = (y_diag + y_off[:, o0:o0 + 2 * SSD_HEAD_DIM] * e_pair
                                                  + dfull_ref[:, c0:c0 + 2 * SSD_HEAD_DIM] * x_pair)
            xw.append((x_pair * per_head_lanes(w_end, h0)).astype(MXU_DTYPE))
            e_last.append(e_pair[lc - 1:lc, :])
        st_ref[g] = (state * jnp.concatenate(e_last, axis=1)
                     + jnp.dot(b_gt, jnp.concatenate(xw, axis=1), preferred_element_type=F32))

    zv = z_ref[0]
    hg = y_ref[...] * (zv * _sigmoid(zv))
    for g in range(SSD_GROUPS):
        sl = hg[:, g * gw:(g + 1) * gw]
        ms = jnp.mean(sl * sl, axis=-1, keepdims=True)
        yn_ref[0, :, g * gw:(g + 1) * gw] = (sl * lax.rsqrt(ms + RMS_EPS) * nw_ref[:, g * gw:(g + 1) * gw]).astype(yn_ref.dtype)

    @pl.when(c == nc - 1)
    def _():
        ssm_out_ref[0] = st_ref[...]


def _ssd(xbc, z, sm, dt_t, conv0, ssm0_t, conv_w, conv_b, dt_bias, a_log, d_skip, norm_w, lc, t_valid):
    b, t, _ = xbc.shape
    nc = t // lc
    hp = jnp.stack([dt_bias, a_log]).astype(F32)
    d_full = jnp.repeat(d_skip.astype(F32), SSD_HEAD_DIM)[None, :]
    gw = HEADS_PER_GROUP * SSD_HEAD_DIM
    const2 = lambda bi, ci: (0, 0)
    kern = functools.partial(_ssd_kernel, lc=lc, t_valid=t_valid)
    return pl.pallas_call(
        kern,
        grid=(b, nc),
        in_specs=[pl.BlockSpec((1, lc, CONV_DIM), lambda bi, ci: (bi, ci, 0)),
                  pl.BlockSpec((1, lc, SSD_INNER), lambda bi, ci: (bi, ci, 0)),
                  pl.BlockSpec((1, lc, LANES), lambda bi, ci: (bi, ci, 0)),
                  pl.BlockSpec((1, SSD_HEADS, lc), lambda bi, ci: (bi, 0, ci)),
                  pl.BlockSpec((1, _HALO, CONV_DIM), lambda bi, ci: (bi, 0, 0)),
                  pl.BlockSpec((1, SSD_GROUPS, D_STATE, gw), lambda bi, ci: (bi, 0, 0, 0)),
                  pl.BlockSpec((CONV_W, CONV_DIM), const2),
                  pl.BlockSpec((1, CONV_DIM), const2),
                  pl.BlockSpec((2, SSD_HEADS), const2),
                  pl.BlockSpec((SSD_HEADS, 2), const2),
                  pl.BlockSpec((1, SSD_INNER), const2),
                  pl.BlockSpec((1, SSD_INNER), const2)],
        out_specs=(pl.BlockSpec((1, lc, SSD_INNER), lambda bi, ci: (bi, ci, 0)),
                   pl.BlockSpec((1, SSD_GROUPS, D_STATE, gw), lambda bi, ci: (bi, 0, 0, 0))),
        out_shape=(jax.ShapeDtypeStruct((b, t, SSD_INNER), MXU_DTYPE),
                   jax.ShapeDtypeStruct((b, SSD_GROUPS, D_STATE, gw), F32)),
        scratch_shapes=[pltpu.VMEM((_XP_BASE + lc, CONV_DIM), F32),
                        pltpu.VMEM((SSD_GROUPS, D_STATE, gw), F32),
                        pltpu.VMEM((lc, SSD_INNER), F32)],
        compiler_params=_cparams(2),
        name="ssd",
    )(xbc, z, sm, dt_t, conv0, ssm0_t, conv_w.astype(F32), conv_b.astype(F32)[None, :], hp, hp.T,
      d_full, norm_w.astype(F32)[None, :])


def _state_to_kernel_layout(s):
    b = s.shape[0]
    s = s.reshape(b, SSD_GROUPS, HEADS_PER_GROUP, SSD_HEAD_DIM, D_STATE)
    return s.transpose(0, 1, 4, 2, 3).reshape(b, SSD_GROUPS, D_STATE, HEADS_PER_GROUP * SSD_HEAD_DIM)


def _state_from_kernel_layout(s):
    b = s.shape[0]
    s = s.reshape(b, SSD_GROUPS, D_STATE, HEADS_PER_GROUP, SSD_HEAD_DIM)
    return s.transpose(0, 1, 3, 4, 2).reshape(b, SSD_HEADS, SSD_HEAD_DIM, D_STATE)


_T5_LOG_STARTS = (12, 16, 23, 32, 46, 64, 91)
_T5_FAR_BUCKET = REL_BUCKETS // 2 - 1


def _bias_kernel(rb_ref, out_ref, *, tk):
    half = REL_BUCKETS // 2
    max_exact = half // 2
    ri = lax.broadcasted_iota(I32, (tk, tk), 0)
    ci = lax.broadcasted_iota(I32, (tk, tk), 1)
    for t in range(2):
        rel = ci - ri - (1 - t) * tk
        n = jnp.abs(rel)
        large = jnp.full((tk, tk), max_exact, I32)
        for start in _T5_LOG_STARTS:
            large = large + (n >= start).astype(I32)
        bucket = jnp.where(rel > 0, half, 0) + jnp.where(n < max_exact, n, large)
        for h in range(N_HEADS):
            acc = jnp.zeros((tk, tk), F32)
            for bkt in range(REL_BUCKETS):
                acc = jnp.where(bucket == bkt, rb_ref[bkt, h], acc)
            out_ref[t, h] = acc - rb_ref[_T5_FAR_BUCKET, h]


def _bias_tiles(rel_bias, tk):
    return pl.pallas_call(
        functools.partial(_bias_kernel, tk=tk),
        in_specs=[pl.BlockSpec(memory_space=pltpu.SMEM)],
        out_specs=pl.BlockSpec(memory_space=pltpu.VMEM),
        out_shape=jax.ShapeDtypeStruct((2, N_HEADS, tk, tk), F32),
        name="bias_tiles",
    )(rel_bias.astype(F32))


_MAX_BISECTIONS = 40
_SCORE_COLS = 256


def _select_kernel(qi_ref, sm_ref, kit_ref, mask_ref, key_ref, w_ref, *, tq, wk, s_len, q_start, n_sel):
    qb = pl.program_id(1)
    n_blocks = mask_ref.shape[2] // wk
    nt = wk // LANES
    p0 = q_start + qb * tq
    pos = p0 + lax.broadcasted_iota(I32, (tq, 1), 0)
    vis_end = (pos // CHUNK + 1) * CHUNK
    vis_max = jnp.minimum(((p0 + tq - 1) // CHUNK + 1) * CHUNK, s_len)
    nkb = (vis_max + wk - 1) // wk

    wi = sm_ref[0][:, SM_WI:SM_WI + IDX_HEADS] * IDX_HEADS ** -0.5
    for hh in range(IDX_HEADS):
        w_ref[hh] = jnp.broadcast_to(wi[:, hh:hh + 1], (tq, LANES))
    lane_k = lax.broadcasted_iota(I32, (1, LANES), 1)

    def to_key(score):
        bits = lax.bitcast_convert_type(score, I32)
        return jnp.where(bits < 0, INT_MIN - bits, bits)

    def from_key(key):
        return lax.bitcast_convert_type(jnp.where(key < 0, INT_MIN - key, key), F32)

    def score_block(j, top, masked):
        m1, m2 = top
        k0 = pl.multiple_of(j * wk, wk)
        for c in range(wk // _SCORE_COLS):
            c0 = k0 + c * _SCORE_COLS
            kt = kit_ref[0, :, pl.ds(c0, _SCORE_COLS)]
            d = jnp.maximum(jnp.dot(qi_ref[0, 0], kt, preferred_element_type=F32), 0.0)
            for u in range(_SCORE_COLS // LANES):
                sc = jnp.zeros((tq, LANES), F32)
                for hh in range(IDX_HEADS):
                    sc = sc + w_ref[hh] * d[hh * tq:(hh + 1) * tq, u * LANES:(u + 1) * LANES]
                if masked:
                    key_pos = c0 + u * LANES + lane_k
                    sc = jnp.where((key_pos < vis_end) & (key_pos < s_len), sc, -jnp.inf)
                sc = jnp.where(sc == 0.0, 0.0, sc)
                key_ref[j * nt + c * (_SCORE_COLS // LANES) + u] = sc
                m2 = jnp.maximum(m2, jnp.minimum(m1, sc))
                m1 = jnp.maximum(m1, sc)
        return m1, m2

    n_open = jnp.minimum(p0, s_len) // wk
    top = (jnp.full((tq, LANES), -jnp.inf, F32),) * 2
    top = lax.fori_loop(0, n_open, functools.partial(score_block, masked=False), top)
    m1, m2 = lax.fori_loop(n_open, nkb, functools.partial(score_block, masked=True), top)

    def count_ge(thr):
        thr_b = jnp.broadcast_to(from_key(thr), (tq, LANES))

        def body(j, below):
            k0 = pl.multiple_of(j * wk, wk)
            for u in range(nt):
                diff = key_ref[j * nt + u] - thr_b
                below = below + lax.shift_right_logical(lax.bitcast_convert_type(diff, I32), 31)
            return below

        below = lax.fori_loop(0, nkb, body, jnp.zeros((tq, LANES), I32))
        return nkb * wk - jnp.sum(below, axis=1, keepdims=True)

    def unfinished(lo, hi, cnt_lo):
        return (cnt_lo > n_sel) & (hi - 1 > lo)

    def bisect(lo, hi, cnt_lo):
        def cond(st):
            return (st[0] < _MAX_BISECTIONS) & (st[1] > 0)

        def body(st):
            it, _, lo, hi, cnt_lo = st
            live = unfinished(lo, hi, cnt_lo)
            mid = (lo >> 1) + (hi >> 1) + (lo & hi & 1)
            cnt = count_ge(jnp.where(live, mid, lo))
            up = live & (cnt >= n_sel)
            lo = jnp.where(up, mid, lo)
            cnt_lo = jnp.where(up, cnt, cnt_lo)
            hi = jnp.where(live & (cnt < n_sel), mid, hi)
            return (it + 1, jnp.max(unfinished(lo, hi, cnt_lo).astype(I32)), lo, hi, cnt_lo)

        st0 = (jnp.int32(0), jnp.max(unfinished(lo, hi, cnt_lo).astype(I32)), lo, hi, cnt_lo)
        return lax.while_loop(cond, body, st0)[2:]

    lo = jnp.maximum(to_key(jnp.min(m2, axis=1, keepdims=True)), KEY_NEG_INF + 1)
    hi = to_key(jnp.max(m1, axis=1, keepdims=True)) + 1
    tau, _, cnt_tau = bisect(lo, hi, count_ge(lo))

    tie_rows = cnt_tau > n_sel
    has_ties = jnp.max(tie_rows.astype(I32)) > 0

    @pl.when(jnp.logical_not(has_ties))
    def _():
        tau_b = jnp.broadcast_to(from_key(tau), (tq, LANES))

        def body(j, carry):
            k0 = pl.multiple_of(j * wk, wk)
            for u in range(nt):
                sl = pl.ds(k0 + u * LANES, LANES)
                mask_ref[0, :, sl] = (key_ref[j * nt + u] >= tau_b).astype(I32).astype(mask_ref.dtype)
            return carry
        lax.fori_loop(0, nkb, body, 0)

    @pl.when(has_ties)
    def _():
        upper = (lax.broadcasted_iota(I32, (wk, wk), 0) <= lax.broadcasted_iota(I32, (wk, wk), 1)).astype(MXU_DTYPE)
        need = jnp.where(tie_rows, n_sel - count_ge(tau + 1), wk * n_blocks).astype(F32)
        tau_f = from_key(tau)

        def body(j, seen):
            k0 = pl.multiple_of(j * wk, wk)
            blk = jnp.concatenate([key_ref[j * nt + u] for u in range(nt)], axis=1)
            eq = blk == tau_f
            rank = seen + jnp.dot(eq.astype(F32).astype(MXU_DTYPE), upper, preferred_element_type=F32)
            sel = (blk > tau_f) | (eq & (rank <= need))
            mask_ref[0, :, pl.ds(k0, wk)] = sel.astype(I32).astype(mask_ref.dtype)
            return seen + jnp.sum(eq.astype(F32), axis=1, keepdims=True)
        lax.fori_loop(0, nkb, body, jnp.zeros((tq, 1), F32))

    def zero_body(j, carry):
        k0 = pl.multiple_of(j * wk, wk)
        mask_ref[0, :, pl.ds(k0, wk)] = jnp.zeros((tq, wk), mask_ref.dtype)
        return carry

    lax.fori_loop(nkb, n_blocks, zero_body, 0)


def _select(qi, sm, kit, tq, wk, s_len, q_start, n_sel):
    b, t, _ = qi.shape
    s_pad = kit.shape[2]
    assert n_sel <= 2 * LANES, "the starting bracket takes two candidates per lane column"
    kern = functools.partial(_select_kernel, tq=tq, wk=wk, s_len=s_len, q_start=q_start, n_sel=n_sel)
    qi = qi.reshape(b, t // tq, tq, IDX_HEADS, IDX_DIM).transpose(0, 1, 3, 2, 4).reshape(b, t // tq, IDX_HEADS * tq, IDX_DIM)
    return pl.pallas_call(
        kern,
        grid=(b, t // tq),
        in_specs=[pl.BlockSpec((1, 1, IDX_HEADS * tq, IDX_DIM), lambda bi, qi_: (bi, qi_, 0, 0)),
                  pl.BlockSpec((1, tq, LANES), lambda bi, qi_: (bi, qi_, 0)),
                  _per_batch((1, IDX_DIM, s_pad), lambda bi, qi_: (bi, 0, 0), t // tq)],
        out_specs=pl.BlockSpec((1, tq, s_pad), lambda bi, qi_: (bi, qi_, 0)),
        out_shape=jax.ShapeDtypeStruct((b, t, s_pad), jnp.int8),
        scratch_shapes=[pltpu.VMEM((s_pad // LANES, tq, LANES), F32),
                        pltpu.VMEM((IDX_HEADS, tq, LANES), F32)],
        compiler_params=_cparams(2),
        name="select",
    )(qi, sm, kit)


_V_AUG = 2 * HEAD_DIM
_V_ROWS = 512
_MID_TILES = (1024, 512)


def _attn_kernel(q_ref, kt_ref, v_ref, mask_ref, bias_ref, o_ref, m_ref, acc_ref, qg_ref, vaug_ref,
                 *, tq, tk, far_widths, q_start):
    qb = pl.program_id(1)
    jd = (q_start + qb * tq) // tk

    @pl.when(qb == 0)
    def _():
        one_hot = (lax.broadcasted_iota(I32, (_V_ROWS, _V_AUG - HEAD_DIM), 1) == 0).astype(vaug_ref.dtype)

        def body(i, carry):
            r0 = pl.multiple_of(i * _V_ROWS, _V_ROWS)
            blk = v_ref[0, pl.ds(r0, _V_ROWS), :]
            for g in range(N_KV_HEADS):
                vaug_ref[pl.ds(r0, _V_ROWS), g * _V_AUG:g * _V_AUG + HEAD_DIM] = blk[:, g * HEAD_DIM:(g + 1) * HEAD_DIM]
                vaug_ref[pl.ds(r0, _V_ROWS), g * _V_AUG + HEAD_DIM:(g + 1) * _V_AUG] = one_hot
            return carry

        lax.fori_loop(0, vaug_ref.shape[0] // _V_ROWS, body, 0)

    for g in range(N_KV_HEADS):
        for r in range(Q_PER_KV):
            c0 = (g * Q_PER_KV + r) * HEAD_DIM
            qg_ref[g, r * tq:(r + 1) * tq, :] = q_ref[0, :, c0:c0 + HEAD_DIM]
    m_ref[...] = jnp.full(m_ref.shape, NEG_BIG, F32)
    acc_ref[...] = jnp.zeros(acc_ref.shape, F32)

    def process(k0, width, near):
        nt = width // LANES
        mb = jnp.where(mask_ref[0, :, pl.ds(k0, width)].astype(I32) != 0, 0.0, NEG_BIG)
        mb = jnp.concatenate([mb] * Q_PER_KV, axis=0)
        for g in range(N_KV_HEADS):
            kt = kt_ref[0, g * HEAD_DIM:(g + 1) * HEAD_DIM, pl.ds(k0, width)]
            s = jnp.dot(qg_ref[g], kt, preferred_element_type=F32) + mb
            if near is not None:
                s = s + bias_ref[near, g]
            tiles = [s[:, u * LANES:(u + 1) * LANES] for u in range(nt)]
            mx = tiles[0]
            for u in range(1, nt):
                mx = jnp.maximum(mx, tiles[u])
            m_old = m_ref[g]
            m_new = jnp.maximum(m_old, jnp.max(mx, axis=1, keepdims=True))
            alpha = jnp.exp(m_old - m_new)
            p = jnp.concatenate([jnp.exp(tl - m_new).astype(MXU_DTYPE) for tl in tiles], axis=1)
            vt = vaug_ref[pl.ds(k0, width), g * _V_AUG:(g + 1) * _V_AUG]
            acc_ref[g] = alpha * acc_ref[g] + jnp.dot(p, vt, preferred_element_type=F32)
            m_ref[g] = m_new

    far_keys = jnp.maximum(jd - 1, 0) * tk
    done = 0
    for width in far_widths:
        steps = (far_keys - done) // width

        def far_body(i, carry, width=width, done=done):
            process(pl.multiple_of(done + i * width, width), width, None)
            return carry

        lax.fori_loop(0, steps, far_body, 0)
        done = done + steps * width

    @pl.when(jd >= 1)
    def _():
        process(pl.multiple_of((jd - 1) * tk, tk), tk, 0)

    process(pl.multiple_of(jd * tk, tk), tk, 1)

    for g in range(N_KV_HEADS):
        acc = acc_ref[g]
        o = acc[:, :HEAD_DIM] / acc[:, HEAD_DIM:HEAD_DIM + 1]
        for r in range(Q_PER_KV):
            c0 = (g * Q_PER_KV + r) * HEAD_DIM
            o_ref[0, :, c0:c0 + HEAD_DIM] = o[r * tq:(r + 1) * tq].astype(o_ref.dtype)


def _attention(q, kt, v, mask, bias, tq, tk, wide, q_start):
    b, t, _ = q.shape
    s_pad = kt.shape[2]
    assert s_pad % _V_ROWS == 0
    rows = Q_PER_KV * tq
    far_widths = tuple(sorted({wide, tk} | {w for w in _MID_TILES if tk < w < wide}, reverse=True))
    assert all(a % b == 0 for a, b in zip(far_widths, far_widths[1:]))
    kern = functools.partial(_attn_kernel, tq=tq, tk=tk, far_widths=far_widths, q_start=q_start)
    return pl.pallas_call(
        kern,
        grid=(b, t // tq),
        in_specs=[pl.BlockSpec((1, tq, N_HEADS * HEAD_DIM), lambda bi, qi_: (bi, qi_, 0)),
                  _per_batch((1, KV_DIM, s_pad), lambda bi, qi_: (bi, 0, 0), t // tq),
                  _per_batch((1, s_pad, KV_DIM), lambda bi, qi_: (bi, 0, 0), t // tq),
                  pl.BlockSpec((1, tq, s_pad), lambda bi, qi_: (bi, qi_, 0)),
                  _resident((2, N_KV_HEADS, rows, tk), lambda bi, qi_: (0, 0, 0, 0))],
        out_specs=pl.BlockSpec((1, tq, N_HEADS * HEAD_DIM), lambda bi, qi_: (bi, qi_, 0)),
        out_shape=jax.ShapeDtypeStruct((b, t, N_HEADS * HEAD_DIM), MXU_DTYPE),
        scratch_shapes=[pltpu.VMEM((N_KV_HEADS, rows, LANES), F32),
                        pltpu.VMEM((N_KV_HEADS, rows, _V_AUG), F32),
                        pltpu.VMEM((N_KV_HEADS, rows, HEAD_DIM), MXU_DTYPE),
                        pltpu.VMEM((s_pad, N_KV_HEADS * _V_AUG), MXU_DTYPE)],
        compiler_params=_cparams(2),
        name="attention",
    )(q, kt, v, mask, bias)


def _layer_norm(x, g, b):
    mu = jnp.mean(x, axis=-1, keepdims=True)
    xc = x - mu
    var = jnp.mean(xc * xc, axis=-1, keepdims=True)
    return xc * lax.rsqrt(var + LN_EPS) * g + b


def _post_kernel(o_ref, yn_ref, gs_ref, ga_ref, x_ref, wso_ref, wao_ref, wout_ref, wup_ref, wdown_ref,
                 ln_ref, out_ref):
    y1 = jnp.dot(yn_ref[...], wso_ref[...], preferred_element_type=F32)
    y2 = jnp.dot(o_ref[...], wao_ref[...], preferred_element_type=F32)
    mix = _sigmoid(gs_ref[...]) * y1 + _sigmoid(ga_ref[...]) * y2
    mixed = _dot(mix, wout_ref[...])
    h = _layer_norm(ALPHA * x_ref[...] + mixed, ln_ref[0:1, :], ln_ref[1:2, :])
    u = jnp.maximum(_dot(h, wup_ref[...]), 0.0)
    f = _dot(u * u, wdown_ref[...])
    out_ref[...] = _layer_norm(ALPHA * h + f, ln_ref[2:3, :], ln_ref[3:4, :])


def _post(o, yn, gs, ga, x2d, w_ssd_o, w_attn_o, w_out, w_up, w_down, ln, tm):
    m = x2d.shape[0]
    row = lambda w: pl.BlockSpec((tm, w), lambda i: (i, 0))
    const = lambda i: (0, 0)
    return pl.pallas_call(
        _post_kernel,
        grid=(m // tm,),
        in_specs=[row(N_HEADS * HEAD_DIM), row(SSD_INNER), row(D_MODEL), row(D_MODEL), row(D_MODEL),
                  _resident((SSD_INNER, D_MODEL), const),
                  _resident((N_HEADS * HEAD_DIM, D_MODEL), const),
                  _resident((D_MODEL, D_MODEL), const),
                  _resident((D_MODEL, D_FF), const),
                  _resident((D_FF, D_MODEL), const),
                  _resident((4, D_MODEL), const)],
        out_specs=row(D_MODEL),
        out_shape=jax.ShapeDtypeStruct((m, D_MODEL), F32),
        compiler_params=_cparams(1),
        name="post",
    )(o, yn, gs, ga, x2d, w_ssd_o, w_attn_o, w_out, w_up, w_down, ln)


def _round_up(n, m):
    return (n + m - 1) // m * m


def _layer(x, cache_k, cache_v, cache_kidx, state_ssm, state_conv, bias_tiles, w_prep, conv_w, conv_b,
           dt_bias, a_log, d_skip, ssd_norm_w, w_ssd_o, w_attn_o, w_out, w_up, w_down, ln,
           *, tq, tk, wide, tqs, wk, lc, tm):
    b, t, _ = x.shape
    past = cache_k.shape[1]
    s_len = past + t
    m = b * t
    assert m % tm == 0 and t % tq == 0 and past % tk == 0 and tk % LANES == 0 and wide % tk == 0
    assert t % tqs == 0 and wk % LANES == 0
    assert tq == tk or t == tq, "every query tile must start on a key-tile boundary"
    assert t % lc == 0 or t < lc
    x2d = x.reshape(m, D_MODEL)

    z, xbc, q, k, v, qi, gs, ga, sm, kb, vb = _inproj(x2d, w_prep, tm)

    t_pad = _round_up(t, lc)
    pad_t = lambda a: jnp.pad(a.reshape(b, t, a.shape[-1]), ((0, 0), (0, t_pad - t), (0, 0)))
    sm3 = pad_t(sm)
    dt_t = sm3[:, :, SM_DT:SM_DT + SSD_HEADS].transpose(0, 2, 1)
    yn, ssm_new = _ssd(pad_t(xbc), pad_t(z), sm3, dt_t, state_conv.astype(F32),
                       _state_to_kernel_layout(state_ssm.astype(F32)), conv_w, conv_b, dt_bias, a_log, d_skip,
                       ssd_norm_w, lc, min(t, lc))
    yn = yn[:, :t].reshape(m, SSD_INNER)
    new_ssm = _state_from_kernel_layout(ssm_new)
    xbc3 = xbc.reshape(b, t, CONV_DIM)
    new_conv = jnp.concatenate([state_conv.astype(F32), xbc3], axis=1)[:, -_HALO:]

    k3 = k.reshape(b, t, KV_DIM)
    v3 = v.reshape(b, t, KV_DIM)
    ki3 = sm.reshape(b, t, LANES)[:, :, SM_KI:SM_KI + IDX_DIM]
    s_pad = _round_up(s_len, wk)
    assert s_pad % tk == 0
    pad_s = lambda a: jnp.pad(a, ((0, 0), (0, s_pad - s_len)) + ((0, 0),) * (a.ndim - 2))
    k_all = pad_s(jnp.concatenate([cache_k.reshape(b, past, KV_DIM).astype(MXU_DTYPE), kb.reshape(b, t, KV_DIM)], axis=1))
    v_all = pad_s(jnp.concatenate([cache_v.reshape(b, past, KV_DIM).astype(MXU_DTYPE), vb.reshape(b, t, KV_DIM)], axis=1))
    ki_all = pad_s(jnp.concatenate([cache_kidx.astype(MXU_DTYPE), ki3.astype(MXU_DTYPE)], axis=1))
    n_sel = min(TOPK_MAX, s_len // 4)
    mask = _select(qi.reshape(b, t, -1), sm.reshape(b, t, LANES), ki_all.transpose(0, 2, 1),
                   tqs, wk, s_len, past, n_sel)
    bias = bias_tiles[:, :, :tq, :].reshape(2, N_KV_HEADS, Q_PER_KV * tq, tk)
    o = _attention(q.reshape(b, t, -1), k_all.transpose(0, 2, 1), v_all, mask, bias, tq, tk, wide, past)

    out = _post(o.reshape(m, -1), yn, gs, ga, x2d, w_ssd_o, w_attn_o, w_out, w_up, w_down, ln, tm)
    return out.reshape(b, t, D_MODEL), (k3.reshape(b, t, N_KV_HEADS, HEAD_DIM), v3.reshape(b, t, N_KV_HEADS, HEAD_DIM),
                                        ki3, new_ssm, new_conv)


ATTN_TILE = 128
WIDE_TILE = 2048
SELECT_ROWS = 128
SELECT_WIDTH = 1024
SSD_CHUNK = 128
ROW_TILE = 256


def kernel(x_prompt, x_sample, cache_k, cache_v, cache_kidx, state_ssm, state_conv, rel_bias, w_in, conv_w, conv_b,
           dt_bias, a_log, d_skip, ssd_norm_w, w_ssd_o, w_attn_o, w_out, ln1_g, ln1_b, w_up, w_down, ln2_g, ln2_b):
    bp = x_prompt.shape[0]
    dtype = x_prompt.dtype
    bias_tiles = _bias_tiles(rel_bias, ATTN_TILE)
    yp, ys = x_prompt, x_sample
    st_p, st_s = [], []
    tiles = dict(tk=ATTN_TILE, wide=WIDE_TILE, wk=SELECT_WIDTH, lc=SSD_CHUNK, tm=ROW_TILE)
    for l in range(DEPTH):
        w_prep = _prep_w_in(w_in[l])
        ln = jnp.stack([ln1_g[l], ln1_b[l], ln2_g[l], ln2_b[l]]).astype(F32)
        weights = (bias_tiles, w_prep, conv_w[l], conv_b[l], dt_bias[l], a_log[l], d_skip[l], ssd_norm_w[l],
                   w_ssd_o[l].astype(MXU_DTYPE), w_attn_o[l].astype(MXU_DTYPE), w_out[l].astype(MXU_DTYPE),
                   w_up[l].astype(MXU_DTYPE), w_down[l].astype(MXU_DTYPE), ln)
        yp, sp = _layer(yp, jnp.zeros((bp, 0, N_KV_HEADS, HEAD_DIM), dtype), jnp.zeros((bp, 0, N_KV_HEADS, HEAD_DIM), dtype),
                        jnp.zeros((bp, 0, IDX_DIM), dtype), jnp.zeros((bp, SSD_HEADS, SSD_HEAD_DIM, D_STATE), dtype),
                        jnp.zeros((bp, _HALO, CONV_DIM), dtype), *weights,
                        tq=min(ATTN_TILE, yp.shape[1]), tqs=min(SELECT_ROWS, yp.shape[1]), **tiles)
        ys, ss = _layer(ys, cache_k[l], cache_v[l], cache_kidx[l], state_ssm[l], state_conv[l], *weights,
                        tq=min(ATTN_TILE, ys.shape[1]), tqs=min(SELECT_ROWS, ys.shape[1]), **tiles)
        st_p.append(sp)
        st_s.append(ss)
    stack = lambda sts, i: jnp.stack([s[i] for s in sts])
    return (yp, ys) + tuple(stack(st_p, i) for i in range(5)) + tuple(stack(st_s, i) for i in range(5))
```

```python
import functools

import jax
import jax.numpy as jnp
from jax import lax
from jax.experimental import pallas as pl
from jax.experimental.pallas import tpu as pltpu

F32 = jnp.float32
I32 = jnp.int32
MXU_DTYPE = jnp.bfloat16

D_MODEL = 1024
CHUNK = 64
SSD_INNER = 2048
SSD_HEAD_DIM = 64
SSD_HEADS = 32
SSD_GROUPS = 4
HEADS_PER_GROUP = SSD_HEADS // SSD_GROUPS
D_STATE = 128
CONV_W = 4
CONV_DIM = SSD_INNER + 2 * SSD_GROUPS * D_STATE
N_HEADS = 16
N_KV_HEADS = 4
Q_PER_KV = N_HEADS // N_KV_HEADS
HEAD_DIM = 64
KV_DIM = N_KV_HEADS * HEAD_DIM
IDX_HEADS = 8
IDX_DIM = 64
TOPK_MAX = 256
REL_BUCKETS = 32
REL_MAX_DIST = 128
D_FF = 4 * D_MODEL
DEPTH = 1
ALPHA = (2.0 * DEPTH) ** 0.25
LN_EPS = 1e-5
RMS_EPS = 1e-5
IN_SIZES = (SSD_INNER, CONV_DIM, SSD_HEADS, N_HEADS * HEAD_DIM, KV_DIM, KV_DIM,
            IDX_HEADS * IDX_DIM, IDX_DIM, IDX_HEADS, D_MODEL, D_MODEL)

LANES = 128
VMEM_LIMIT_BYTES = 56 * 1024 * 1024

SM_KI = 0
SM_DT = IDX_DIM
SM_WI = IDX_DIM + SSD_HEADS
SM_PAD = LANES - (IDX_DIM + SSD_HEADS + IDX_HEADS)

NEG_BIG = -1e30
INT_MIN = -2 ** 31
KEY_NEG_INF = -0x7F800000


def _cparams(n_axes):
    return pltpu.CompilerParams(dimension_semantics=("arbitrary",) * n_axes,
                                vmem_limit_bytes=VMEM_LIMIT_BYTES)


def _resident(shape, index_map):
    return pl.BlockSpec(shape, index_map, pipeline_mode=pl.Buffered(1))


def _per_batch(shape, index_map, steps_per_batch):
    if steps_per_batch > 1:
        return _resident(shape, index_map)
    return pl.BlockSpec(shape, index_map)


def _dot(a, b):
    return jnp.dot(a.astype(MXU_DTYPE), b.astype(MXU_DTYPE), preferred_element_type=F32)


def _dot_exact(a, b):
    return jnp.dot(a, b, preferred_element_type=F32, precision=lax.Precision.HIGHEST)


def _sigmoid(x):
    return 0.5 * jnp.tanh(0.5 * x) + 0.5


def _softplus(x):
    return jnp.maximum(x, 0.0) + jnp.log1p(jnp.exp(-jnp.abs(x)))


_PROJ_WIDTHS = (SSD_INNER, CONV_DIM, N_HEADS * HEAD_DIM, KV_DIM, KV_DIM, IDX_HEADS * IDX_DIM,
                D_MODEL, D_MODEL, LANES)
_PROJ_WIDTH = sum(_PROJ_WIDTHS)
_PROJ_CHUNK = 512


def _prep_w_in(w_in):
    offs = [0]
    for s in IN_SIZES:
        offs.append(offs[-1] + s)
    z, xbc, dt, q, k, v, qi, ki, wi, gs, ga = [w_in[:, offs[i]:offs[i + 1]] for i in range(len(IN_SIZES))]
    small = jnp.concatenate([ki, dt, wi, jnp.zeros((D_MODEL, SM_PAD), w_in.dtype)], axis=1)
    w = jnp.concatenate([z, xbc, q * HEAD_DIM ** -0.5, k, v, qi * IDX_DIM ** -0.5, gs, ga, small], axis=1)
    return w.astype(MXU_DTYPE)


def _inproj_kernel(x_ref, w_ref, z_ref, xbc_ref, q_ref, k_ref, v_ref, qi_ref, gs_ref, ga_ref, sm_ref,
                   kb_ref, vb_ref):
    xb = x_ref[...].astype(MXU_DTYPE)
    outs = (z_ref, xbc_ref, q_ref, k_ref, v_ref, qi_ref, gs_ref, ga_ref, sm_ref)
    c0 = 0
    for ref, width in zip(outs, _PROJ_WIDTHS):
        for o in range(0, width, _PROJ_CHUNK):
            n = min(_PROJ_CHUNK, width - o)
            r = jnp.dot(xb, w_ref[:, c0 + o:c0 + o + n], preferred_element_type=F32)
            ref[:, o:o + n] = r.astype(ref.dtype)
        c0 += width
    kb_ref[...] = k_ref[...].astype(kb_ref.dtype)
    vb_ref[...] = v_ref[...].astype(vb_ref.dtype)


def _inproj(x2d, w_prep, tm):
    m = x2d.shape[0]
    widths = _PROJ_WIDTHS + (KV_DIM, KV_DIM)
    dtypes = (F32, F32, MXU_DTYPE, F32, F32, MXU_DTYPE, F32, F32, F32, MXU_DTYPE, MXU_DTYPE)
    out_shape = tuple(jax.ShapeDtypeStruct((m, w), d) for w, d in zip(widths, dtypes))
    out_specs = tuple(pl.BlockSpec((tm, w), lambda i: (i, 0)) for w in widths)
    return pl.pallas_call(
        _inproj_kernel,
        grid=(m // tm,),
        in_specs=[pl.BlockSpec((tm, D_MODEL), lambda i: (i, 0)),
                  _resident((D_MODEL, _PROJ_WIDTH), lambda i: (0, 0))],
        out_specs=out_specs,
        out_shape=out_shape,
        compiler_params=_cparams(1),
        name="inproj",
    )(x2d, w_prep)


_HALO = CONV_W - 1
_XP_BASE = 8


def _ssd_kernel(xbc_ref, z_ref, sm_ref, dtT_ref, conv0_ref, ssm0_ref, convw_ref, convb_ref,
                hp_ref, hpT_ref, dfull_ref, nw_ref,
                yn_ref, ssm_out_ref, xp_ref, st_ref, y_ref, *, lc, t_valid):
    c = pl.program_id(1)
    nc = pl.num_programs(1)

    @pl.when(c == 0)
    def _():
        xp_ref[_XP_BASE - _HALO:_XP_BASE, :] = conv0_ref[0]
        st_ref[...] = ssm0_ref[0]

    xp_ref[_XP_BASE:_XP_BASE + lc, :] = xbc_ref[0]
    staged = xp_ref[...]
    conv = convb_ref[...]
    for i in range(CONV_W):
        back = _HALO - i
        shifted = pltpu.roll(staged, back, axis=0) if back else staged
        conv = conv + shifted[_XP_BASE:, :] * convw_ref[i:i + 1, :]
    xp_ref[_XP_BASE - _HALO:_XP_BASE, :] = xp_ref[_XP_BASE + lc - _HALO:_XP_BASE + lc, :]
    act = conv * _sigmoid(conv)
    xs = act[:, :SSD_INNER]
    bm = act[:, SSD_INNER:SSD_INNER + SSD_GROUPS * D_STATE]
    cm = act[:, SSD_INNER + SSD_GROUPS * D_STATE:]

    dt = _softplus(sm_ref[0][:, SM_DT:SM_DT + SSD_HEADS] + hp_ref[0:1, :])
    dt_t = _softplus(dtT_ref[0] + hpT_ref[:, 0:1])
    if t_valid < lc:
        dt = jnp.where(lax.broadcasted_iota(I32, (lc, SSD_HEADS), 0) < t_valid, dt, 0.0)
        dt_t = jnp.where(lax.broadcasted_iota(I32, (SSD_HEADS, lc), 1) < t_valid, dt_t, 0.0)
    a = dt * (-jnp.exp(hp_ref[1:2, :]))
    a_t = dt_t * (-jnp.exp(hpT_ref[:, 1:2]))
    ri = lax.broadcasted_iota(I32, (lc, lc), 0)
    ci = lax.broadcasted_iota(I32, (lc, lc), 1)
    causal = ci <= ri
    acum = _dot_exact(causal.astype(F32), a)
    acum_t = _dot_exact(a_t, (ri <= ci).astype(F32))
    e_a = jnp.exp(acum)
    w_end = jnp.exp(acum[lc - 1:lc, :] - acum) * dt
    lane = lax.broadcasted_iota(I32, (lc, 2 * SSD_HEAD_DIM), 1)
    first_head = lane < SSD_HEAD_DIM
    gw = HEADS_PER_GROUP * SSD_HEAD_DIM

    def per_head_lanes(v, h0):
        return jnp.where(first_head, v[:, h0:h0 + 1], v[:, h0 + 1:h0 + 2])

    for g in range(SSD_GROUPS):
        b_g = bm[:, g * D_STATE:(g + 1) * D_STATE]
        c_g = cm[:, g * D_STATE:(g + 1) * D_STATE].astype(MXU_DTYPE)
        b_gt = b_g.T.astype(MXU_DTYPE)
        cb = jnp.dot(c_g, b_gt, preferred_element_type=F32)
        state = st_ref[g]
        y_off = _dot(c_g, state)
        xw, e_last = [], []
        for jp in range(HEADS_PER_GROUP // 2):
            h0 = g * HEADS_PER_GROUP + 2 * jp
            ms = []
            for h in (h0, h0 + 1):
                seg = acum[:, h:h + 1] - acum_t[h:h + 1, :]
                decay = jnp.exp(jnp.where(causal, seg, -jnp.inf))
                ms.append((cb * decay * dt_t[h:h + 1, :]).astype(MXU_DTYPE))
            c0 = h0 * SSD_HEAD_DIM
            x_pair = xs[:, c0:c0 + 2 * SSD_HEAD_DIM]
            r = jnp.dot(jnp.concatenate(ms, axis=0), x_pair.astype(MXU_DTYPE), preferred_element_type=F32)
            y_diag = jnp.where(first_head, r[:lc], r[lc:])
            o0 = 2 * jp * SSD_HEAD_DIM
            e_pair = per_head_lanes(e_a, h0)
            y_ref[:, c0:c0 + 2 * SSD_HEAD_DIM] = (y_diag + y_off[:, o0:o0 + 2 * SSD_HEAD_DIM] * e_pair
                                                  + dfull_ref[:, c0:c0 + 2 * SSD_HEAD_DIM] * x_pair)
            xw.append((x_pair * per_head_lanes(w_end, h0)).astype(MXU_DTYPE))
            e_last.append(e_pair[lc - 1:lc, :])
        st_ref[g] = (state * jnp.concatenate(e_last, axis=1)
                     + jnp.dot(b_gt, jnp.concatenate(xw, axis=1), preferred_element_type=F32))

    zv = z_ref[0]
    hg = y_ref[...] * (zv * _sigmoid(zv))
    for g in range(SSD_GROUPS):
        sl = hg[:, g * gw:(g + 1) * gw]
        ms = jnp.mean(sl * sl, axis=-1, keepdims=True)
        yn_ref[0, :, g * gw:(g + 1) * gw] = (sl * lax.rsqrt(ms + RMS_EPS) * nw_ref[:, g * gw:(g + 1) * gw]).astype(yn_ref.dtype)

    @pl.when(c == nc - 1)
    def _():
        ssm_out_ref[0] = st_ref[...]


def _ssd(xbc, z, sm, dt_t, conv0, ssm0_t, conv_w, conv_b, dt_bias, a_log, d_skip, norm_w, lc, t_valid):
    b, t, _ = xbc.shape
    nc = t // lc
    hp = jnp.stack([dt_bias, a_log]).astype(F32)
    d_full = jnp.repeat(d_skip.astype(F32), SSD_HEAD_DIM)[None, :]
    gw = HEADS_PER_GROUP * SSD_HEAD_DIM
    const2 = lambda bi, ci: (0, 0)
    kern = functools.partial(_ssd_kernel, lc=lc, t_valid=t_valid)
    return pl.pallas_call(
        kern,
        grid=(b, nc),
        in_specs=[pl.BlockSpec((1, lc, CONV_DIM), lambda bi, ci: (bi, ci, 0)),
                  pl.BlockSpec((1, lc, SSD_INNER), lambda bi, ci: (bi, ci, 0)),
                  pl.BlockSpec((1, lc, LANES), lambda bi, ci: (bi, ci, 0)),
                  pl.BlockSpec((1, SSD_HEADS, lc), lambda bi, ci: (bi, 0, ci)),
                  pl.BlockSpec((1, _HALO, CONV_DIM), lambda bi, ci: (bi, 0, 0)),
                  pl.BlockSpec((1, SSD_GROUPS, D_STATE, gw), lambda bi, ci: (bi, 0, 0, 0)),
                  pl.BlockSpec((CONV_W, CONV_DIM), const2),
                  pl.BlockSpec((1, CONV_DIM), const2),
                  pl.BlockSpec((2, SSD_HEADS), const2),
                  pl.BlockSpec((SSD_HEADS, 2), const2),
                  pl.BlockSpec((1, SSD_INNER), const2),
                  pl.BlockSpec((1, SSD_INNER), const2)],
        out_specs=(pl.BlockSpec((1, lc, SSD_INNER), lambda bi, ci: (bi, ci, 0)),
                   pl.BlockSpec((1, SSD_GROUPS, D_STATE, gw), lambda bi, ci: (bi, 0, 0, 0))),
        out_shape=(jax.ShapeDtypeStruct((b, t, SSD_INNER), MXU_DTYPE),
                   jax.ShapeDtypeStruct((b, SSD_GROUPS, D_STATE, gw), F32)),
        scratch_shapes=[pltpu.VMEM((_XP_BASE + lc, CONV_DIM), F32),
                        pltpu.VMEM((SSD_GROUPS, D_STATE, gw), F32),
                        pltpu.VMEM((lc, SSD_INNER), F32)],
        compiler_params=_cparams(2),
        name="ssd",
    )(xbc, z, sm, dt_t, conv0, ssm0_t, conv_w.astype(F32), conv_b.astype(F32)[None, :], hp, hp.T,
      d_full, norm_w.astype(F32)[None, :])


def _state_to_kernel_layout(s):
    b = s.shape[0]
    s = s.reshape(b, SSD_GROUPS, HEADS_PER_GROUP, SSD_HEAD_DIM, D_STATE)
    return s.transpose(0, 1, 4, 2, 3).reshape(b, SSD_GROUPS, D_STATE, HEADS_PER_GROUP * SSD_HEAD_DIM)


def _state_from_kernel_layout(s):
    b = s.shape[0]
    s = s.reshape(b, SSD_GROUPS, D_STATE, HEADS_PER_GROUP, SSD_HEAD_DIM)
    return s.transpose(0, 1, 3, 4, 2).reshape(b, SSD_HEADS, SSD_HEAD_DIM, D_STATE)


_T5_LOG_STARTS = (12, 16, 23, 32, 46, 64, 91)
_T5_FAR_BUCKET = REL_BUCKETS // 2 - 1


def _bias_kernel(rb_ref, out_ref, *, tk):
    half = REL_BUCKETS // 2
    max_exact = half // 2
    ri = lax.broadcasted_iota(I32, (tk, tk), 0)
    ci = lax.broadcasted_iota(I32, (tk, tk), 1)
    for t in range(2):
        rel = ci - ri - (1 - t) * tk
        n = jnp.abs(rel)
        large = jnp.full((tk, tk), max_exact, I32)
        for start in _T5_LOG_STARTS:
            large = large + (n >= start).astype(I32)
        bucket = jnp.where(rel > 0, half, 0) + jnp.where(n < max_exact, n, large)
        for h in range(N_HEADS):
            acc = jnp.zeros((tk, tk), F32)
            for bkt in range(REL_BUCKETS):
                acc = jnp.where(bucket == bkt, rb_ref[bkt, h], acc)
            out_ref[t, h] = acc - rb_ref[_T5_FAR_BUCKET, h]


def _bias_tiles(rel_bias, tk):
    return pl.pallas_call(
        functools.partial(_bias_kernel, tk=tk),
        in_specs=[pl.BlockSpec(memory_space=pltpu.SMEM)],
        out_specs=pl.BlockSpec(memory_space=pltpu.VMEM),
        out_shape=jax.ShapeDtypeStruct((2, N_HEADS, tk, tk), F32),
        name="bias_tiles",
    )(rel_bias.astype(F32))


_MAX_BISECTIONS = 40
_SCORE_COLS = 256
_STEPS_PER_CHECK = 2


def _select_kernel(qi_ref, sm_ref, kit_ref, mask_ref, key_ref, w_ref, *, tq, wk, s_len, q_start, n_sel):
    qb = pl.program_id(1)
    n_blocks = mask_ref.shape[2] // wk
    nt = wk // LANES
    p0 = q_start + qb * tq
    pos = p0 + lax.broadcasted_iota(I32, (tq, 1), 0)
    vis_end = (pos // CHUNK + 1) * CHUNK
    vis_max = jnp.minimum(((p0 + tq - 1) // CHUNK + 1) * CHUNK, s_len)
    nkb = (vis_max + wk - 1) // wk

    wi = sm_ref[0][:, SM_WI:SM_WI + IDX_HEADS] * IDX_HEADS ** -0.5
    for hh in range(IDX_HEADS):
        w_ref[hh] = jnp.broadcast_to(wi[:, hh:hh + 1], (tq, LANES))
    lane_k = lax.broadcasted_iota(I32, (1, LANES), 1)

    def to_key(score):
        bits = lax.bitcast_convert_type(score, I32)
        return jnp.where(bits < 0, INT_MIN - bits, bits)

    def from_key(key):
        return lax.bitcast_convert_type(jnp.where(key < 0, INT_MIN - key, key), F32)

    def score_block(j, top, masked):
        m1, m2 = top
        k0 = pl.multiple_of(j * wk, wk)
        for c in range(wk // _SCORE_COLS):
            c0 = k0 + c * _SCORE_COLS
            kt = kit_ref[0, :, pl.ds(c0, _SCORE_COLS)]
            d = jnp.maximum(jnp.dot(qi_ref[0, 0], kt, preferred_element_type=F32), 0.0)
            for u in range(_SCORE_COLS // LANES):
                sc = jnp.zeros((tq, LANES), F32)
                for hh in range(IDX_HEADS):
                    sc = sc + w_ref[hh] * d[hh * tq:(hh + 1) * tq, u * LANES:(u + 1) * LANES]
                if masked:
                    key_pos = c0 + u * LANES + lane_k
                    sc = jnp.where((key_pos < vis_end) & (key_pos < s_len), sc, -jnp.inf)
                sc = jnp.where(sc == 0.0, 0.0, sc)
                key_ref[j * nt + c * (_SCORE_COLS // LANES) + u] = sc
                m2 = jnp.maximum(m2, jnp.minimum(m1, sc))
                m1 = jnp.maximum(m1, sc)
        return m1, m2

    n_open = jnp.minimum(p0, s_len) // wk
    top = (jnp.full((tq, LANES), -jnp.inf, F32),) * 2
    top = lax.fori_loop(0, n_open, functools.partial(score_block, masked=False), top)
    m1, m2 = lax.fori_loop(n_open, nkb, functools.partial(score_block, masked=True), top)

    def count_ge(thr):
        thr_b = jnp.broadcast_to(from_key(thr), (tq, LANES))

        def body(j, below):
            k0 = pl.multiple_of(j * wk, wk)
            for u in range(nt):
                diff = key_ref[j * nt + u] - thr_b
                below = below + lax.shift_right_logical(lax.bitcast_convert_type(diff, I32), 31)
            return below

        below = lax.fori_loop(0, nkb, body, jnp.zeros((tq, LANES), I32))
        return nkb * wk - jnp.sum(below, axis=1, keepdims=True)

    def unfinished(lo, hi, cnt_lo):
        return (cnt_lo > n_sel) & (hi - 1 > lo)

    def bisect(lo, hi, cnt_lo):
        def cond(st):
            return (st[0] < _MAX_BISECTIONS) & (st[1] > 0)

        def body(st):
            it, _, lo, hi, cnt_lo = st
            for _ in range(_STEPS_PER_CHECK):
                live = unfinished(lo, hi, cnt_lo)
                mid = (lo >> 1) + (hi >> 1) + (lo & hi & 1)
                cnt = count_ge(jnp.where(live, mid, lo))
                up = live & (cnt >= n_sel)
                lo = jnp.where(up, mid, lo)
                cnt_lo = jnp.where(up, cnt, cnt_lo)
                hi = jnp.where(live & (cnt < n_sel), mid, hi)
            return (it + 1, jnp.max(unfinished(lo, hi, cnt_lo).astype(I32)), lo, hi, cnt_lo)

        st0 = (jnp.int32(0), jnp.max(unfinished(lo, hi, cnt_lo).astype(I32)), lo, hi, cnt_lo)
        return lax.while_loop(cond, body, st0)[2:]

    lo = jnp.maximum(to_key(jnp.min(m2, axis=1, keepdims=True)), KEY_NEG_INF + 1)
    hi = to_key(jnp.max(m1, axis=1, keepdims=True)) + 1
    tau, _, cnt_tau = bisect(lo, hi, count_ge(lo))

    tie_rows = cnt_tau > n_sel
    has_ties = jnp.max(tie_rows.astype(I32)) > 0

    @pl.when(jnp.logical_not(has_ties))
    def _():
        tau_b = jnp.broadcast_to(from_key(tau), (tq, LANES))

        def body(j, carry):
            k0 = pl.multiple_of(j * wk, wk)
            for u in range(nt):
                sl = pl.ds(k0 + u * LANES, LANES)
                mask_ref[0, :, sl] = (key_ref[j * nt + u] >= tau_b).astype(I32).astype(mask_ref.dtype)
            return carry
        lax.fori_loop(0, nkb, body, 0)

    @pl.when(has_ties)
    def _():
        upper = (lax.broadcasted_iota(I32, (wk, wk), 0) <= lax.broadcasted_iota(I32, (wk, wk), 1)).astype(MXU_DTYPE)
        need = jnp.where(tie_rows, n_sel - count_ge(tau + 1), wk * n_blocks).astype(F32)
        tau_f = from_key(tau)

        def body(j, seen):
            k0 = pl.multiple_of(j * wk, wk)
            blk = jnp.concatenate([key_ref[j * nt + u] for u in range(nt)], axis=1)
            eq = blk == tau_f
            rank = seen + jnp.dot(eq.astype(F32).astype(MXU_DTYPE), upper, preferred_element_type=F32)
            sel = (blk > tau_f) | (eq & (rank <= need))
            mask_ref[0, :, pl.ds(k0, wk)] = sel.astype(I32).astype(mask_ref.dtype)
            return seen + jnp.sum(eq.astype(F32), axis=1, keepdims=True)
        lax.fori_loop(0, nkb, body, jnp.zeros((tq, 1), F32))

    def zero_body(j, carry):
        k0 = pl.multiple_of(j * wk, wk)
        mask_ref[0, :, pl.ds(k0, wk)] = jnp.zeros((tq, wk), mask_ref.dtype)
        return carry

    lax.fori_loop(nkb, n_blocks, zero_body, 0)


def _select(qi, sm, kit, tq, wk, s_len, q_start, n_sel):
    b, t, _ = qi.shape
    s_pad = kit.shape[2]
    assert n_sel <= 2 * LANES, "the starting bracket takes two candidates per lane column"
    kern = functools.partial(_select_kernel, tq=tq, wk=wk, s_len=s_len, q_start=q_start, n_sel=n_sel)
    qi = qi.reshape(b, t // tq, tq, IDX_HEADS, IDX_DIM).transpose(0, 1, 3, 2, 4).reshape(b, t // tq, IDX_HEADS * tq, IDX_DIM)
    return pl.pallas_call(
        kern,
        grid=(b, t // tq),
        in_specs=[pl.BlockSpec((1, 1, IDX_HEADS * tq, IDX_DIM), lambda bi, qi_: (bi, qi_, 0, 0)),
                  pl.BlockSpec((1, tq, LANES), lambda bi, qi_: (bi, qi_, 0)),
                  _per_batch((1, IDX_DIM, s_pad), lambda bi, qi_: (bi, 0, 0), t // tq)],
        out_specs=pl.BlockSpec((1, tq, s_pad), lambda bi, qi_: (bi, qi_, 0)),
        out_shape=jax.ShapeDtypeStruct((b, t, s_pad), jnp.int8),
        scratch_shapes=[pltpu.VMEM((s_pad // LANES, tq, LANES), F32),
                        pltpu.VMEM((IDX_HEADS, tq, LANES), F32)],
        compiler_params=_cparams(2),
        name="select",
    )(qi, sm, kit)


_V_AUG = 2 * HEAD_DIM
_V_ROWS = 512
_MID_TILES = (1024, 512)


def _attn_kernel(q_ref, kt_ref, v_ref, mask_ref, bias_ref, o_ref, m_ref, acc_ref, qg_ref, vaug_ref,
                 *, tq, tk, far_widths, q_start):
    qb = pl.program_id(1)
    jd = (q_start + qb * tq) // tk

    @pl.when(qb == 0)
    def _():
        one_hot = (lax.broadcasted_iota(I32, (_V_ROWS, _V_AUG - HEAD_DIM), 1) == 0).astype(vaug_ref.dtype)

        def body(i, carry):
            r0 = pl.multiple_of(i * _V_ROWS, _V_ROWS)
            blk = v_ref[0, pl.ds(r0, _V_ROWS), :]
            for g in range(N_KV_HEADS):
                vaug_ref[pl.ds(r0, _V_ROWS), g * _V_AUG:g * _V_AUG + HEAD_DIM] = blk[:, g * HEAD_DIM:(g + 1) * HEAD_DIM]
                vaug_ref[pl.ds(r0, _V_ROWS), g * _V_AUG + HEAD_DIM:(g + 1) * _V_AUG] = one_hot
            return carry

        lax.fori_loop(0, vaug_ref.shape[0] // _V_ROWS, body, 0)

    for g in range(N_KV_HEADS):
        for r in range(Q_PER_KV):
            c0 = (g * Q_PER_KV + r) * HEAD_DIM
            qg_ref[g, r * tq:(r + 1) * tq, :] = q_ref[0, :, c0:c0 + HEAD_DIM]
    m_ref[...] = jnp.full(m_ref.shape, NEG_BIG, F32)
    acc_ref[...] = jnp.zeros(acc_ref.shape, F32)

    def process(k0, width, near):
        nt = width // LANES
        mb = jnp.where(mask_ref[0, :, pl.ds(k0, width)].astype(I32) != 0, 0.0, NEG_BIG)
        mb = jnp.concatenate([mb] * Q_PER_KV, axis=0)
        for g in range(N_KV_HEADS):
            kt = kt_ref[0, g * HEAD_DIM:(g + 1) * HEAD_DIM, pl.ds(k0, width)]
            s = jnp.dot(qg_ref[g], kt, preferred_element_type=F32) + mb
            if near is not None:
                s = s + bias_ref[near, g]
            tiles = [s[:, u * LANES:(u + 1) * LANES] for u in range(nt)]
            mx = tiles[0]
            for u in range(1, nt):
                mx = jnp.maximum(mx, tiles[u])
            m_old = m_ref[g]
            m_new = jnp.maximum(m_old, jnp.max(mx, axis=1, keepdims=True))
            alpha = jnp.exp(m_old - m_new)
            p = jnp.concatenate([jnp.exp(tl - m_new).astype(MXU_DTYPE) for tl in tiles], axis=1)
            vt = vaug_ref[pl.ds(k0, width), g * _V_AUG:(g + 1) * _V_AUG]
            acc_ref[g] = alpha * acc_ref[g] + jnp.dot(p, vt, preferred_element_type=F32)
            m_ref[g] = m_new

    far_keys = jnp.maximum(jd - 1, 0) * tk
    done = 0
    for width in far_widths:
        steps = (far_keys - done) // width

        def far_body(i, carry, width=width, done=done):
            process(pl.multiple_of(done + i * width, width), width, None)
            return carry

        lax.fori_loop(0, steps, far_body, 0)
        done = done + steps * width

    @pl.when(jd >= 1)
    def _():
        process(pl.multiple_of((jd - 1) * tk, tk), tk, 0)

    process(pl.multiple_of(jd * tk, tk), tk, 1)

    for g in range(N_KV_HEADS):
        acc = acc_ref[g]
        o = acc[:, :HEAD_DIM] / acc[:, HEAD_DIM:HEAD_DIM + 1]
        for r in range(Q_PER_KV):
            c0 = (g * Q_PER_KV + r) * HEAD_DIM
            o_ref[0, :, c0:c0 + HEAD_DIM] = o[r * tq:(r + 1) * tq].astype(o_ref.dtype)


def _attention(q, kt, v, mask, bias, tq, tk, wide, q_start):
    b, t, _ = q.shape
    s_pad = kt.shape[2]
    assert s_pad % _V_ROWS == 0
    rows = Q_PER_KV * tq
    far_widths = tuple(sorted({wide, tk} | {w for w in _MID_TILES if tk < w < wide}, reverse=True))
    assert all(a % b == 0 for a, b in zip(far_widths, far_widths[1:]))
    kern = functools.partial(_attn_kernel, tq=tq, tk=tk, far_widths=far_widths, q_start=q_start)
    return pl.pallas_call(
        kern,
        grid=(b, t // tq),
        in_specs=[pl.BlockSpec((1, tq, N_HEADS * HEAD_DIM), lambda bi, qi_: (bi, qi_, 0)),
                  _per_batch((1, KV_DIM, s_pad), lambda bi, qi_: (bi, 0, 0), t // tq),
                  _per_batch((1, s_pad, KV_DIM), lambda bi, qi_: (bi, 0, 0), t // tq),
                  pl.BlockSpec((1, tq, s_pad), lambda bi, qi_: (bi, qi_, 0)),
                  _resident((2, N_KV_HEADS, rows, tk), lambda bi, qi_: (0, 0, 0, 0))],
        out_specs=pl.BlockSpec((1, tq, N_HEADS * HEAD_DIM), lambda bi, qi_: (bi, qi_, 0)),
        out_shape=jax.ShapeDtypeStruct((b, t, N_HEADS * HEAD_DIM), MXU_DTYPE),
        scratch_shapes=[pltpu.VMEM((N_KV_HEADS, rows, LANES), F32),
                        pltpu.VMEM((N_KV_HEADS, rows, _V_AUG), F32),
                        pltpu.VMEM((N_KV_HEADS, rows, HEAD_DIM), MXU_DTYPE),
                        pltpu.VMEM((s_pad, N_KV_HEADS * _V_AUG), MXU_DTYPE)],
        compiler_params=_cparams(2),
        name="attention",
    )(q, kt, v, mask, bias)


def _layer_norm(x, g, b):
    mu = jnp.mean(x, axis=-1, keepdims=True)
    xc = x - mu
    var = jnp.mean(xc * xc, axis=-1, keepdims=True)
    return xc * lax.rsqrt(var + LN_EPS) * g + b


def _post_kernel(o_ref, yn_ref, gs_ref, ga_ref, x_ref, wso_ref, wao_ref, wout_ref, wup_ref, wdown_ref,
                 ln_ref, out_ref):
    y1 = jnp.dot(yn_ref[...], wso_ref[...], preferred_element_type=F32)
    y2 = jnp.dot(o_ref[...], wao_ref[...], preferred_element_type=F32)
    mix = _sigmoid(gs_ref[...]) * y1 + _sigmoid(ga_ref[...]) * y2
    mixed = _dot(mix, wout_ref[...])
    h = _layer_norm(ALPHA * x_ref[...] + mixed, ln_ref[0:1, :], ln_ref[1:2, :])
    u = jnp.maximum(_dot(h, wup_ref[...]), 0.0)
    f = _dot(u * u, wdown_ref[...])
    out_ref[...] = _layer_norm(ALPHA * h + f, ln_ref[2:3, :], ln_ref[3:4, :])


def _post(o, yn, gs, ga, x2d, w_ssd_o, w_attn_o, w_out, w_up, w_down, ln, tm):
    m = x2d.shape[0]
    row = lambda w: pl.BlockSpec((tm, w), lambda i: (i, 0))
    const = lambda i: (0, 0)
    return pl.pallas_call(
        _post_kernel,
        grid=(m // tm,),
        in_specs=[row(N_HEADS * HEAD_DIM), row(SSD_INNER), row(D_MODEL), row(D_MODEL), row(D_MODEL),
                  _resident((SSD_INNER, D_MODEL), const),
                  _resident((N_HEADS * HEAD_DIM, D_MODEL), const),
                  _resident((D_MODEL, D_MODEL), const),
                  _resident((D_MODEL, D_FF), const),
                  _resident((D_FF, D_MODEL), const),
                  _resident((4, D_MODEL), const)],
        out_specs=row(D_MODEL),
        out_shape=jax.ShapeDtypeStruct((m, D_MODEL), F32),
        compiler_params=_cparams(1),
        name="post",
    )(o, yn, gs, ga, x2d, w_ssd_o, w_attn_o, w_out, w_up, w_down, ln)


def _round_up(n, m):
    return (n + m - 1) // m * m


def _layer(x, cache_k, cache_v, cache_kidx, state_ssm, state_conv, bias_tiles, w_prep, conv_w, conv_b,
           dt_bias, a_log, d_skip, ssd_norm_w, w_ssd_o, w_attn_o, w_out, w_up, w_down, ln,
           *, tq, tk, wide, tqs, wk, lc, tm):
    b, t, _ = x.shape
    past = cache_k.shape[1]
    s_len = past + t
    m = b * t
    assert m % tm == 0 and t % tq == 0 and past % tk == 0 and tk % LANES == 0 and wide % tk == 0
    assert t % tqs == 0 and wk % LANES == 0
    assert tq == tk or t == tq, "every query tile must start on a key-tile boundary"
    assert t % lc == 0 or t < lc
    x2d = x.reshape(m, D_MODEL)

    z, xbc, q, k, v, qi, gs, ga, sm, kb, vb = _inproj(x2d, w_prep, tm)

    t_pad = _round_up(t, lc)
    pad_t = lambda a: jnp.pad(a.reshape(b, t, a.shape[-1]), ((0, 0), (0, t_pad - t), (0, 0)))
    sm3 = pad_t(sm)
    dt_t = sm3[:, :, SM_DT:SM_DT + SSD_HEADS].transpose(0, 2, 1)
    yn, ssm_new = _ssd(pad_t(xbc), pad_t(z), sm3, dt_t, state_conv.astype(F32),
                       _state_to_kernel_layout(state_ssm.astype(F32)), conv_w, conv_b, dt_bias, a_log, d_skip,
                       ssd_norm_w, lc, min(t, lc))
    yn = yn[:, :t].reshape(m, SSD_INNER)
    new_ssm = _state_from_kernel_layout(ssm_new)
    xbc3 = xbc.reshape(b, t, CONV_DIM)
    new_conv = jnp.concatenate([state_conv.astype(F32), xbc3], axis=1)[:, -_HALO:]

    k3 = k.reshape(b, t, KV_DIM)
    v3 = v.reshape(b, t, KV_DIM)
    ki3 = sm.reshape(b, t, LANES)[:, :, SM_KI:SM_KI + IDX_DIM]
    s_pad = _round_up(s_len, wk)
    assert s_pad % tk == 0
    pad_s = lambda a: jnp.pad(a, ((0, 0), (0, s_pad - s_len)) + ((0, 0),) * (a.ndim - 2))
    k_all = pad_s(jnp.concatenate([cache_k.reshape(b, past, KV_DIM).astype(MXU_DTYPE), kb.reshape(b, t, KV_DIM)], axis=1))
    v_all = pad_s(jnp.concatenate([cache_v.reshape(b, past, KV_DIM).astype(MXU_DTYPE), vb.reshape(b, t, KV_DIM)], axis=1))
    ki_all = pad_s(jnp.concatenate([cache_kidx.astype(MXU_DTYPE), ki3.astype(MXU_DTYPE)], axis=1))
    n_sel = min(TOPK_MAX, s_len // 4)
    mask = _select(qi.reshape(b, t, -1), sm.reshape(b, t, LANES), ki_all.transpose(0, 2, 1),
                   tqs, wk, s_len, past, n_sel)
    bias = bias_tiles[:, :, :tq, :].reshape(2, N_KV_HEADS, Q_PER_KV * tq, tk)
    o = _attention(q.reshape(b, t, -1), k_all.transpose(0, 2, 1), v_all, mask, bias, tq, tk, wide, past)

    out = _post(o.reshape(m, -1), yn, gs, ga, x2d, w_ssd_o, w_attn_o, w_out, w_up, w_down, ln, tm)
    return out.reshape(b, t, D_MODEL), (k3.reshape(b, t, N_KV_HEADS, HEAD_DIM), v3.reshape(b, t, N_KV_HEADS, HEAD_DIM),
                                        ki3, new_ssm, new_conv)


ATTN_TILE = 128
WIDE_TILE = 2048
SELECT_ROWS = 128
SELECT_WIDTH = 1024
SSD_CHUNK = 128
ROW_TILE = 256


def kernel(x_prompt, x_sample, cache_k, cache_v, cache_kidx, state_ssm, state_conv, rel_bias, w_in, conv_w, conv_b,
           dt_bias, a_log, d_skip, ssd_norm_w, w_ssd_o, w_attn_o, w_out, ln1_g, ln1_b, w_up, w_down, ln2_g, ln2_b):
    bp = x_prompt.shape[0]
    dtype = x_prompt.dtype
    bias_tiles = _bias_tiles(rel_bias, ATTN_TILE)
    yp, ys = x_prompt, x_sample
    st_p, st_s = [], []
    tiles = dict(tk=ATTN_TILE, wide=WIDE_TILE, wk=SELECT_WIDTH, lc=SSD_CHUNK, tm=ROW_TILE)
    for l in range(DEPTH):
        w_prep = _prep_w_in(w_in[l])
        ln = jnp.stack([ln1_g[l], ln1_b[l], ln2_g[l], ln2_b[l]]).astype(F32)
        weights = (bias_tiles, w_prep, conv_w[l], conv_b[l], dt_bias[l], a_log[l], d_skip[l], ssd_norm_w[l],
                   w_ssd_o[l].astype(MXU_DTYPE), w_attn_o[l].astype(MXU_DTYPE), w_out[l].astype(MXU_DTYPE),
                   w_up[l].astype(MXU_DTYPE), w_down[l].astype(MXU_DTYPE), ln)
        yp, sp = _layer(yp, jnp.zeros((bp, 0, N_KV_HEADS, HEAD_DIM), dtype), jnp.zeros((bp, 0, N_KV_HEADS, HEAD_DIM), dtype),
                        jnp.zeros((bp, 0, IDX_DIM), dtype), jnp.zeros((bp, SSD_HEADS, SSD_HEAD_DIM, D_STATE), dtype),
                        jnp.zeros((bp, _HALO, CONV_DIM), dtype), *weights,
                        tq=min(ATTN_TILE, yp.shape[1]), tqs=min(SELECT_ROWS, yp.shape[1]), **tiles)
        ys, ss = _layer(ys, cache_k[l], cache_v[l], cache_kidx[l], state_ssm[l], state_conv[l], *weights,
                        tq=min(ATTN_TILE, ys.shape[1]), tqs=min(SELECT_ROWS, ys.shape[1]), **tiles)
        st_p.append(sp)
        st_s.append(ss)
    stack = lambda sts, i: jnp.stack([s[i] for s in sts])
    return (yp, ys) + tuple(stack(st_p, i) for i in range(5)) + tuple(stack(st_s, i) for i in range(5))
```
